```python
import jax, jax.numpy as jnp
from jax import lax
import numpy as np

D_MODEL = 1024
BATCH = 4
SEQ = 4096
DEPTH = 2

N_META = 16
CHUNK = 128
RET_HEADS = 8
RET_DK = D_MODEL // (2 * RET_HEADS)
RET_DV = 2 * RET_DK
SB_HEADS = 8
SB_DH = D_MODEL // (2 * SB_HEADS)
D_FF = ((8 * D_MODEL // 3 + 255) // 256) * 256
ROPE_BASE = 10000.0
EPS = 1e-6

COL_SIZES = [RET_HEADS * RET_DK, RET_HEADS * RET_DK, RET_HEADS * RET_DV, RET_HEADS * RET_DV,
             SB_HEADS * SB_DH, SB_HEADS * SB_DH, SB_HEADS * SB_DH, D_MODEL, D_MODEL]
D_IN = int(sum(COL_SIZES))
SPLITS = [int(s) for s in np.cumsum(COL_SIZES)[:-1]]

kernel_name = "hybrid_retention_stickbreaking_macaron"


def rmsnorm(x, g):
    xf = x.astype(jnp.float32)
    r = lax.rsqrt(jnp.mean(xf * xf, axis=-1, keepdims=True) + EPS)
    return (xf * r).astype(x.dtype) * g


def swiglu(x, w_gu, w_down):
    a, b = jnp.split(x @ w_gu, 2, axis=-1)
    return (jax.nn.silu(a) * b) @ w_down


def rotary(t, pos):
    dh = t.shape[-1]
    freqs = ROPE_BASE ** (-jnp.arange(0, dh, 2, dtype=jnp.float32) / dh)
    ang = pos[:, None] * freqs[None, :]
    cos, sin = jnp.cos(ang), jnp.sin(ang)
    tf = t.astype(jnp.float32)
    t1, t2 = tf[..., : dh // 2], tf[..., dh // 2:]
    return jnp.concatenate([t1 * cos - t2 * sin, t1 * sin + t2 * cos], axis=-1).astype(t.dtype)


def retention(q, k, v, valid):
    B, H, L, dk = q.shape
    dv = v.shape[-1]
    N = L // CHUNK
    log_gamma = jnp.log(1.0 - 2.0 ** (-5.0 - jnp.arange(H, dtype=jnp.float32)))
    vmask = valid.astype(jnp.float32)[None, None, :, None]
    qf = q.astype(jnp.float32).reshape(B, H, N, CHUNK, dk)
    kf = (k.astype(jnp.float32) * vmask * dk ** -0.5).reshape(B, H, N, CHUNK, dk)
    vf = (v.astype(jnp.float32) * vmask).reshape(B, H, N, CHUNK, dv)
    idx = jnp.arange(CHUNK, dtype=jnp.float32)
    diff = idx[:, None] - idx[None, :]
    decay = jnp.where(diff >= 0, jnp.exp(log_gamma[:, None, None] * jnp.maximum(diff, 0.0)), 0.0)
    scores = jnp.einsum('bhncd,bhnsd->bhncs', qf, kf) * decay[None, :, None]
    o_intra = jnp.einsum('bhncs,bhnse->bhnce', scores, vf)
    k_decay = jnp.exp(log_gamma[:, None] * (CHUNK - 1 - idx)[None, :])
    kv = jnp.einsum('bhnsd,bhnse->bhnde', kf * k_decay[None, :, None, :, None], vf)
    chunk_decay = jnp.exp(log_gamma * CHUNK)[None, :, None, None]

    def step(R, kv_n):
        return R * chunk_decay + kv_n, R

    _, R_prev = lax.scan(step, jnp.zeros((B, H, dk, dv), jnp.float32), jnp.moveaxis(kv, 2, 0))
    R_prev = jnp.moveaxis(R_prev, 0, 2)
    q_decay = jnp.exp(log_gamma[:, None] * (idx + 1.0)[None, :])
    o_cross = jnp.einsum('bhncd,bhnde->bhnce', qf, R_prev) * q_decay[None, :, None, :, None]
    return (o_intra + o_cross).reshape(B, H, L, dv)


def stick_breaking(q, k, v, valid):
    B, H, L, dh = q.shape
    NB = L // CHUNK
    kf = k.astype(jnp.float32)
    vf = v.astype(jnp.float32)
    qb = jnp.moveaxis(q.astype(jnp.float32).reshape(B, H, NB, CHUNK, dh), 2, 0)
    kpos = jnp.arange(L)
    scale = dh ** -0.5

    def block(args):
        q_blk, n = args
        qpos = n * CHUNK + jnp.arange(CHUNK)
        z = jnp.einsum('bhqd,bhkd->bhqk', q_blk, kf) * scale
        mask = ((kpos[None, :] < qpos[:, None]) & valid[None, :])[None, None]
        log_1mb = jnp.where(mask, jax.nn.log_sigmoid(-z), 0.0)
        acc = lax.cumsum(log_1mb, axis=3, reverse=True) - log_1mb
        a = jnp.where(mask, jnp.exp(jax.nn.log_sigmoid(z) + acc), 0.0)
        return jnp.einsum('bhqk,bhkd->bhqd', a, vf)

    o = lax.map(block, (qb, jnp.arange(NB)))
    return jnp.moveaxis(o, 0, 2).reshape(B, H, L, dh)


def mixer(u, w_in, ret_gn, w_ret_proj, w_sb_proj, w_out, pos, valid):
    B, L, _ = u.shape
    rq, rk, rv, rg, sq, sk, sv, ga, gb = jnp.split(u @ w_in, SPLITS, axis=-1)

    def heads(t, n):
        return t.reshape(B, L, n, -1).transpose(0, 2, 1, 3)

    o_r = retention(rotary(heads(rq, RET_HEADS), pos), rotary(heads(rk, RET_HEADS), pos),
                    heads(rv, RET_HEADS), valid)
    mu = jnp.mean(o_r, axis=-1, keepdims=True)
    var = jnp.mean(jnp.square(o_r - mu), axis=-1, keepdims=True)
    o_r = ((o_r - mu) * lax.rsqrt(var + EPS)).transpose(0, 2, 1, 3).reshape(B, L, RET_HEADS * RET_DV)
    y_ret = (jax.nn.silu(rg) * (o_r.astype(u.dtype) * ret_gn)) @ w_ret_proj
    o_s = stick_breaking(heads(sq, SB_HEADS), heads(sk, SB_HEADS), heads(sv, SB_HEADS), valid)
    y_sb = o_s.transpose(0, 2, 1, 3).reshape(B, L, SB_HEADS * SB_DH).astype(u.dtype) @ w_sb_proj
    return (jax.nn.sigmoid(ga) * y_ret + jax.nn.sigmoid(gb) * y_sb) @ w_out


def setup_inputs(seed: int = 0) -> dict:
    key = jax.random.key(seed)
    ks = jax.random.split(key, 20)
    f32 = jnp.float32

    def w(k, shape, fan_in):
        return jax.random.normal(k, shape, f32) * fan_in ** -0.5

    def gain(k, shape):
        return 1.0 + 0.01 * jax.random.normal(k, shape, f32)

    return {
        "x": jax.random.normal(ks[0], (BATCH, SEQ, D_MODEL), f32),
        "meta": jax.random.normal(ks[1], (N_META, D_MODEL), f32),
        "ffn1_norm": gain(ks[2], (DEPTH, D_MODEL)),
        "ffn1_w_gu": w(ks[3], (DEPTH, D_MODEL, 2 * D_FF), D_MODEL),
        "ffn1_w_down": w(ks[4], (DEPTH, D_FF, D_MODEL), D_FF),
        "mix_norm": gain(ks[5], (DEPTH, D_MODEL)),
        "w_in": w(ks[6], (DEPTH, D_MODEL, D_IN), D_MODEL),
        "ret_gn": gain(ks[7], (DEPTH, RET_HEADS * RET_DV)),
        "w_ret_proj": w(ks[8], (DEPTH, RET_HEADS * RET_DV, D_MODEL), RET_HEADS * RET_DV),
        "w_sb_proj": w(ks[9], (DEPTH, SB_HEADS * SB_DH, D_MODEL), SB_HEADS * SB_DH),
        "w_out": w(ks[10], (DEPTH, D_MODEL, D_MODEL), D_MODEL),
        "ffn2_norm": gain(ks[11], (DEPTH, D_MODEL)),
        "ffn2_w_gu": w(ks[12], (DEPTH, D_MODEL, 2 * D_FF), D_MODEL),
        "ffn2_w_down": w(ks[13], (DEPTH, D_FF, D_MODEL), D_FF),
        "final_norm": gain(ks[14], (D_MODEL,)),
    }


def reference(x, meta, ffn1_norm, ffn1_w_gu, ffn1_w_down, mix_norm, w_in, ret_gn,
              w_ret_proj, w_sb_proj, w_out, ffn2_norm, ffn2_w_gu, ffn2_w_down, final_norm):
    B, S, D = x.shape
    pad = CHUNK - N_META
    h = jnp.concatenate([jnp.zeros((B, pad, D), x.dtype),
                         jnp.broadcast_to(meta.astype(x.dtype)[None], (B, N_META, D)), x], axis=1)
    L = S + CHUNK
    idx = jnp.arange(L)
    valid = idx >= pad
    pos = (idx - pad).astype(jnp.float32)
    for l in range(DEPTH):
        h = h + 0.5 * swiglu(rmsnorm(h, ffn1_norm[l]), ffn1_w_gu[l], ffn1_w_down[l])
        h = h + mixer(rmsnorm(h, mix_norm[l]), w_in[l], ret_gn[l], w_ret_proj[l],
                      w_sb_proj[l], w_out[l], pos, valid)
        h = h + 0.5 * swiglu(rmsnorm(h, ffn2_norm[l]), ffn2_w_gu[l], ffn2_w_down[l])
    h = rmsnorm(h, final_norm)
    return h[:, CHUNK:]
```

```python
import functools

import jax
import jax.numpy as jnp
import numpy as np
from jax import lax
from jax.experimental import pallas as pl
from jax.experimental.pallas import tpu as pltpu

D_MODEL = 1024
DEPTH = 2
N_META = 16
RET_HEADS = 8
RET_DK = 64
RET_DV = 128
SB_HEADS = 8
SB_DH = 64
D_FF = 2816
ROPE_BASE = 10000.0
EPS = 1e-6

LANES = 128
BLK = 256
PAD = BLK - N_META
FF_CHUNK = 256
ROW_TILE = 512
NEG_BIG = -1e30
VMEM_LIMIT = 56 * 1024 * 1024

_COLS = np.cumsum([0, 512, 512, 1024, 1024, 512, 512, 512, 1024, 1024])
C_RQ, C_RK, C_RV, C_RG, C_SQ, C_SK, C_SV, C_GA, C_GB, C_END = (int(c) for c in _COLS)

F32 = jnp.float32
BF16 = jnp.bfloat16


def _dot(a, b):
    return jnp.dot(a, b, preferred_element_type=F32)


def _dot_nt(a, b):
    return lax.dot_general(a, b, (((1,), (1,)), ((), ())), preferred_element_type=F32)


def _dot_tn(a, b):
    return lax.dot_general(a, b, (((0,), (0,)), ((), ())), preferred_element_type=F32)


def _rmsnorm(x, g):
    r = lax.rsqrt(jnp.mean(x * x, axis=-1, keepdims=True) + EPS)
    return (x * r) * g


def _sigmoid(x):
    return 1.0 / (1.0 + jnp.exp(-x))


def _params(*sem):
    return pltpu.CompilerParams(dimension_semantics=sem, vmem_limit_bytes=VMEM_LIMIT)


def _const_spec(shape):
    zeros = (0,) * len(shape)
    return pl.BlockSpec(shape, lambda *_: zeros)


def _ffn_kernel(x_ref, g_ref, wa_ref, wb_ref, wd_ref, gf_ref, o_ref, acc_ref, *, final_norm):
    x = x_ref[...]
    u = _rmsnorm(x, g_ref[...]).astype(BF16)
    acc_ref[...] = x

    def body(j, carry):
        a = _dot(u, wa_ref[j])
        b = _dot(u, wb_ref[j])
        gate = (0.5 * a * _sigmoid(a) * b).astype(BF16)
        acc_ref[...] += _dot(gate, wd_ref[j])
        return carry

    lax.fori_loop(0, D_FF // FF_CHUNK, body, 0)
    y = acc_ref[...]
    if final_norm:
        y = _rmsnorm(y, gf_ref[...])
    o_ref[...] = y


def _ffn(h, g, wa, wb, wd, gf, *, final_norm):
    rows = h.shape[0]
    nj = D_FF // FF_CHUNK
    return pl.pallas_call(
        functools.partial(_ffn_kernel, final_norm=final_norm),
        grid=(rows // ROW_TILE,),
        in_specs=[
            pl.BlockSpec((ROW_TILE, D_MODEL), lambda i: (i, 0)),
            _const_spec((1, D_MODEL)),
            _const_spec((nj, D_MODEL, FF_CHUNK)),
            _const_spec((nj, D_MODEL, FF_CHUNK)),
            _const_spec((nj, FF_CHUNK, D_MODEL)),
            _const_spec((1, D_MODEL)),
        ],
        out_specs=pl.BlockSpec((ROW_TILE, D_MODEL), lambda i: (i, 0)),
        out_shape=jax.ShapeDtypeStruct((rows, D_MODEL), F32),
        scratch_shapes=[pltpu.VMEM((ROW_TILE, D_MODEL), F32)],
        compiler_params=_params("parallel"),
        name="ffn_final" if final_norm else "ffn",
    )(h, g, wa, wb, wd, gf)


def _proj_kernel(x_ref, g_ref, w_ref, cos_ref, sin_ref, valid_ref,
                 rq_ref, rk_ref, rv_ref, sq_ref, sk_ref, sv_ref):
    u = _rmsnorm(x_ref[...], g_ref[...]).astype(BF16)
    cos = jnp.concatenate([cos_ref[...]] * (512 // LANES), axis=1)
    sin = jnp.concatenate([sin_ref[...]] * (512 // LANES), axis=1)
    valid = jnp.concatenate([valid_ref[...]] * (512 // LANES), axis=1)
    lane = lax.broadcasted_iota(jnp.int32, (ROW_TILE, 512), 1)
    first_half = (lane % RET_DK) < (RET_DK // 2)

    def rotary(t):
        partner = jnp.where(first_half, pltpu.roll(t, 512 - RET_DK // 2, 1),
                            pltpu.roll(t, RET_DK // 2, 1))
        return t * cos + partner * sin

    rq = _dot(u, w_ref[:, 0:512])
    rq_ref[...] = rotary(rq).astype(BF16)
    rk = _dot(u, w_ref[:, 512:1024])
    rk_ref[...] = (rotary(rk) * (valid * RET_DK ** -0.5)).astype(BF16)
    rv = _dot(u, w_ref[:, 1024:2048])
    rv_ref[...] = (rv * jnp.concatenate([valid, valid], axis=1)).astype(BF16)
    sq = _dot(u, w_ref[:, 2048:2560])
    sq_ref[...] = (sq * SB_DH ** -0.5).astype(BF16)
    sk_ref[...] = _dot(u, w_ref[:, 2560:3072]).astype(BF16)
    sv_ref[...] = _dot(u, w_ref[:, 3072:3584]).astype(BF16)


def _proj(h, g, w_qkv, cos_t, sin_t, valid_t):
    rows = h.shape[0]

    def row_spec(width):
        return pl.BlockSpec((ROW_TILE, width), lambda i: (i, 0))

    tab_spec = row_spec(LANES)
    widths = (512, 512, 1024, 512, 512, 512)
    return pl.pallas_call(
        _proj_kernel,
        grid=(rows // ROW_TILE,),
        in_specs=[row_spec(D_MODEL), _const_spec((1, D_MODEL)),
                  _const_spec((D_MODEL, 3584)), tab_spec, tab_spec, tab_spec],
        out_specs=[row_spec(w) for w in widths],
        out_shape=[jax.ShapeDtypeStruct((rows, w), BF16) for w in widths],
        compiler_params=_params("parallel"),
        name="mixer_proj",
    )(h, g, w_qkv, cos_t, sin_t, valid_t)


def _ret_kernel(lg_ref, q_ref, k_ref, v_ref, o_ref,
                state_ref, dec_ref, kdec_ref, qdec_ref, sdec_ref):
    c = pl.program_id(2)

    @pl.when(c == 0)
    def _init():
        lg = lg_ref[0]
        t = lax.broadcasted_iota(jnp.int32, (BLK, BLK), 0)
        s = lax.broadcasted_iota(jnp.int32, (BLK, BLK), 1)
        diff = (t - s).astype(F32)
        for i in range(2):
            dec_ref[i] = jnp.where(diff >= 0, jnp.exp(lg[i:i + 1, :] * jnp.maximum(diff, 0.0)), 0.0)
        lane = lax.broadcasted_iota(jnp.int32, (BLK, LANES), 1)
        pos = lax.broadcasted_iota(jnp.int32, (BLK, LANES), 0).astype(F32)
        lg_k = jnp.where(lane < RET_DK, lg[0:1, :LANES], lg[1:2, :LANES])
        kdec_ref[...] = jnp.exp(lg_k * (BLK - 1.0 - pos))
        col = lax.broadcasted_iota(jnp.int32, (BLK, 2 * RET_DV), 1)
        posv = lax.broadcasted_iota(jnp.int32, (BLK, 2 * RET_DV), 0).astype(F32)
        lg_v = jnp.where(col < RET_DV, lg[0:1, :], lg[1:2, :])
        qdec_ref[...] = jnp.exp(lg_v * (posv + 1.0))
        srow = lax.broadcasted_iota(jnp.int32, (LANES, 2 * RET_DV), 0)
        scol = lax.broadcasted_iota(jnp.int32, (LANES, 2 * RET_DV), 1)
        own = (srow // RET_DK) == (scol // RET_DV)
        lg_s = jnp.where(scol < RET_DV, lg[0:1, :], lg[1:2, :])
        sdec_ref[0] = jnp.where(own, jnp.exp(lg_s * float(BLK)), 0.0)
        sdec_ref[1] = jnp.where(own, 1.0, 0.0)
        state_ref[...] = jnp.zeros_like(state_ref)

    q = q_ref[0]
    k = k_ref[0]
    v = v_ref[0]
    state = state_ref[...]
    cross = _dot(q, state.astype(BF16)) * qdec_ref[...]
    lane = lax.broadcasted_iota(jnp.int32, (BLK, LANES), 1)
    for i in range(2):
        in_head = (lane < RET_DK) if i == 0 else (lane >= RET_DK)
        qi = jnp.where(in_head, q, jnp.zeros_like(q))
        scores = _dot_nt(qi, k) * dec_ref[i]
        o = _dot(scores.astype(BF16), v[:, i * RET_DV:(i + 1) * RET_DV])
        o = o + cross[:, i * RET_DV:(i + 1) * RET_DV]
        mu = jnp.mean(o, axis=-1, keepdims=True)
        d = o - mu
        var = jnp.mean(d * d, axis=-1, keepdims=True)
        o_ref[0, :, i * RET_DV:(i + 1) * RET_DV] = (d * lax.rsqrt(var + EPS)).astype(BF16)
    kd_t = (k.astype(F32) * kdec_ref[...]).T.astype(BF16)
    state_ref[...] = state * sdec_ref[0] + _dot(kd_t, v) * sdec_ref[1]


def _retention(lg_tab, rq, rk, rv):
    batch, seq_len, _ = rq.shape
    pairs = RET_HEADS // 2
    return pl.pallas_call(
        _ret_kernel,
        grid=(batch, pairs, seq_len // BLK),
        in_specs=[
            pl.BlockSpec((1, 2, BLK), lambda b, p, c: (p, 0, 0)),
            pl.BlockSpec((1, BLK, LANES), lambda b, p, c: (b, c, p)),
            pl.BlockSpec((1, BLK, LANES), lambda b, p, c: (b, c, p)),
            pl.BlockSpec((1, BLK, 2 * RET_DV), lambda b, p, c: (b, c, p)),
        ],
        out_specs=pl.BlockSpec((1, BLK, 2 * RET_DV), lambda b, p, c: (b, c, p)),
        out_shape=jax.ShapeDtypeStruct((batch, seq_len, RET_HEADS * RET_DV), BF16),
        scratch_shapes=[
            pltpu.VMEM((LANES, 2 * RET_DV), F32),
            pltpu.VMEM((2, BLK, BLK), F32),
            pltpu.VMEM((BLK, LANES), F32),
            pltpu.VMEM((BLK, 2 * RET_DV), F32),
            pltpu.VMEM((2, LANES, 2 * RET_DV), F32),
        ],
        compiler_params=_params("parallel", "parallel", "arbitrary"),
        name="retention",
    )(lg_tab, rq, rk, rv)


def _sb_kernel(q_ref, k_ref, v_ref, o_ref, acc_ref, carry_ref):
    qb = pl.program_id(2)
    q = q_ref[0]
    lane = lax.broadcasted_iota(jnp.int32, (BLK, LANES), 1)
    zero = jnp.zeros_like(q)
    q2 = jnp.concatenate([jnp.where(lane < SB_DH, q, zero),
                          jnp.where(lane >= SB_DH, q, zero)], axis=0)
    r = lax.broadcasted_iota(jnp.int32, (BLK, BLK), 0)
    s = lax.broadcasted_iota(jnp.int32, (BLK, BLK), 1)
    suffix = jnp.where(r >= s, 1.0, 0.0).astype(BF16)
    key_col = lax.broadcasted_iota(jnp.int32, (1, BLK), 1)

    def valid_bias(kb):
        return jnp.where(kb * BLK + key_col < PAD, NEG_BIG, 0.0).astype(F32)

    def step(kb, bias, carry):
        start = pl.multiple_of(kb * BLK, BLK)
        kblk = k_ref[0, pl.ds(start, BLK), :]
        vblk = v_ref[0, pl.ds(start, BLK), :]
        z = _dot_nt(q2, kblk)
        if bias is not None:
            z = z + bias
        log_1mb = -(jnp.maximum(z, 0.0) + jnp.log(1.0 + jnp.exp(-jnp.abs(z))))
        hi = log_1mb.astype(BF16)
        lo = (log_1mb - hi.astype(F32)).astype(BF16)
        incl = _dot(hi, suffix) + _dot(lo, suffix)
        if carry is not None:
            incl = incl + carry
        a = jnp.exp(z + incl).astype(BF16)
        return _dot(a, vblk), incl[:, 0:1]

    causal = jnp.where(s < r, 0.0, NEG_BIG).astype(F32)
    diag_bias = causal + valid_bias(qb)
    pv, carry = step(qb, jnp.concatenate([diag_bias, diag_bias], axis=0), None)
    acc_ref[...] = pv
    carry_ref[...] = carry

    def body(i, c):
        pv, carry = step(qb - 1 - i, None, carry_ref[...])
        acc_ref[...] += pv
        carry_ref[...] = carry
        return c

    lax.fori_loop(0, jnp.maximum(qb - 1, 0), body, 0)

    @pl.when(qb > 0)
    def _first_block():
        pv, _ = step(0, valid_bias(0), carry_ref[...])
        acc_ref[...] += pv

    acc = acc_ref[...]
    o_ref[0] = jnp.where(lane < SB_DH, acc[:BLK], acc[BLK:]).astype(BF16)


def _stick_breaking(sq, sk, sv):
    batch, seq_len, width = sq.shape
    pairs = SB_HEADS // 2
    assert PAD <= BLK
    return pl.pallas_call(
        _sb_kernel,
        grid=(batch, pairs, seq_len // BLK),
        in_specs=[
            pl.BlockSpec((1, BLK, LANES), lambda b, p, i: (b, i, p)),
            pl.BlockSpec((1, seq_len, LANES), lambda b, p, i: (b, 0, p)),
            pl.BlockSpec((1, seq_len, LANES), lambda b, p, i: (b, 0, p)),
        ],
        out_specs=pl.BlockSpec((1, BLK, LANES), lambda b, p, i: (b, i, p)),
        out_shape=jax.ShapeDtypeStruct((batch, seq_len, width), BF16),
        scratch_shapes=[pltpu.VMEM((2 * BLK, LANES), F32), pltpu.VMEM((2 * BLK, 1), F32)],
        compiler_params=_params("parallel", "parallel", "parallel"),
        name="stick_breaking",
    )(sq, sk, sv)


def _merge_kernel(x_ref, g_ref, wg_ref, gn_ref, or_ref, os_ref, wr_ref, ws_ref, wo_ref, o_ref):
    x = x_ref[...]
    u = _rmsnorm(x, g_ref[...]).astype(BF16)
    rg = _dot(u, wg_ref[:, 0:1024])
    gated = (rg * _sigmoid(rg)) * (or_ref[...].astype(F32) * gn_ref[...])
    y_ret = _dot(gated.astype(BF16), wr_ref[...])
    y_sb = _dot(os_ref[...], ws_ref[...])
    ga = _dot(u, wg_ref[:, 1024:2048])
    gb = _dot(u, wg_ref[:, 2048:3072])
    y = _sigmoid(ga) * y_ret + _sigmoid(gb) * y_sb
    o_ref[...] = x + _dot(y.astype(BF16), wo_ref[...])


def _merge(h, g, w_gate, gn, o_r, o_s, w_ret, w_sb, w_out, *, tile, in_map, out_rows):
    def in_spec(width):
        return pl.BlockSpec((tile, width), lambda i: (in_map(i), 0))

    return pl.pallas_call(
        _merge_kernel,
        grid=(out_rows // tile,),
        in_specs=[
            in_spec(D_MODEL), _const_spec((1, D_MODEL)), _const_spec((D_MODEL, 3072)),
            _const_spec((1, RET_HEADS * RET_DV)), in_spec(RET_HEADS * RET_DV),
            in_spec(SB_HEADS * SB_DH), _const_spec((RET_HEADS * RET_DV, D_MODEL)),
            _const_spec((SB_HEADS * SB_DH, D_MODEL)), _const_spec((D_MODEL, D_MODEL)),
        ],
        out_specs=pl.BlockSpec((tile, D_MODEL), lambda i: (i, 0)),
        out_shape=jax.ShapeDtypeStruct((out_rows, D_MODEL), F32),
        compiler_params=_params("parallel"),
        name="mixer_merge",
    )(h, g, w_gate, gn, o_r, o_s, w_ret, w_sb, w_out)


def _split_ffn_weights(w_gu, w_down):
    nj = D_FF // FF_CHUNK
    wa = w_gu[:, :D_FF].astype(BF16).reshape(D_MODEL, nj, FF_CHUNK).transpose(1, 0, 2)
    wb = w_gu[:, D_FF:].astype(BF16).reshape(D_MODEL, nj, FF_CHUNK).transpose(1, 0, 2)
    wd = w_down.astype(BF16).reshape(nj, FF_CHUNK, D_MODEL)
    return wa, wb, wd


def kernel(x, meta, ffn1_norm, ffn1_w_gu, ffn1_w_down, mix_norm, w_in, ret_gn, w_ret_proj,
           w_sb_proj, w_out, ffn2_norm, ffn2_w_gu, ffn2_w_down, final_norm):
    batch, seq, d = x.shape
    seq_len = seq + BLK
    rows = batch * seq_len
    assert d == D_MODEL and seq % ROW_TILE == 0 and rows % ROW_TILE == 0

    h = jnp.concatenate([jnp.zeros((batch, PAD, d), x.dtype),
                         jnp.broadcast_to(meta.astype(x.dtype)[None], (batch, N_META, d)), x], axis=1)
    h = h.reshape(rows, d)

    pos = (jnp.arange(seq_len) - PAD).astype(F32)
    freqs = ROPE_BASE ** (-jnp.arange(0, RET_DK, 2, dtype=F32) / RET_DK)
    ang = pos[:, None] * freqs[None, :]
    cos_h, sin_h = jnp.cos(ang), jnp.sin(ang)
    cos_t = jnp.tile(jnp.concatenate([cos_h, cos_h] * 2, axis=1), (batch, 1))
    sin_t = jnp.tile(jnp.concatenate([-sin_h, sin_h] * 2, axis=1), (batch, 1))
    valid_t = jnp.tile(jnp.broadcast_to((pos >= 0).astype(F32)[:, None], (seq_len, LANES)), (batch, 1))
    log_gamma = jnp.log(1.0 - 2.0 ** (-5.0 - jnp.arange(RET_HEADS, dtype=F32)))
    lg_tab = jnp.broadcast_to(log_gamma.reshape(RET_HEADS // 2, 2, 1), (RET_HEADS // 2, 2, BLK))

    row2 = lambda t: t.reshape(1, -1)
    blocks_per_seq = seq_len // BLK
    for l in range(DEPTH):
        last = l == DEPTH - 1
        wa, wb, wd = _split_ffn_weights(ffn1_w_gu[l], ffn1_w_down[l])
        h = _ffn(h, row2(ffn1_norm[l]), wa, wb, wd, row2(final_norm), final_norm=False)

        w_l = w_in[l]
        w_qkv = jnp.concatenate([w_l[:, C_RQ:C_RG], w_l[:, C_SQ:C_GA]], axis=1).astype(BF16)
        w_gate = jnp.concatenate([w_l[:, C_RG:C_SQ], w_l[:, C_GA:C_END]], axis=1).astype(BF16)
        rq, rk, rv, sq, sk, sv = _proj(h, row2(mix_norm[l]), w_qkv, cos_t, sin_t, valid_t)
        shape3 = lambda t: t.reshape(batch, seq_len, t.shape[-1])
        o_r = _retention(lg_tab, shape3(rq), shape3(rk), shape3(rv)).reshape(rows, -1)
        o_s = _stick_breaking(shape3(sq), shape3(sk), shape3(sv)).reshape(rows, -1)
        merge = functools.partial(
            _merge, h, row2(mix_norm[l]), w_gate, row2(ret_gn[l]), o_r, o_s,
            w_ret_proj[l].astype(BF16), w_sb_proj[l].astype(BF16), w_out[l].astype(BF16))
        if last:
            real_blocks = blocks_per_seq - 1
            h = merge(tile=BLK, out_rows=batch * seq,
                      in_map=lambda i: i + i // real_blocks + 1)
        else:
            h = merge(tile=ROW_TILE, out_rows=rows, in_map=lambda i: i)

        wa, wb, wd = _split_ffn_weights(ffn2_w_gu[l], ffn2_w_down[l])
        h = _ffn(h, row2(ffn2_norm[l]), wa, wb, wd, row2(final_norm), final_norm=last)
    return h.reshape(batch, seq, d)
```

```python
import functools

import jax
import jax.numpy as jnp
import numpy as np
from jax import lax
from jax.experimental import pallas as pl
from jax.experimental.pallas import tpu as pltpu

D_MODEL = 1024
DEPTH = 2
N_META = 16
RET_HEADS = 8
RET_DK = 64
RET_DV = 128
SB_HEADS = 8
SB_DH = 64
D_FF = 2816
ROPE_BASE = 10000.0
EPS = 1e-6

LANES = 128
BLK = 256
PAD = BLK - N_META
FF_CHUNK = 256
ROW_TILE = 512
HALF = LANES
MASK_BIG = 1e30
LOG2E = 1.4426950408889634
INV_LN2 = LOG2E
UNDERFLOW_LOG2 = -160.0
VMEM_LIMIT = 56 * 1024 * 1024

_COLS = np.cumsum([0, 512, 512, 1024, 1024, 512, 512, 512, 1024, 1024])
C_RQ, C_RK, C_RV, C_RG, C_SQ, C_SK, C_SV, C_GA, C_GB, C_END = (int(c) for c in _COLS)

F32 = jnp.float32
BF16 = jnp.bfloat16


def _dot(a, b):
    return jnp.dot(a, b, preferred_element_type=F32)


def _dot_nt(a, b):
    return lax.dot_general(a, b, (((1,), (1,)), ((), ())), preferred_element_type=F32)


def _dot_tn(a, b):
    return lax.dot_general(a, b, (((0,), (0,)), ((), ())), preferred_element_type=F32)


def _rmsnorm(x, g):
    r = lax.rsqrt(jnp.mean(x * x, axis=-1, keepdims=True) + EPS)
    return (x * r) * g


def _sigmoid(x):
    return 1.0 / (1.0 + jnp.exp(-x))


def _params(*sem):
    return pltpu.CompilerParams(dimension_semantics=sem, vmem_limit_bytes=VMEM_LIMIT)


def _const_spec(shape):
    zeros = (0,) * len(shape)
    return pl.BlockSpec(shape, lambda *_: zeros)


def _ffn_kernel(x_ref, g_ref, wa_ref, wb_ref, wd_ref, gf_ref, o_ref, acc_ref, *, final_norm):
    x = x_ref[...]
    u = _rmsnorm(x, g_ref[...]).astype(BF16)
    acc_ref[...] = x

    def body(j, carry):
        a = _dot(u, wa_ref[j])
        b = _dot(u, wb_ref[j])
        gate = (0.5 * a * _sigmoid(a) * b).astype(BF16)
        acc_ref[...] += _dot(gate, wd_ref[j])
        return carry

    lax.fori_loop(0, D_FF // FF_CHUNK, body, 0)
    y = acc_ref[...]
    if final_norm:
        y = _rmsnorm(y, gf_ref[...])
    o_ref[...] = y


def _ffn(h, g, wa, wb, wd, gf, *, final_norm):
    rows = h.shape[0]
    nj = D_FF // FF_CHUNK
    return pl.pallas_call(
        functools.partial(_ffn_kernel, final_norm=final_norm),
        grid=(rows // ROW_TILE,),
        in_specs=[
            pl.BlockSpec((ROW_TILE, D_MODEL), lambda i: (i, 0)),
            _const_spec((1, D_MODEL)),
            _const_spec((nj, D_MODEL, FF_CHUNK)),
            _const_spec((nj, D_MODEL, FF_CHUNK)),
            _const_spec((nj, FF_CHUNK, D_MODEL)),
            _const_spec((1, D_MODEL)),
        ],
        out_specs=pl.BlockSpec((ROW_TILE, D_MODEL), lambda i: (i, 0)),
        out_shape=jax.ShapeDtypeStruct((rows, D_MODEL), F32),
        scratch_shapes=[pltpu.VMEM((ROW_TILE, D_MODEL), F32)],
        compiler_params=_params("parallel"),
        name="ffn_final" if final_norm else "ffn",
    )(h, g, wa, wb, wd, gf)


def _proj_kernel(x_ref, g_ref, w_ref, cos_ref, sin_ref, valid_ref,
                 rq_ref, rk_ref, rv_ref, sq_ref, sk_ref, sv_ref):
    u = _rmsnorm(x_ref[...], g_ref[...]).astype(BF16)
    cos = jnp.concatenate([cos_ref[...]] * (512 // LANES), axis=1)
    sin = jnp.concatenate([sin_ref[...]] * (512 // LANES), axis=1)
    valid = jnp.concatenate([valid_ref[...]] * (512 // LANES), axis=1)
    lane = lax.broadcasted_iota(jnp.int32, (ROW_TILE, 512), 1)
    first_half = (lane % RET_DK) < (RET_DK // 2)

    def rotary(t):
        partner = jnp.where(first_half, pltpu.roll(t, 512 - RET_DK // 2, 1),
                            pltpu.roll(t, RET_DK // 2, 1))
        return t * cos + partner * sin

    rq = _dot(u, w_ref[:, 0:512])
    rq_ref[...] = rotary(rq).astype(BF16)
    rk = _dot(u, w_ref[:, 512:1024])
    rk_ref[...] = (rotary(rk) * (valid * RET_DK ** -0.5)).astype(BF16)
    rv = _dot(u, w_ref[:, 1024:2048])
    rv_ref[...] = (rv * jnp.concatenate([valid, valid], axis=1)).astype(BF16)
    sq = _dot(u, w_ref[:, 2048:2560])
    sq_ref[...] = (sq * (-LOG2E * SB_DH ** -0.5)).astype(BF16)
    sk_ref[...] = _dot(u, w_ref[:, 2560:3072]).astype(BF16)
    sv_ref[...] = _dot(u, w_ref[:, 3072:3584]).astype(BF16)


def _proj(h, g, w_qkv, cos_t, sin_t, valid_t):
    rows = h.shape[0]

    def row_spec(width):
        return pl.BlockSpec((ROW_TILE, width), lambda i: (i, 0))

    tab_spec = row_spec(LANES)
    widths = (512, 512, 1024, 512, 512, 512)
    return pl.pallas_call(
        _proj_kernel,
        grid=(rows // ROW_TILE,),
        in_specs=[row_spec(D_MODEL), _const_spec((1, D_MODEL)),
                  _const_spec((D_MODEL, 3584)), tab_spec, tab_spec, tab_spec],
        out_specs=[row_spec(w) for w in widths],
        out_shape=[jax.ShapeDtypeStruct((rows, w), BF16) for w in widths],
        compiler_params=_params("parallel"),
        name="mixer_proj",
    )(h, g, w_qkv, cos_t, sin_t, valid_t)


def _ret_kernel(lg_ref, q_ref, k_ref, v_ref, o_ref,
                state_ref, dec_ref, kdec_ref, qdec_ref, sdec_ref):
    c = pl.program_id(2)

    @pl.when(c == 0)
    def _init():
        lg = lg_ref[0]
        t = lax.broadcasted_iota(jnp.int32, (BLK, BLK), 0)
        s = lax.broadcasted_iota(jnp.int32, (BLK, BLK), 1)
        diff = (t - s).astype(F32)
        for i in range(2):
            dec_ref[i] = jnp.where(diff >= 0, jnp.exp(lg[i:i + 1, :] * jnp.maximum(diff, 0.0)), 0.0)
        lane = lax.broadcasted_iota(jnp.int32, (BLK, LANES), 1)
        pos = lax.broadcasted_iota(jnp.int32, (BLK, LANES), 0).astype(F32)
        lg_k = jnp.where(lane < RET_DK, lg[0:1, :LANES], lg[1:2, :LANES])
        kdec_ref[...] = jnp.exp(lg_k * (BLK - 1.0 - pos))
        col = lax.broadcasted_iota(jnp.int32, (BLK, 2 * RET_DV), 1)
        posv = lax.broadcasted_iota(jnp.int32, (BLK, 2 * RET_DV), 0).astype(F32)
        lg_v = jnp.where(col < RET_DV, lg[0:1, :], lg[1:2, :])
        qdec_ref[...] = jnp.exp(lg_v * (posv + 1.0))
        srow = lax.broadcasted_iota(jnp.int32, (LANES, 2 * RET_DV), 0)
        scol = lax.broadcasted_iota(jnp.int32, (LANES, 2 * RET_DV), 1)
        own = (srow // RET_DK) == (scol // RET_DV)
        lg_s = jnp.where(scol < RET_DV, lg[0:1, :], lg[1:2, :])
        sdec_ref[0] = jnp.where(own, jnp.exp(lg_s * float(BLK)), 0.0)
        sdec_ref[1] = jnp.where(own, 1.0, 0.0)
        state_ref[...] = jnp.zeros_like(state_ref)

    q = q_ref[0]
    k = k_ref[0]
    v = v_ref[0]
    state = state_ref[...]
    cross = _dot(q, state.astype(BF16)) * qdec_ref[...]
    lane = lax.broadcasted_iota(jnp.int32, (BLK, LANES), 1)
    for i in range(2):
        in_head = (lane < RET_DK) if i == 0 else (lane >= RET_DK)
        qi = jnp.where(in_head, q, jnp.zeros_like(q))
        scores = _dot_nt(qi, k) * dec_ref[i]
        o = _dot(scores.astype(BF16), v[:, i * RET_DV:(i + 1) * RET_DV])
        o = o + cross[:, i * RET_DV:(i + 1) * RET_DV]
        mu = jnp.mean(o, axis=-1, keepdims=True)
        d = o - mu
        var = jnp.mean(d * d, axis=-1, keepdims=True)
        o_ref[0, :, i * RET_DV:(i + 1) * RET_DV] = (d * lax.rsqrt(var + EPS)).astype(BF16)
    kd_t = (k.astype(F32) * kdec_ref[...]).T.astype(BF16)
    state_ref[...] = state * sdec_ref[0] + _dot(kd_t, v) * sdec_ref[1]


def _retention(lg_tab, rq, rk, rv):
    batch, seq_len, _ = rq.shape
    pairs = RET_HEADS // 2
    return pl.pallas_call(
        _ret_kernel,
        grid=(batch, pairs, seq_len // BLK),
        in_specs=[
            pl.BlockSpec((1, 2, BLK), lambda b, p, c: (p, 0, 0)),
            pl.BlockSpec((1, BLK, LANES), lambda b, p, c: (b, c, p)),
            pl.BlockSpec((1, BLK, LANES), lambda b, p, c: (b, c, p)),
            pl.BlockSpec((1, BLK, 2 * RET_DV), lambda b, p, c: (b, c, p)),
        ],
        out_specs=pl.BlockSpec((1, BLK, 2 * RET_DV), lambda b, p, c: (b, c, p)),
        out_shape=jax.ShapeDtypeStruct((batch, seq_len, RET_HEADS * RET_DV), BF16),
        scratch_shapes=[
            pltpu.VMEM((LANES, 2 * RET_DV), F32),
            pltpu.VMEM((2, BLK, BLK), F32),
            pltpu.VMEM((BLK, LANES), F32),
            pltpu.VMEM((BLK, 2 * RET_DV), F32),
            pltpu.VMEM((2, LANES, 2 * RET_DV), F32),
        ],
        compiler_params=_params("parallel", "parallel", "arbitrary"),
        name="retention",
    )(lg_tab, rq, rk, rv)


def _sb_kernel(q_ref, k_ref, v_ref, o_ref, acc_ref, carry_ref):
    qb = pl.program_id(2)
    q = q_ref[0]
    lane = lax.broadcasted_iota(jnp.int32, (BLK, LANES), 1)
    zero = jnp.zeros_like(q)
    q_heads = (jnp.where(lane < SB_DH, q, zero), jnp.where(lane >= SB_DH, q, zero))
    j = lax.broadcasted_iota(jnp.int32, (2 * HALF, 2 * HALF), 0) % HALF
    c = lax.broadcasted_iota(jnp.int32, (2 * HALF, 2 * HALF), 1)
    suffix = jnp.where((j >= c) | (c >= HALF), 1.0, 0.0).astype(BF16)
    key_col = lax.broadcasted_iota(jnp.int32, (1, BLK), 1)

    def half_sums(x):
        hi = x.astype(BF16)
        lo = (x - hi.astype(F32)).astype(BF16)
        return _dot(jnp.concatenate([hi, lo], axis=1), suffix)

    def step(kb, bias, first):
        start = pl.multiple_of(kb * BLK, BLK)
        kblk = k_ref[0, pl.ds(start, BLK), :]
        vblk = v_ref[0, pl.ds(start, BLK), :]
        for h in range(2):
            y = _dot_nt(q_heads[h], kblk) + bias
            log_1mb = jnp.minimum(y, 0.0) - jnp.log(1.0 + jnp.exp2(-jnp.abs(y))) * INV_LN2
            late = half_sums(log_1mb[:, HALF:])
            early = half_sums(log_1mb[:, :HALF])
            if first:
                incl_late, after_late = late[:, :HALF], late[:, HALF:]
            else:
                carry = carry_ref[h]
                incl_late, after_late = late[:, :HALF] + carry, late[:, HALF:] + carry
            incl = jnp.concatenate([early[:, :HALF] + after_late, incl_late], axis=1)
            a = jnp.exp2(incl - y).astype(BF16)
            pv = _dot(a, vblk)
            carry_ref[h] = early[:, HALF:] + after_late
            if first:
                acc_ref[h] = pv
            else:
                acc_ref[h] += pv

    def valid_bias(kb):
        return jnp.where(kb * BLK + key_col < PAD, MASK_BIG, 0.0).astype(F32)

    r = lax.broadcasted_iota(jnp.int32, (BLK, BLK), 0)
    s = lax.broadcasted_iota(jnp.int32, (BLK, BLK), 1)
    step(qb, jnp.where(s < r, 0.0, MASK_BIG).astype(F32) + valid_bias(qb), True)

    def live(state):
        kb, decayed = state
        return jnp.logical_and(kb >= 0, jnp.logical_not(decayed))

    def body(state):
        kb, _ = state
        step(kb, valid_bias(kb), False)
        top = jnp.max(jnp.maximum(carry_ref[0], carry_ref[1]))
        return kb - 1, top < UNDERFLOW_LOG2

    lax.while_loop(live, body, (qb - 1, False))
    o_ref[0] = jnp.where(lane < SB_DH, acc_ref[0], acc_ref[1]).astype(BF16)


def _stick_breaking(sq, sk, sv):
    batch, seq_len, width = sq.shape
    pairs = SB_HEADS // 2
    assert PAD <= BLK
    return pl.pallas_call(
        _sb_kernel,
        grid=(batch, pairs, seq_len // BLK),
        in_specs=[
            pl.BlockSpec((1, BLK, LANES), lambda b, p, i: (b, i, p)),
            pl.BlockSpec((1, seq_len, LANES), lambda b, p, i: (b, 0, p)),
            pl.BlockSpec((1, seq_len, LANES), lambda b, p, i: (b, 0, p)),
        ],
        out_specs=pl.BlockSpec((1, BLK, LANES), lambda b, p, i: (b, i, p)),
        out_shape=jax.ShapeDtypeStruct((batch, seq_len, width), BF16),
        scratch_shapes=[pltpu.VMEM((2, BLK, LANES), F32), pltpu.VMEM((2, BLK, LANES), F32)],
        compiler_params=_params("parallel", "parallel", "parallel"),
        name="stick_breaking",
    )(sq, sk, sv)


def _merge_kernel(x_ref, g_ref, wg_ref, gn_ref, or_ref, os_ref, wr_ref, ws_ref, wo_ref, o_ref):
    x = x_ref[...]
    u = _rmsnorm(x, g_ref[...]).astype(BF16)
    rg = _dot(u, wg_ref[:, 0:1024])
    gated = (rg * _sigmoid(rg)) * (or_ref[...].astype(F32) * gn_ref[...])
    y_ret = _dot(gated.astype(BF16), wr_ref[...])
    y_sb = _dot(os_ref[...], ws_ref[...])
    ga = _dot(u, wg_ref[:, 1024:2048])
    gb = _dot(u, wg_ref[:, 2048:3072])
    y = _sigmoid(ga) * y_ret + _sigmoid(gb) * y_sb
    o_ref[...] = x + _dot(y.astype(BF16), wo_ref[...])


def _merge(h, g, w_gate, gn, o_r, o_s, w_ret, w_sb, w_out, *, tile, in_map, out_rows):
    def in_spec(width):
        return pl.BlockSpec((tile, width), lambda i: (in_map(i), 0))

    return pl.pallas_call(
        _merge_kernel,
        grid=(out_rows // tile,),
        in_specs=[
            in_spec(D_MODEL), _const_spec((1, D_MODEL)), _const_spec((D_MODEL, 3072)),
            _const_spec((1, RET_HEADS * RET_DV)), in_spec(RET_HEADS * RET_DV),
            in_spec(SB_HEADS * SB_DH), _const_spec((RET_HEADS * RET_DV, D_MODEL)),
            _const_spec((SB_HEADS * SB_DH, D_MODEL)), _const_spec((D_MODEL, D_MODEL)),
        ],
        out_specs=pl.BlockSpec((tile, D_MODEL), lambda i: (i, 0)),
        out_shape=jax.ShapeDtypeStruct((out_rows, D_MODEL), F32),
        compiler_params=_params("parallel"),
        name="mixer_merge",
    )(h, g, w_gate, gn, o_r, o_s, w_ret, w_sb, w_out)


def _split_ffn_weights(w_gu, w_down):
    nj = D_FF // FF_CHUNK
    wa = w_gu[:, :D_FF].astype(BF16).reshape(D_MODEL, nj, FF_CHUNK).transpose(1, 0, 2)
    wb = w_gu[:, D_FF:].astype(BF16).reshape(D_MODEL, nj, FF_CHUNK).transpose(1, 0, 2)
    wd = w_down.astype(BF16).reshape(nj, FF_CHUNK, D_MODEL)
    return wa, wb, wd


def kernel(x, meta, ffn1_norm, ffn1_w_gu, ffn1_w_down, mix_norm, w_in, ret_gn, w_ret_proj,
           w_sb_proj, w_out, ffn2_norm, ffn2_w_gu, ffn2_w_down, final_norm):
    batch, seq, d = x.shape
    seq_len = seq + BLK
    rows = batch * seq_len
    assert d == D_MODEL and seq % ROW_TILE == 0 and rows % ROW_TILE == 0

    h = jnp.concatenate([jnp.zeros((batch, PAD, d), x.dtype),
                         jnp.broadcast_to(meta.astype(x.dtype)[None], (batch, N_META, d)), x], axis=1)
    h = h.reshape(rows, d)

    pos = (jnp.arange(seq_len) - PAD).astype(F32)
    freqs = ROPE_BASE ** (-jnp.arange(0, RET_DK, 2, dtype=F32) / RET_DK)
    ang = pos[:, None] * freqs[None, :]
    cos_h, sin_h = jnp.cos(ang), jnp.sin(ang)
    cos_t = jnp.tile(jnp.concatenate([cos_h, cos_h] * 2, axis=1), (batch, 1))
    sin_t = jnp.tile(jnp.concatenate([-sin_h, sin_h] * 2, axis=1), (batch, 1))
    valid_t = jnp.tile(jnp.broadcast_to((pos >= 0).astype(F32)[:, None], (seq_len, LANES)), (batch, 1))
    log_gamma = jnp.log(1.0 - 2.0 ** (-5.0 - jnp.arange(RET_HEADS, dtype=F32)))
    lg_tab = jnp.broadcast_to(log_gamma.reshape(RET_HEADS // 2, 2, 1), (RET_HEADS // 2, 2, BLK))

    row2 = lambda t: t.reshape(1, -1)
    blocks_per_seq = seq_len // BLK
    for l in range(DEPTH):
        last = l == DEPTH - 1
        wa, wb, wd = _split_ffn_weights(ffn1_w_gu[l], ffn1_w_down[l])
        h = _ffn(h, row2(ffn1_norm[l]), wa, wb, wd, row2(final_norm), final_norm=False)

        w_l = w_in[l]
        w_qkv = jnp.concatenate([w_l[:, C_RQ:C_RG], w_l[:, C_SQ:C_GA]], axis=1).astype(BF16)
        w_gate = jnp.concatenate([w_l[:, C_RG:C_SQ], w_l[:, C_GA:C_END]], axis=1).astype(BF16)
        rq, rk, rv, sq, sk, sv = _proj(h, row2(mix_norm[l]), w_qkv, cos_t, sin_t, valid_t)
        shape3 = lambda t: t.reshape(batch, seq_len, t.shape[-1])
        o_r = _retention(lg_tab, shape3(rq), shape3(rk), shape3(rv)).reshape(rows, -1)
        o_s = _stick_breaking(shape3(sq), shape3(sk), shape3(sv)).reshape(rows, -1)
        merge = functools.partial(
            _merge, h, row2(mix_norm[l]), w_gate, row2(ret_gn[l]), o_r, o_s,
            w_ret_proj[l].astype(BF16), w_sb_proj[l].astype(BF16), w_out[l].astype(BF16))
        if last:
            real_blocks = blocks_per_seq - 1
            h = merge(tile=BLK, out_rows=batch * seq,
                      in_map=lambda i: i + i // real_blocks + 1)
        else:
            h = merge(tile=ROW_TILE, out_rows=rows, in_map=lambda i: i)

        wa, wb, wd = _split_ffn_weights(ffn2_w_gu[l], ffn2_w_down[l])
        h = _ffn(h, row2(ffn2_norm[l]), wa, wb, wd, row2(final_norm), final_norm=last)
    return h.reshape(batch, seq, d)
```

```python
import functools

import jax
import jax.numpy as jnp
import numpy as np
from jax import lax
from jax.experimental import pallas as pl
from jax.experimental.pallas import tpu as pltpu

D_MODEL = 1024
DEPTH = 2
N_META = 16
RET_HEADS = 8
RET_DK = 64
RET_DV = 128
SB_HEADS = 8
SB_DH = 64
D_FF = 2816
ROPE_BASE = 10000.0
EPS = 1e-6

LANES = 128
BLK = 256
PAD = BLK - N_META
FF_CHUNK = 256
ROW_TILE = 512
HALF = LANES
MASK_BIG = 1e30
LOG2E = 1.4426950408889634
INV_LN2 = LOG2E
UNDERFLOW_LOG2 = -160.0
VMEM_LIMIT = 56 * 1024 * 1024

_COLS = np.cumsum([0, 512, 512, 1024, 1024, 512, 512, 512, 1024, 1024])
C_RQ, C_RK, C_RV, C_RG, C_SQ, C_SK, C_SV, C_GA, C_GB, C_END = (int(c) for c in _COLS)

F32 = jnp.float32
BF16 = jnp.bfloat16


def _dot(a, b):
    return jnp.dot(a, b, preferred_element_type=F32)


def _dot_nt(a, b):
    return lax.dot_general(a, b, (((1,), (1,)), ((), ())), preferred_element_type=F32)


def _rmsnorm(x, g):
    r = lax.rsqrt(jnp.mean(x * x, axis=-1, keepdims=True) + EPS)
    return (x * r) * g


def _sigmoid(x):
    return 1.0 / (1.0 + jnp.exp(-x))


def _params(*sem):
    return pltpu.CompilerParams(dimension_semantics=sem, vmem_limit_bytes=VMEM_LIMIT)


def _resident_spec(block, index_map):
    return pl.BlockSpec(block, index_map, pipeline_mode=pl.Buffered(1))


def _layer_spec(layer, shape, col_block=0):
    return _resident_spec((None,) + shape, lambda *_: (layer, 0, col_block))


def _ffn_kernel(x_ref, g_ref, wgu_ref, wd_ref, gf_ref, o_ref, gate_ref, *, final_norm):
    x = x_ref[...]
    u = _rmsnorm(x, g_ref[...]).astype(BF16)
    for c in range(0, D_FF, FF_CHUNK):
        a = _dot(u, wgu_ref[:, c:c + FF_CHUNK].astype(BF16))
        b = _dot(u, wgu_ref[:, D_FF + c:D_FF + c + FF_CHUNK].astype(BF16))
        gate_ref[:, c:c + FF_CHUNK] = (0.5 * a * _sigmoid(a) * b).astype(BF16)
    y = x + _dot(gate_ref[...], wd_ref[...].astype(BF16))
    if final_norm:
        y = _rmsnorm(y, gf_ref[...])
    o_ref[...] = y


def _ffn(h, layer, g, w_gu, w_down, gf, *, final_norm):
    rows = h.shape[0]
    return pl.pallas_call(
        functools.partial(_ffn_kernel, final_norm=final_norm),
        grid=(rows // ROW_TILE,),
        in_specs=[
            pl.BlockSpec((ROW_TILE, D_MODEL), lambda i: (i, 0)),
            _layer_spec(layer, (1, D_MODEL)),
            _layer_spec(layer, (D_MODEL, 2 * D_FF)),
            _layer_spec(layer, (D_FF, D_MODEL)),
            _resident_spec((1, D_MODEL), lambda i: (0, 0)),
        ],
        out_specs=pl.BlockSpec((ROW_TILE, D_MODEL), lambda i: (i, 0)),
        out_shape=jax.ShapeDtypeStruct((rows, D_MODEL), F32),
        scratch_shapes=[pltpu.VMEM((ROW_TILE, D_FF), BF16)],
        compiler_params=_params("parallel"),
        name="ffn_final" if final_norm else "ffn",
    )(h, g, w_gu, w_down, gf)


def _proj_kernel(x_ref, g_ref, wr_ref, ws_ref, cos_ref, sin_ref, valid_ref,
                 rq_ref, rk_ref, rv_ref, sq_ref, sk_ref, sv_ref):
    u = _rmsnorm(x_ref[...], g_ref[...]).astype(BF16)
    cos = jnp.concatenate([cos_ref[...]] * (512 // LANES), axis=1)
    sin = jnp.concatenate([sin_ref[...]] * (512 // LANES), axis=1)
    valid = jnp.concatenate([valid_ref[...]] * (512 // LANES), axis=1)
    lane = lax.broadcasted_iota(jnp.int32, (ROW_TILE, 512), 1)
    first_half = (lane % RET_DK) < (RET_DK // 2)

    def rotary(t):
        partner = jnp.where(first_half, pltpu.roll(t, 512 - RET_DK // 2, 1),
                            pltpu.roll(t, RET_DK // 2, 1))
        return t * cos + partner * sin

    def proj(w_ref, lo, hi):
        return _dot(u, w_ref[:, lo:hi].astype(BF16))

    rq_ref[...] = rotary(proj(wr_ref, C_RQ, C_RK)).astype(BF16)
    rk_ref[...] = (rotary(proj(wr_ref, C_RK, C_RV)) * (valid * RET_DK ** -0.5)).astype(BF16)
    rv_ref[...] = (proj(wr_ref, C_RV, C_RG) * jnp.concatenate([valid, valid], axis=1)).astype(BF16)
    sq_ref[...] = (proj(ws_ref, 0, 512) * (-LOG2E * SB_DH ** -0.5)).astype(BF16)
    sk_ref[...] = proj(ws_ref, 512, 1024).astype(BF16)
    sv_ref[...] = proj(ws_ref, 1024, 1536).astype(BF16)


def _proj(h, layer, g, w_in, cos_t, sin_t, valid_t):
    rows = h.shape[0]

    def row_spec(width):
        return pl.BlockSpec((ROW_TILE, width), lambda i: (i, 0))

    tab_spec = row_spec(LANES)
    widths = (512, 512, 1024, 512, 512, 512)
    sb_cols = C_GA - C_SQ
    assert C_SQ % sb_cols == 0
    return pl.pallas_call(
        _proj_kernel,
        grid=(rows // ROW_TILE,),
        in_specs=[row_spec(D_MODEL), _layer_spec(layer, (1, D_MODEL)),
                  _layer_spec(layer, (D_MODEL, C_RG)),
                  _layer_spec(layer, (D_MODEL, sb_cols), col_block=C_SQ // sb_cols),
                  tab_spec, tab_spec, tab_spec],
        out_specs=[row_spec(w) for w in widths],
        out_shape=[jax.ShapeDtypeStruct((rows, w), BF16) for w in widths],
        compiler_params=_params("parallel"),
        name="mixer_proj",
    )(h, g, w_in, w_in, cos_t, sin_t, valid_t)


def _ret_kernel(lg_ref, q_ref, k_ref, v_ref, o_ref,
                state_ref, dec_ref, kdec_ref, qdec_ref, sdec_ref):
    c = pl.program_id(2)

    @pl.when(c == 0)
    def _init():
        lg = lg_ref[0]
        t = lax.broadcasted_iota(jnp.int32, (BLK, BLK), 0)
        s = lax.broadcasted_iota(jnp.int32, (BLK, BLK), 1)
        diff = (t - s).astype(F32)
        for i in range(2):
            dec_ref[i] = jnp.where(diff >= 0, jnp.exp(lg[i:i + 1, :] * jnp.maximum(diff, 0.0)), 0.0)
        lane = lax.broadcasted_iota(jnp.int32, (BLK, LANES), 1)
        pos = lax.broadcasted_iota(jnp.int32, (BLK, LANES), 0).astype(F32)
        lg_k = jnp.where(lane < RET_DK, lg[0:1, :LANES], lg[1:2, :LANES])
        kdec_ref[...] = jnp.exp(lg_k * (BLK - 1.0 - pos))
        col = lax.broadcasted_iota(jnp.int32, (BLK, 2 * RET_DV), 1)
        posv = lax.broadcasted_iota(jnp.int32, (BLK, 2 * RET_DV), 0).astype(F32)
        lg_v = jnp.where(col < RET_DV, lg[0:1, :], lg[1:2, :])
        qdec_ref[...] = jnp.exp(lg_v * (posv + 1.0))
        srow = lax.broadcasted_iota(jnp.int32, (LANES, 2 * RET_DV), 0)
        scol = lax.broadcasted_iota(jnp.int32, (LANES, 2 * RET_DV), 1)
        own = (srow // RET_DK) == (scol // RET_DV)
        lg_s = jnp.where(scol < RET_DV, lg[0:1, :], lg[1:2, :])
        sdec_ref[0] = jnp.where(own, jnp.exp(lg_s * float(BLK)), 0.0)
        sdec_ref[1] = jnp.where(own, 1.0, 0.0)
        state_ref[...] = jnp.zeros_like(state_ref)

    q = q_ref[0]
    k = k_ref[0]
    v = v_ref[0]
    state = state_ref[...]
    cross = _dot(q, state.astype(BF16)) * qdec_ref[...]
    lane = lax.broadcasted_iota(jnp.int32, (BLK, LANES), 1)
    for i in range(2):
        in_head = (lane < RET_DK) if i == 0 else (lane >= RET_DK)
        qi = jnp.where(in_head, q, jnp.zeros_like(q))
        scores = _dot_nt(qi, k) * dec_ref[i]
        o = _dot(scores.astype(BF16), v[:, i * RET_DV:(i + 1) * RET_DV])
        o = o + cross[:, i * RET_DV:(i + 1) * RET_DV]
        mu = jnp.mean(o, axis=-1, keepdims=True)
        d = o - mu
        var = jnp.mean(d * d, axis=-1, keepdims=True)
        o_ref[0, :, i * RET_DV:(i + 1) * RET_DV] = (d * lax.rsqrt(var + EPS)).astype(BF16)
    kd_t = (k.astype(F32) * kdec_ref[...]).T.astype(BF16)
    state_ref[...] = state * sdec_ref[0] + _dot(kd_t, v) * sdec_ref[1]


def _retention(lg_tab, rq, rk, rv):
    batch, seq_len, _ = rq.shape
    pairs = RET_HEADS // 2
    return pl.pallas_call(
        _ret_kernel,
        grid=(batch, pairs, seq_len // BLK),
        in_specs=[
            pl.BlockSpec((1, 2, BLK), lambda b, p, c: (p, 0, 0)),
            pl.BlockSpec((1, BLK, LANES), lambda b, p, c: (b, c, p)),
            pl.BlockSpec((1, BLK, LANES), lambda b, p, c: (b, c, p)),
            pl.BlockSpec((1, BLK, 2 * RET_DV), lambda b, p, c: (b, c, p)),
        ],
        out_specs=pl.BlockSpec((1, BLK, 2 * RET_DV), lambda b, p, c: (b, c, p)),
        out_shape=jax.ShapeDtypeStruct((batch, seq_len, RET_HEADS * RET_DV), BF16),
        scratch_shapes=[
            pltpu.VMEM((LANES, 2 * RET_DV), F32),
            pltpu.VMEM((2, BLK, BLK), F32),
            pltpu.VMEM((BLK, LANES), F32),
            pltpu.VMEM((BLK, 2 * RET_DV), F32),
            pltpu.VMEM((2, LANES, 2 * RET_DV), F32),
        ],
        compiler_params=_params("parallel", "parallel", "arbitrary"),
        name="retention",
    )(lg_tab, rq, rk, rv)


def _sb_kernel(q_ref, k_ref, v_ref, o_ref, acc_ref, carry_ref):
    qb = pl.program_id(2)
    q = q_ref[0]
    lane = lax.broadcasted_iota(jnp.int32, (BLK, LANES), 1)
    zero = jnp.zeros_like(q)
    q_heads = (jnp.where(lane < SB_DH, q, zero), jnp.where(lane >= SB_DH, q, zero))
    j = lax.broadcasted_iota(jnp.int32, (2 * HALF, 2 * HALF), 0) % HALF
    c = lax.broadcasted_iota(jnp.int32, (2 * HALF, 2 * HALF), 1)
    suffix = jnp.where((j >= c) | (c >= HALF), 1.0, 0.0).astype(BF16)
    key_col = lax.broadcasted_iota(jnp.int32, (1, BLK), 1)

    def half_sums(x):
        hi = x.astype(BF16)
        lo = (x - hi.astype(F32)).astype(BF16)
        return _dot(jnp.concatenate([hi, lo], axis=1), suffix)

    def step(kb, bias, first):
        start = pl.multiple_of(kb * BLK, BLK)
        kblk = k_ref[0, pl.ds(start, BLK), :]
        vblk = v_ref[0, pl.ds(start, BLK), :]
        for h in range(2):
            y = _dot_nt(q_heads[h], kblk) + bias
            log_1mb = jnp.minimum(y, 0.0) - jnp.log(1.0 + jnp.exp2(-jnp.abs(y))) * INV_LN2
            late = half_sums(log_1mb[:, HALF:])
            early = half_sums(log_1mb[:, :HALF])
            if first:
                incl_late, after_late = late[:, :HALF], late[:, HALF:]
            else:
                carry = carry_ref[h]
                incl_late, after_late = late[:, :HALF] + carry, late[:, HALF:] + carry
            incl = jnp.concatenate([early[:, :HALF] + after_late, incl_late], axis=1)
            a = jnp.exp2(incl - y).astype(BF16)
            pv = _dot(a, vblk)
            carry_ref[h] = early[:, HALF:] + after_late
            if first:
                acc_ref[h] = pv
            else:
                acc_ref[h] += pv

    def valid_bias(kb):
        return jnp.where(kb * BLK + key_col < PAD, MASK_BIG, 0.0).astype(F32)

    r = lax.broadcasted_iota(jnp.int32, (BLK, BLK), 0)
    s = lax.broadcasted_iota(jnp.int32, (BLK, BLK), 1)
    step(qb, jnp.where(s < r, 0.0, MASK_BIG).astype(F32) + valid_bias(qb), True)

    def live(state):
        kb, decayed = state
        return jnp.logical_and(kb >= 0, jnp.logical_not(decayed))

    def body(state):
        kb, _ = state
        step(kb, valid_bias(kb), False)
        top = jnp.max(jnp.maximum(carry_ref[0], carry_ref[1]))
        return kb - 1, top < UNDERFLOW_LOG2

    lax.while_loop(live, body, (qb - 1, False))
    o_ref[0] = jnp.where(lane < SB_DH, acc_ref[0], acc_ref[1]).astype(BF16)


def _stick_breaking(sq, sk, sv):
    batch, seq_len, width = sq.shape
    pairs = SB_HEADS // 2
    assert PAD <= BLK
    return pl.pallas_call(
        _sb_kernel,
        grid=(batch, pairs, seq_len // BLK),
        in_specs=[
            pl.BlockSpec((1, BLK, LANES), lambda b, p, i: (b, i, p)),
            pl.BlockSpec((1, seq_len, LANES), lambda b, p, i: (b, 0, p)),
            pl.BlockSpec((1, seq_len, LANES), lambda b, p, i: (b, 0, p)),
        ],
        out_specs=pl.BlockSpec((1, BLK, LANES), lambda b, p, i: (b, i, p)),
        out_shape=jax.ShapeDtypeStruct((batch, seq_len, width), BF16),
        scratch_shapes=[pltpu.VMEM((2, BLK, LANES), F32), pltpu.VMEM((2, BLK, LANES), F32)],
        compiler_params=_params("parallel", "parallel", "parallel"),
        name="stick_breaking",
    )(sq, sk, sv)


def _merge_kernel(x_ref, g_ref, wrg_ref, wga0_ref, wga1_ref, wgb0_ref, wgb1_ref, gn_ref,
                  or_ref, os_ref, wr_ref, ws_ref, wo_ref, o_ref):
    x = x_ref[...]
    u = _rmsnorm(x, g_ref[...]).astype(BF16)

    def gate(*w_refs):
        return jnp.concatenate([_dot(u, w[...].astype(BF16)) for w in w_refs], axis=1)

    rg = gate(wrg_ref)
    gated = (rg * _sigmoid(rg)) * (or_ref[...].astype(F32) * gn_ref[...])
    y_ret = _dot(gated.astype(BF16), wr_ref[...].astype(BF16))
    y_sb = _dot(os_ref[...], ws_ref[...].astype(BF16))
    y = _sigmoid(gate(wga0_ref, wga1_ref)) * y_ret + _sigmoid(gate(wgb0_ref, wgb1_ref)) * y_sb
    o_ref[...] = x + _dot(y.astype(BF16), wo_ref[...].astype(BF16))


def _merge(h, layer, g, w_in, gn, o_r, o_s, w_ret, w_sb, w_out, *, tile, in_map, out_rows):
    def in_spec(width):
        return pl.BlockSpec((tile, width), lambda i: (in_map(i), 0))

    half = D_MODEL // 2
    assert C_RG % D_MODEL == 0 and C_GA % half == 0
    ret_w, sb_w = RET_HEADS * RET_DV, SB_HEADS * SB_DH
    return pl.pallas_call(
        _merge_kernel,
        grid=(out_rows // tile,),
        in_specs=[
            in_spec(D_MODEL), _layer_spec(layer, (1, D_MODEL)),
            _layer_spec(layer, (D_MODEL, D_MODEL), col_block=C_RG // D_MODEL),
            _layer_spec(layer, (D_MODEL, half), col_block=C_GA // half),
            _layer_spec(layer, (D_MODEL, half), col_block=C_GA // half + 1),
            _layer_spec(layer, (D_MODEL, half), col_block=C_GB // half),
            _layer_spec(layer, (D_MODEL, half), col_block=C_GB // half + 1),
            _layer_spec(layer, (1, ret_w)), in_spec(ret_w), in_spec(sb_w),
            _layer_spec(layer, (ret_w, D_MODEL)), _layer_spec(layer, (sb_w, D_MODEL)),
            _layer_spec(layer, (D_MODEL, D_MODEL)),
        ],
        out_specs=pl.BlockSpec((tile, D_MODEL), lambda i: (i, 0)),
        out_shape=jax.ShapeDtypeStruct((out_rows, D_MODEL), F32),
        compiler_params=_params("parallel"),
        name="mixer_merge",
    )(h, g, w_in, w_in, w_in, w_in, w_in, gn, o_r, o_s, w_ret, w_sb, w_out)


def kernel(x, meta, ffn1_norm, ffn1_w_gu, ffn1_w_down, mix_norm, w_in, ret_gn, w_ret_proj,
           w_sb_proj, w_out, ffn2_norm, ffn2_w_gu, ffn2_w_down, final_norm):
    batch, seq, d = x.shape
    seq_len = seq + BLK
    rows = batch * seq_len
    assert d == D_MODEL and seq % ROW_TILE == 0 and rows % ROW_TILE == 0

    h = jnp.concatenate([jnp.zeros((batch, PAD, d), x.dtype),
                         jnp.broadcast_to(meta.astype(x.dtype)[None], (batch, N_META, d)), x], axis=1)
    h = h.reshape(rows, d)

    pos = (jnp.arange(seq_len) - PAD).astype(F32)
    freqs = ROPE_BASE ** (-jnp.arange(0, RET_DK, 2, dtype=F32) / RET_DK)
    ang = pos[:, None] * freqs[None, :]
    cos_h, sin_h = jnp.cos(ang), jnp.sin(ang)
    cos_t = jnp.tile(jnp.concatenate([cos_h, cos_h] * 2, axis=1), (batch, 1))
    sin_t = jnp.tile(jnp.concatenate([-sin_h, sin_h] * 2, axis=1), (batch, 1))
    valid_t = jnp.tile(jnp.broadcast_to((pos >= 0).astype(F32)[:, None], (seq_len, LANES)), (batch, 1))
    log_gamma = jnp.log(1.0 - 2.0 ** (-5.0 - jnp.arange(RET_HEADS, dtype=F32)))
    lg_tab = jnp.broadcast_to(log_gamma.reshape(RET_HEADS // 2, 2, 1), (RET_HEADS // 2, 2, BLK))

    gains = lambda t: t.reshape(t.shape[0], 1, t.shape[1])
    final_gain = final_norm.reshape(1, d)
    shape3 = lambda t: t.reshape(batch, seq_len, t.shape[-1])
    real_blocks = seq // BLK
    for l in range(DEPTH):
        last = l == DEPTH - 1
        h = _ffn(h, l, gains(ffn1_norm), ffn1_w_gu, ffn1_w_down, final_gain, final_norm=False)
        rq, rk, rv, sq, sk, sv = _proj(h, l, gains(mix_norm), w_in, cos_t, sin_t, valid_t)
        o_r = _retention(lg_tab, shape3(rq), shape3(rk), shape3(rv)).reshape(rows, -1)
        o_s = _stick_breaking(shape3(sq), shape3(sk), shape3(sv)).reshape(rows, -1)
        merge = functools.partial(_merge, h, l, gains(mix_norm), w_in, gains(ret_gn), o_r, o_s,
                                  w_ret_proj, w_sb_proj, w_out)
        if last:
            h = merge(tile=BLK, out_rows=batch * seq, in_map=lambda i: i + i // real_blocks + 1)
        else:
            h = merge(tile=ROW_TILE, out_rows=rows, in_map=lambda i: i)
        h = _ffn(h, l, gains(ffn2_norm), ffn2_w_gu, ffn2_w_down, final_gain, final_norm=last)
    return h.reshape(batch, seq, d)
```

```python
import functools

import jax
import jax.numpy as jnp
import numpy as np
from jax import lax
from jax.experimental import pallas as pl
from jax.experimental.pallas import tpu as pltpu

D_MODEL = 1024
DEPTH = 2
N_META = 16
RET_HEADS = 8
RET_DK = 64
RET_DV = 128
SB_HEADS = 8
SB_DH = 64
D_FF = 2816
ROPE_BASE = 10000.0
EPS = 1e-6

LANES = 128
BLK = 256
PAD = BLK - N_META
FF_CHUNK = 256
ROW_TILE = 512
HALF = LANES
MASK_BIG = 1e30
LOG2E = 1.4426950408889634
INV_LN2 = LOG2E
UNDERFLOW_LOG2 = -160.0
VMEM_LIMIT = 56 * 1024 * 1024

_COLS = np.cumsum([0, 512, 512, 1024, 1024, 512, 512, 512, 1024, 1024])
C_RQ, C_RK, C_RV, C_RG, C_SQ, C_SK, C_SV, C_GA, C_GB, C_END = (int(c) for c in _COLS)

F32 = jnp.float32
BF16 = jnp.bfloat16


def _dot(a, b):
    return jnp.dot(a, b, preferred_element_type=F32)


def _dot_nt(a, b):
    return lax.dot_general(a, b, (((1,), (1,)), ((), ())), preferred_element_type=F32)


def _rmsnorm(x, g):
    r = lax.rsqrt(jnp.mean(x * x, axis=-1, keepdims=True) + EPS)
    return (x * r) * g


def _sigmoid(x):
    return 1.0 / (1.0 + jnp.exp(-x))


def _params(*sem):
    return pltpu.CompilerParams(dimension_semantics=sem, vmem_limit_bytes=VMEM_LIMIT)


def _resident_spec(block, index_map):
    return pl.BlockSpec(block, index_map, pipeline_mode=pl.Buffered(1))


def _layer_spec(layer, shape, col_block=0):
    return _resident_spec((None,) + shape, lambda *_: (layer, 0, col_block))


def _ffn_kernel(x_ref, g_ref, wgu_ref, wd_ref, gf_ref, o_ref, gate_ref, *, final_norm):
    x = x_ref[...]
    u = _rmsnorm(x, g_ref[...]).astype(BF16)
    for c in range(0, D_FF, FF_CHUNK):
        a = _dot(u, wgu_ref[:, c:c + FF_CHUNK].astype(BF16))
        b = _dot(u, wgu_ref[:, D_FF + c:D_FF + c + FF_CHUNK].astype(BF16))
        gate_ref[:, c:c + FF_CHUNK] = (0.5 * a * _sigmoid(a) * b).astype(BF16)
    y = x + _dot(gate_ref[...], wd_ref[...].astype(BF16))
    if final_norm:
        y = _rmsnorm(y, gf_ref[...])
    o_ref[...] = y


def _ffn(h, layer, g, w_gu, w_down, gf, *, final_norm):
    rows = h.shape[0]
    return pl.pallas_call(
        functools.partial(_ffn_kernel, final_norm=final_norm),
        grid=(rows // ROW_TILE,),
        in_specs=[
            pl.BlockSpec((ROW_TILE, D_MODEL), lambda i: (i, 0)),
            _layer_spec(layer, (1, D_MODEL)),
            _layer_spec(layer, (D_MODEL, 2 * D_FF)),
            _layer_spec(layer, (D_FF, D_MODEL)),
            _resident_spec((1, D_MODEL), lambda i: (0, 0)),
        ],
        out_specs=pl.BlockSpec((ROW_TILE, D_MODEL), lambda i: (i, 0)),
        out_shape=jax.ShapeDtypeStruct((rows, D_MODEL), F32),
        scratch_shapes=[pltpu.VMEM((ROW_TILE, D_FF), BF16)],
        compiler_params=_params("parallel"),
        name="ffn_final" if final_norm else "ffn",
    )(h, g, w_gu, w_down, gf)


def _proj_kernel(x_ref, g_ref, wr_ref, ws_ref, cos_ref, sin_ref, valid_ref,
                 rq_ref, rk_ref, rv_ref, sq_ref, sk_ref, sv_ref):
    u = _rmsnorm(x_ref[...], g_ref[...]).astype(BF16)
    cos = jnp.concatenate([cos_ref[...]] * (512 // LANES), axis=1)
    sin = jnp.concatenate([sin_ref[...]] * (512 // LANES), axis=1)
    valid = jnp.concatenate([valid_ref[...]] * (512 // LANES), axis=1)
    lane = lax.broadcasted_iota(jnp.int32, (ROW_TILE, 512), 1)
    first_half = (lane % RET_DK) < (RET_DK // 2)

    def rotary(t):
        partner = jnp.where(first_half, pltpu.roll(t, 512 - RET_DK // 2, 1),
                            pltpu.roll(t, RET_DK // 2, 1))
        return t * cos + partner * sin

    def proj(w_ref, lo, hi):
        return _dot(u, w_ref[:, lo:hi].astype(BF16))

    rq_ref[...] = rotary(proj(wr_ref, C_RQ, C_RK)).astype(BF16)
    rk_ref[...] = (rotary(proj(wr_ref, C_RK, C_RV)) * (valid * RET_DK ** -0.5)).astype(BF16)
    rv_ref[...] = (proj(wr_ref, C_RV, C_RG) * jnp.concatenate([valid, valid], axis=1)).astype(BF16)
    sq_ref[...] = (proj(ws_ref, 0, 512) * (-LOG2E * SB_DH ** -0.5)).astype(BF16)
    sk_ref[...] = proj(ws_ref, 512, 1024).astype(BF16)
    sv_ref[...] = proj(ws_ref, 1024, 1536).astype(BF16)


def _proj(h, layer, g, w_in, cos_t, sin_t, valid_t):
    rows = h.shape[0]

    def row_spec(width):
        return pl.BlockSpec((ROW_TILE, width), lambda i: (i, 0))

    tab_spec = row_spec(LANES)
    widths = (512, 512, 1024, 512, 512, 512)
    sb_cols = C_GA - C_SQ
    assert C_SQ % sb_cols == 0
    return pl.pallas_call(
        _proj_kernel,
        grid=(rows // ROW_TILE,),
        in_specs=[row_spec(D_MODEL), _layer_spec(layer, (1, D_MODEL)),
                  _layer_spec(layer, (D_MODEL, C_RG)),
                  _layer_spec(layer, (D_MODEL, sb_cols), col_block=C_SQ // sb_cols),
                  tab_spec, tab_spec, tab_spec],
        out_specs=[row_spec(w) for w in widths],
        out_shape=[jax.ShapeDtypeStruct((rows, w), BF16) for w in widths],
        compiler_params=_params("parallel"),
        name="mixer_proj",
    )(h, g, w_in, w_in, cos_t, sin_t, valid_t)


def _ret_kernel(lg_ref, q_ref, k_ref, v_ref, o_ref,
                state_ref, dec_ref, kdec_ref, qdec_ref, sdec_ref):
    c = pl.program_id(1)
    pairs = RET_HEADS // 2

    @pl.when(c == 0)
    def _init():
        t = lax.broadcasted_iota(jnp.int32, (BLK, BLK), 0)
        s = lax.broadcasted_iota(jnp.int32, (BLK, BLK), 1)
        diff = (t - s).astype(F32)
        lane = lax.broadcasted_iota(jnp.int32, (BLK, LANES), 1)
        pos = lax.broadcasted_iota(jnp.int32, (BLK, LANES), 0).astype(F32)
        col = lax.broadcasted_iota(jnp.int32, (BLK, 2 * RET_DV), 1)
        posv = lax.broadcasted_iota(jnp.int32, (BLK, 2 * RET_DV), 0).astype(F32)
        srow = lax.broadcasted_iota(jnp.int32, (LANES, 2 * RET_DV), 0)
        scol = lax.broadcasted_iota(jnp.int32, (LANES, 2 * RET_DV), 1)
        own = (srow // RET_DK) == (scol // RET_DV)
        for p in range(pairs):
            lg = lg_ref[p]
            for i in range(2):
                dec_ref[2 * p + i] = jnp.where(
                    diff >= 0, jnp.exp(lg[i:i + 1, :] * jnp.maximum(diff, 0.0)), 0.0)
            lg_k = jnp.where(lane < RET_DK, lg[0:1, :LANES], lg[1:2, :LANES])
            kdec_ref[p] = jnp.exp(lg_k * (BLK - 1.0 - pos))
            lg_v = jnp.where(col < RET_DV, lg[0:1, :], lg[1:2, :])
            qdec_ref[p] = jnp.exp(lg_v * (posv + 1.0))
            lg_s = jnp.where(scol < RET_DV, lg[0:1, :], lg[1:2, :])
            sdec_ref[2 * p] = jnp.where(own, jnp.exp(lg_s * float(BLK)), 0.0)
            sdec_ref[2 * p + 1] = jnp.where(own, 1.0, 0.0)
        state_ref[...] = jnp.zeros_like(state_ref)

    lane = lax.broadcasted_iota(jnp.int32, (BLK, LANES), 1)
    for p in range(pairs):
        q = q_ref[0, :, p * LANES:(p + 1) * LANES]
        k = k_ref[0, :, p * LANES:(p + 1) * LANES]
        v = v_ref[0, :, 2 * p * RET_DV:2 * (p + 1) * RET_DV]
        state = state_ref[p]
        cross = _dot(q, state.astype(BF16)) * qdec_ref[p]
        for i in range(2):
            in_head = (lane < RET_DK) if i == 0 else (lane >= RET_DK)
            qi = jnp.where(in_head, q, jnp.zeros_like(q))
            scores = _dot_nt(qi, k) * dec_ref[2 * p + i]
            o = _dot(scores.astype(BF16), v[:, i * RET_DV:(i + 1) * RET_DV])
            o = o + cross[:, i * RET_DV:(i + 1) * RET_DV]
            mu = jnp.mean(o, axis=-1, keepdims=True)
            d = o - mu
            var = jnp.mean(d * d, axis=-1, keepdims=True)
            head = 2 * p + i
            o_ref[0, :, head * RET_DV:(head + 1) * RET_DV] = (d * lax.rsqrt(var + EPS)).astype(BF16)
        kd_t = (k.astype(F32) * kdec_ref[p]).T.astype(BF16)
        state_ref[p] = state * sdec_ref[2 * p] + _dot(kd_t, v) * sdec_ref[2 * p + 1]


def _retention(lg_tab, rq, rk, rv):
    batch, seq_len, _ = rq.shape
    pairs = RET_HEADS // 2
    qk_w, v_w = RET_HEADS * RET_DK, RET_HEADS * RET_DV
    return pl.pallas_call(
        _ret_kernel,
        grid=(batch, seq_len // BLK),
        in_specs=[
            pl.BlockSpec((pairs, 2, BLK), lambda b, c: (0, 0, 0)),
            pl.BlockSpec((1, BLK, qk_w), lambda b, c: (b, c, 0)),
            pl.BlockSpec((1, BLK, qk_w), lambda b, c: (b, c, 0)),
            pl.BlockSpec((1, BLK, v_w), lambda b, c: (b, c, 0)),
        ],
        out_specs=pl.BlockSpec((1, BLK, v_w), lambda b, c: (b, c, 0)),
        out_shape=jax.ShapeDtypeStruct((batch, seq_len, v_w), BF16),
        scratch_shapes=[
            pltpu.VMEM((pairs, LANES, 2 * RET_DV), F32),
            pltpu.VMEM((RET_HEADS, BLK, BLK), F32),
            pltpu.VMEM((pairs, BLK, LANES), F32),
            pltpu.VMEM((pairs, BLK, 2 * RET_DV), F32),
            pltpu.VMEM((2 * pairs, LANES, 2 * RET_DV), F32),
        ],
        compiler_params=_params("parallel", "arbitrary"),
        name="retention",
    )(lg_tab, rq, rk, rv)


def _sb_kernel(q_ref, k_ref, v_ref, o_ref, acc_ref, carry_ref):
    qb = pl.program_id(1)
    pairs = SB_HEADS // 2
    lane = lax.broadcasted_iota(jnp.int32, (BLK, LANES), 1)
    q_heads = []
    for p in range(pairs):
        q = q_ref[0, :, p * LANES:(p + 1) * LANES]
        zero = jnp.zeros_like(q)
        q_heads += [jnp.where(lane < SB_DH, q, zero), jnp.where(lane >= SB_DH, q, zero)]
    j = lax.broadcasted_iota(jnp.int32, (2 * HALF, 2 * HALF), 0) % HALF
    c = lax.broadcasted_iota(jnp.int32, (2 * HALF, 2 * HALF), 1)
    suffix = jnp.where((j >= c) | (c >= HALF), 1.0, 0.0).astype(BF16)
    key_col = lax.broadcasted_iota(jnp.int32, (1, BLK), 1)

    def half_sums(x):
        hi = x.astype(BF16)
        lo = (x - hi.astype(F32)).astype(BF16)
        return _dot(jnp.concatenate([hi, lo], axis=1), suffix)

    def step(kb, bias, first):
        start = pl.multiple_of(kb * BLK, BLK)
        for h in range(SB_HEADS):
            cols = slice((h // 2) * LANES, (h // 2 + 1) * LANES)
            kblk = k_ref[0, pl.ds(start, BLK), cols]
            vblk = v_ref[0, pl.ds(start, BLK), cols]
            y = _dot_nt(q_heads[h], kblk) + bias
            log_1mb = jnp.minimum(y, 0.0) - jnp.log(1.0 + jnp.exp2(-jnp.abs(y))) * INV_LN2
            late = half_sums(log_1mb[:, HALF:])
            early = half_sums(log_1mb[:, :HALF])
            if first:
                incl_late, after_late = late[:, :HALF], late[:, HALF:]
            else:
                carry = carry_ref[h]
                incl_late, after_late = late[:, :HALF] + carry, late[:, HALF:] + carry
            incl = jnp.concatenate([early[:, :HALF] + after_late, incl_late], axis=1)
            a = jnp.exp2(incl - y).astype(BF16)
            pv = _dot(a, vblk)
            carry_ref[h] = early[:, HALF:] + after_late
            if first:
                acc_ref[h] = pv
            else:
                acc_ref[h] += pv

    def valid_bias(kb):
        return jnp.where(kb * BLK + key_col < PAD, MASK_BIG, 0.0).astype(F32)

    r = lax.broadcasted_iota(jnp.int32, (BLK, BLK), 0)
    s = lax.broadcasted_iota(jnp.int32, (BLK, BLK), 1)
    step(qb, jnp.where(s < r, 0.0, MASK_BIG).astype(F32) + valid_bias(qb), True)

    def live(state):
        kb, decayed = state
        return jnp.logical_and(kb >= 0, jnp.logical_not(decayed))

    def body(state):
        kb, _ = state
        step(kb, valid_bias(kb), False)
        top = carry_ref[0]
        for h in range(1, SB_HEADS):
            top = jnp.maximum(top, carry_ref[h])
        return kb - 1, jnp.max(top) < UNDERFLOW_LOG2

    lax.while_loop(live, body, (qb - 1, False))
    for p in range(pairs):
        o_ref[0, :, p * LANES:(p + 1) * LANES] = jnp.where(
            lane < SB_DH, acc_ref[2 * p], acc_ref[2 * p + 1]).astype(BF16)


def _stick_breaking(sq, sk, sv):
    batch, seq_len, width = sq.shape
    assert PAD <= BLK and width == SB_HEADS * SB_DH
    return pl.pallas_call(
        _sb_kernel,
        grid=(batch, seq_len // BLK),
        in_specs=[
            pl.BlockSpec((1, BLK, width), lambda b, i: (b, i, 0)),
            pl.BlockSpec((1, seq_len, width), lambda b, i: (b, 0, 0)),
            pl.BlockSpec((1, seq_len, width), lambda b, i: (b, 0, 0)),
        ],
        out_specs=pl.BlockSpec((1, BLK, width), lambda b, i: (b, i, 0)),
        out_shape=jax.ShapeDtypeStruct((batch, seq_len, width), BF16),
        scratch_shapes=[pltpu.VMEM((SB_HEADS, BLK, LANES), F32),
                        pltpu.VMEM((SB_HEADS, BLK, LANES), F32)],
        compiler_params=_params("parallel", "parallel"),
        name="stick_breaking",
    )(sq, sk, sv)


def _merge_kernel(x_ref, g_ref, wrg_ref, wga0_ref, wga1_ref, wgb0_ref, wgb1_ref, gn_ref,
                  or_ref, os_ref, wr_ref, ws_ref, wo_ref, o_ref):
    x = x_ref[...]
    u = _rmsnorm(x, g_ref[...]).astype(BF16)

    def gate(*w_refs):
        return jnp.concatenate([_dot(u, w[...].astype(BF16)) for w in w_refs], axis=1)

    rg = gate(wrg_ref)
    gated = (rg * _sigmoid(rg)) * (or_ref[...].astype(F32) * gn_ref[...])
    y_ret = _dot(gated.astype(BF16), wr_ref[...].astype(BF16))
    y_sb = _dot(os_ref[...], ws_ref[...].astype(BF16))
    y = _sigmoid(gate(wga0_ref, wga1_ref)) * y_ret + _sigmoid(gate(wgb0_ref, wgb1_ref)) * y_sb
    o_ref[...] = x + _dot(y.astype(BF16), wo_ref[...].astype(BF16))


def _merge(h, layer, g, w_in, gn, o_r, o_s, w_ret, w_sb, w_out, *, tile, in_map, out_rows):
    def in_spec(width):
        return pl.BlockSpec((tile, width), lambda i: (in_map(i), 0))

    half = D_MODEL // 2
    assert C_RG % D_MODEL == 0 and C_GA % half == 0
    ret_w, sb_w = RET_HEADS * RET_DV, SB_HEADS * SB_DH
    return pl.pallas_call(
        _merge_kernel,
        grid=(out_rows // tile,),
        in_specs=[
            in_spec(D_MODEL), _layer_spec(layer, (1, D_MODEL)),
            _layer_spec(layer, (D_MODEL, D_MODEL), col_block=C_RG // D_MODEL),
            _layer_spec(layer, (D_MODEL, half), col_block=C_GA // half),
            _layer_spec(layer, (D_MODEL, half), col_block=C_GA // half + 1),
            _layer_spec(layer, (D_MODEL, half), col_block=C_GB // half),
            _layer_spec(layer, (D_MODEL, half), col_block=C_GB // half + 1),
            _layer_spec(layer, (1, ret_w)), in_spec(ret_w), in_spec(sb_w),
            _layer_spec(layer, (ret_w, D_MODEL)), _layer_spec(layer, (sb_w, D_MODEL)),
            _layer_spec(layer, (D_MODEL, D_MODEL)),
        ],
        out_specs=pl.BlockSpec((tile, D_MODEL), lambda i: (i, 0)),
        out_shape=jax.ShapeDtypeStruct((out_rows, D_MODEL), F32),
        compiler_params=_params("parallel"),
        name="mixer_merge",
    )(h, g, w_in, w_in, w_in, w_in, w_in, gn, o_r, o_s, w_ret, w_sb, w_out)


def kernel(x, meta, ffn1_norm, ffn1_w_gu, ffn1_w_down, mix_norm, w_in, ret_gn, w_ret_proj,
           w_sb_proj, w_out, ffn2_norm, ffn2_w_gu, ffn2_w_down, final_norm):
    batch, seq, d = x.shape
    seq_len = seq + BLK
    rows = batch * seq_len
    assert d == D_MODEL and seq % ROW_TILE == 0 and rows % ROW_TILE == 0

    h = jnp.concatenate([jnp.zeros((batch, PAD, d), x.dtype),
                         jnp.broadcast_to(meta.astype(x.dtype)[None], (batch, N_META, d)), x], axis=1)
    h = h.reshape(rows, d)

    pos = (jnp.arange(seq_len) - PAD).astype(F32)
    freqs = ROPE_BASE ** (-jnp.arange(0, RET_DK, 2, dtype=F32) / RET_DK)
    ang = pos[:, None] * freqs[None, :]
    cos_h, sin_h = jnp.cos(ang), jnp.sin(ang)
    cos_t = jnp.tile(jnp.concatenate([cos_h, cos_h] * 2, axis=1), (batch, 1))
    sin_t = jnp.tile(jnp.concatenate([-sin_h, sin_h] * 2, axis=1), (batch, 1))
    valid_t = jnp.tile(jnp.broadcast_to((pos >= 0).astype(F32)[:, None], (seq_len, LANES)), (batch, 1))
    log_gamma = jnp.log(1.0 - 2.0 ** (-5.0 - jnp.arange(RET_HEADS, dtype=F32)))
    lg_tab = jnp.broadcast_to(log_gamma.reshape(RET_HEADS // 2, 2, 1), (RET_HEADS // 2, 2, BLK))

    gains = lambda t: t.reshape(t.shape[0], 1, t.shape[1])
    final_gain = final_norm.reshape(1, d)
    shape3 = lambda t: t.reshape(batch, seq_len, t.shape[-1])
    real_blocks = seq // BLK
    for l in range(DEPTH):
        last = l == DEPTH - 1
        h = _ffn(h, l, gains(ffn1_norm), ffn1_w_gu, ffn1_w_down, final_gain, final_norm=False)
        rq, rk, rv, sq, sk, sv = _proj(h, l, gains(mix_norm), w_in, cos_t, sin_t, valid_t)
        o_r = _retention(lg_tab, shape3(rq), shape3(rk), shape3(rv)).reshape(rows, -1)
        o_s = _stick_breaking(shape3(sq), shape3(sk), shape3(sv)).reshape(rows, -1)
        merge = functools.partial(_merge, h, l, gains(mix_norm), w_in, gains(ret_gn), o_r, o_s,
                                  w_ret_proj, w_sb_proj, w_out)
        if last:
            h = merge(tile=BLK, out_rows=batch * seq, in_map=lambda i: i + i // real_blocks + 1)
        else:
            h = merge(tile=ROW_TILE, out_rows=rows, in_map=lambda i: i)
        h = _ffn(h, l, gains(ffn2_norm), ffn2_w_gu, ffn2_w_down, final_gain, final_norm=last)
    return h.reshape(batch, seq, d)
```

```python
import functools

import jax
import jax.numpy as jnp
import numpy as np
from jax import lax
from jax.experimental import pallas as pl
from jax.experimental.pallas import tpu as pltpu

D_MODEL = 1024
DEPTH = 2
N_META = 16
RET_HEADS = 8
RET_DK = 64
RET_DV = 128
SB_HEADS = 8
SB_DH = 64
D_FF = 2816
ROPE_BASE = 10000.0
EPS = 1e-6

LANES = 128
BLK = 256
PAD = BLK - N_META
FF_CHUNK = 256
ROW_TILE = 512
HALF = LANES
MASK_BIG = 1e30
LOG2E = 1.4426950408889634
INV_LN2 = LOG2E
UNDERFLOW_LOG2 = -160.0
VMEM_LIMIT = 56 * 1024 * 1024

_COLS = np.cumsum([0, 512, 512, 1024, 1024, 512, 512, 512, 1024, 1024])
C_RQ, C_RK, C_RV, C_RG, C_SQ, C_SK, C_SV, C_GA, C_GB, C_END = (int(c) for c in _COLS)

F32 = jnp.float32
BF16 = jnp.bfloat16


def _dot(a, b):
    return jnp.dot(a, b, preferred_element_type=F32)


def _dot_nt(a, b):
    return lax.dot_general(a, b, (((1,), (1,)), ((), ())), preferred_element_type=F32)


def _rmsnorm(x, g):
    r = lax.rsqrt(jnp.mean(x * x, axis=-1, keepdims=True) + EPS)
    return (x * r) * g


def _sigmoid(x):
    return 1.0 / (1.0 + jnp.exp(-x))


def _params(*sem):
    return pltpu.CompilerParams(dimension_semantics=sem, vmem_limit_bytes=VMEM_LIMIT)


def _resident_spec(block, index_map):
    return pl.BlockSpec(block, index_map, pipeline_mode=pl.Buffered(1))


def _layer_spec(layer, shape, col_block=0):
    return _resident_spec((None,) + shape, lambda *_: (layer, 0, col_block))


def _ffn_kernel(x_ref, g_ref, wgu_ref, wd_ref, gf_ref, o_ref, gate_ref, *, final_norm):
    x = x_ref[...]
    u = _rmsnorm(x, g_ref[...]).astype(BF16)
    for c in range(0, D_FF, FF_CHUNK):
        a = _dot(u, wgu_ref[:, c:c + FF_CHUNK].astype(BF16))
        b = _dot(u, wgu_ref[:, D_FF + c:D_FF + c + FF_CHUNK].astype(BF16))
        gate_ref[:, c:c + FF_CHUNK] = (0.5 * a * _sigmoid(a) * b).astype(BF16)
    y = x + _dot(gate_ref[...], wd_ref[...].astype(BF16))
    if final_norm:
        y = _rmsnorm(y, gf_ref[...])
    o_ref[...] = y


def _ffn(h, layer, g, w_gu, w_down, gf, *, final_norm):
    rows = h.shape[0]
    return pl.pallas_call(
        functools.partial(_ffn_kernel, final_norm=final_norm),
        grid=(rows // ROW_TILE,),
        in_specs=[
            pl.BlockSpec((ROW_TILE, D_MODEL), lambda i: (i, 0)),
            _layer_spec(layer, (1, D_MODEL)),
            _layer_spec(layer, (D_MODEL, 2 * D_FF)),
            _layer_spec(layer, (D_FF, D_MODEL)),
            _resident_spec((1, D_MODEL), lambda i: (0, 0)),
        ],
        out_specs=pl.BlockSpec((ROW_TILE, D_MODEL), lambda i: (i, 0)),
        out_shape=jax.ShapeDtypeStruct((rows, D_MODEL), F32),
        scratch_shapes=[pltpu.VMEM((ROW_TILE, D_FF), BF16)],
        compiler_params=_params("parallel"),
        name="ffn_final" if final_norm else "ffn",
    )(h, g, w_gu, w_down, gf)


def _proj_kernel(x_ref, g_ref, wr_ref, ws_ref, cos_ref, sin_ref, valid_ref,
                 rq_ref, rk_ref, rv_ref, sq_ref, sk_ref, sv_ref):
    u = _rmsnorm(x_ref[...], g_ref[...]).astype(BF16)
    cos = jnp.concatenate([cos_ref[...]] * (512 // LANES), axis=1)
    sin = jnp.concatenate([sin_ref[...]] * (512 // LANES), axis=1)
    valid = jnp.concatenate([valid_ref[...]] * (512 // LANES), axis=1)
    lane = lax.broadcasted_iota(jnp.int32, (ROW_TILE, 512), 1)
    first_half = (lane % RET_DK) < (RET_DK // 2)

    def rotary(t):
        partner = jnp.where(first_half, pltpu.roll(t, 512 - RET_DK // 2, 1),
                            pltpu.roll(t, RET_DK // 2, 1))
        return t * cos + partner * sin

    def proj(w_ref, lo, hi):
        return _dot(u, w_ref[:, lo:hi].astype(BF16))

    rq_ref[...] = rotary(proj(wr_ref, C_RQ, C_RK)).astype(BF16)
    rk_ref[...] = (rotary(proj(wr_ref, C_RK, C_RV)) * (valid * RET_DK ** -0.5)).astype(BF16)
    rv_ref[...] = (proj(wr_ref, C_RV, C_RG) * jnp.concatenate([valid, valid], axis=1)).astype(BF16)
    sq_ref[...] = (proj(ws_ref, 0, 512) * (-LOG2E * SB_DH ** -0.5)).astype(BF16)
    sk_ref[...] = proj(ws_ref, 512, 1024).astype(BF16)
    sv_ref[...] = proj(ws_ref, 1024, 1536).astype(BF16)


def _proj(h, layer, g, w_in, cos_t, sin_t, valid_t):
    rows = h.shape[0]

    def row_spec(width):
        return pl.BlockSpec((ROW_TILE, width), lambda i: (i, 0))

    tab_spec = row_spec(LANES)
    widths = (512, 512, 1024, 512, 512, 512)
    sb_cols = C_GA - C_SQ
    assert C_SQ % sb_cols == 0
    return pl.pallas_call(
        _proj_kernel,
        grid=(rows // ROW_TILE,),
        in_specs=[row_spec(D_MODEL), _layer_spec(layer, (1, D_MODEL)),
                  _layer_spec(layer, (D_MODEL, C_RG)),
                  _layer_spec(layer, (D_MODEL, sb_cols), col_block=C_SQ // sb_cols),
                  tab_spec, tab_spec, tab_spec],
        out_specs=[row_spec(w) for w in widths],
        out_shape=[jax.ShapeDtypeStruct((rows, w), BF16) for w in widths],
        compiler_params=_params("parallel"),
        name="mixer_proj",
    )(h, g, w_in, w_in, cos_t, sin_t, valid_t)


def _ret_kernel(lg_ref, q_ref, k_ref, v_ref, o_ref,
                state_ref, dec_ref, kdec_ref, qdec_ref, sdec_ref):
    c = pl.program_id(1)
    pairs = RET_HEADS // 2

    @pl.when(c == 0)
    def _init():
        t = lax.broadcasted_iota(jnp.int32, (BLK, BLK), 0)
        s = lax.broadcasted_iota(jnp.int32, (BLK, BLK), 1)
        diff = (t - s).astype(F32)
        lane = lax.broadcasted_iota(jnp.int32, (BLK, LANES), 1)
        pos = lax.broadcasted_iota(jnp.int32, (BLK, LANES), 0).astype(F32)
        col = lax.broadcasted_iota(jnp.int32, (BLK, 2 * RET_DV), 1)
        posv = lax.broadcasted_iota(jnp.int32, (BLK, 2 * RET_DV), 0).astype(F32)
        srow = lax.broadcasted_iota(jnp.int32, (LANES, 2 * RET_DV), 0)
        scol = lax.broadcasted_iota(jnp.int32, (LANES, 2 * RET_DV), 1)
        own = (srow // RET_DK) == (scol // RET_DV)
        for p in range(pairs):
            lg = lg_ref[p]
            for i in range(2):
                dec_ref[2 * p + i] = jnp.where(
                    diff >= 0, jnp.exp(lg[i:i + 1, :] * jnp.maximum(diff, 0.0)), 0.0)
            lg_k = jnp.where(lane < RET_DK, lg[0:1, :LANES], lg[1:2, :LANES])
            kdec_ref[p] = jnp.exp(lg_k * (BLK - 1.0 - pos))
            lg_v = jnp.where(col < RET_DV, lg[0:1, :], lg[1:2, :])
            qdec_ref[p] = jnp.exp(lg_v * (posv + 1.0))
            lg_s = jnp.where(scol < RET_DV, lg[0:1, :], lg[1:2, :])
            sdec_ref[2 * p] = jnp.where(own, jnp.exp(lg_s * float(BLK)), 0.0)
            sdec_ref[2 * p + 1] = jnp.where(own, 1.0, 0.0)
        state_ref[...] = jnp.zeros_like(state_ref)

    lane = lax.broadcasted_iota(jnp.int32, (BLK, LANES), 1)
    for p in range(pairs):
        q = q_ref[0, :, p * LANES:(p + 1) * LANES]
        k = k_ref[0, :, p * LANES:(p + 1) * LANES]
        v = v_ref[0, :, 2 * p * RET_DV:2 * (p + 1) * RET_DV]
        state = state_ref[p]
        cross = _dot(q, state.astype(BF16)) * qdec_ref[p]
        for i in range(2):
            in_head = (lane < RET_DK) if i == 0 else (lane >= RET_DK)
            qi = jnp.where(in_head, q, jnp.zeros_like(q))
            scores = _dot_nt(qi, k) * dec_ref[2 * p + i]
            o = _dot(scores.astype(BF16), v[:, i * RET_DV:(i + 1) * RET_DV])
            o = o + cross[:, i * RET_DV:(i + 1) * RET_DV]
            mu = jnp.mean(o, axis=-1, keepdims=True)
            d = o - mu
            var = jnp.mean(d * d, axis=-1, keepdims=True)
            head = 2 * p + i
            o_ref[0, :, head * RET_DV:(head + 1) * RET_DV] = (d * lax.rsqrt(var + EPS)).astype(BF16)
        kd_t = (k.astype(F32) * kdec_ref[p]).T.astype(BF16)
        state_ref[p] = state * sdec_ref[2 * p] + _dot(kd_t, v) * sdec_ref[2 * p + 1]


def _retention(lg_tab, rq, rk, rv):
    batch, seq_len, _ = rq.shape
    pairs = RET_HEADS // 2
    qk_w, v_w = RET_HEADS * RET_DK, RET_HEADS * RET_DV
    return pl.pallas_call(
        _ret_kernel,
        grid=(batch, seq_len // BLK),
        in_specs=[
            pl.BlockSpec((pairs, 2, BLK), lambda b, c: (0, 0, 0)),
            pl.BlockSpec((1, BLK, qk_w), lambda b, c: (b, c, 0)),
            pl.BlockSpec((1, BLK, qk_w), lambda b, c: (b, c, 0)),
            pl.BlockSpec((1, BLK, v_w), lambda b, c: (b, c, 0)),
        ],
        out_specs=pl.BlockSpec((1, BLK, v_w), lambda b, c: (b, c, 0)),
        out_shape=jax.ShapeDtypeStruct((batch, seq_len, v_w), BF16),
        scratch_shapes=[
            pltpu.VMEM((pairs, LANES, 2 * RET_DV), F32),
            pltpu.VMEM((RET_HEADS, BLK, BLK), F32),
            pltpu.VMEM((pairs, BLK, LANES), F32),
            pltpu.VMEM((pairs, BLK, 2 * RET_DV), F32),
            pltpu.VMEM((2 * pairs, LANES, 2 * RET_DV), F32),
        ],
        compiler_params=_params("parallel", "arbitrary"),
        name="retention",
    )(lg_tab, rq, rk, rv)


def _sb_kernel(q_ref, k_ref, v_ref, o_ref, acc_ref, carry_ref):
    qb = pl.program_id(1)
    pairs = SB_HEADS // 2
    lane = lax.broadcasted_iota(jnp.int32, (BLK, LANES), 1)
    q_heads = []
    for p in range(pairs):
        q = q_ref[0, :, p * LANES:(p + 1) * LANES]
        zero = jnp.zeros_like(q)
        q_heads += [jnp.where(lane < SB_DH, q, zero), jnp.where(lane >= SB_DH, q, zero)]
    j = lax.broadcasted_iota(jnp.int32, (2 * HALF, 2 * HALF), 0) % HALF
    c = lax.broadcasted_iota(jnp.int32, (2 * HALF, 2 * HALF), 1)
    suffix = jnp.where((j >= c) | (c >= HALF), 1.0, 0.0).astype(BF16)
    key_col = lax.broadcasted_iota(jnp.int32, (1, BLK), 1)

    def hi_lo(x):
        hi = x.astype(BF16)
        lo = (x - hi.astype(F32)).astype(BF16)
        return jnp.concatenate([hi, lo], axis=1)

    def step(blocks, first):
        ys, splits = [], []
        for kb, bias in blocks:
            start = pl.multiple_of(kb * BLK, BLK)
            for h in range(SB_HEADS):
                cols = slice((h // 2) * LANES, (h // 2 + 1) * LANES)
                y = _dot_nt(q_heads[h], k_ref[0, pl.ds(start, BLK), cols]) + bias
                log_1mb = jnp.minimum(y, 0.0) - jnp.log2(1.0 + jnp.exp2(-jnp.abs(y)))
                ys.append(y)
                splits += [hi_lo(log_1mb[:, HALF:]), hi_lo(log_1mb[:, :HALF])]
        sums = _dot(jnp.concatenate(splits, axis=0), suffix)
        carries = [None if first else carry_ref[h] for h in range(SB_HEADS)]
        accs = [None] * SB_HEADS
        for i, (kb, _) in enumerate(blocks):
            start = pl.multiple_of(kb * BLK, BLK)
            for h in range(SB_HEADS):
                n = i * SB_HEADS + h
                late = sums[2 * n * BLK:(2 * n + 1) * BLK]
                early = sums[(2 * n + 1) * BLK:(2 * n + 2) * BLK]
                incl_late, after_late = late[:, :HALF], late[:, HALF:]
                if carries[h] is not None:
                    incl_late, after_late = incl_late + carries[h], after_late + carries[h]
                incl = jnp.concatenate([early[:, :HALF] + after_late, incl_late], axis=1)
                a = jnp.exp2(incl - ys[n]).astype(BF16)
                cols = slice((h // 2) * LANES, (h // 2 + 1) * LANES)
                pv = _dot(a, v_ref[0, pl.ds(start, BLK), cols])
                carries[h] = early[:, HALF:] + after_late
                accs[h] = pv if accs[h] is None else accs[h] + pv
        for h in range(SB_HEADS):
            carry_ref[h] = carries[h]
            if first:
                acc_ref[h] = accs[h]
            else:
                acc_ref[h] += accs[h]

    def valid_bias(kb):
        return jnp.where(kb * BLK + key_col < PAD, MASK_BIG, 0.0).astype(F32)

    r = lax.broadcasted_iota(jnp.int32, (BLK, BLK), 0)
    s = lax.broadcasted_iota(jnp.int32, (BLK, BLK), 1)
    causal = jnp.where(s < r, 0.0, MASK_BIG).astype(F32)

    @pl.when(qb == 0)
    def _first_block():
        step([(0, causal + valid_bias(0))], True)

    @pl.when(qb > 0)
    def _diagonal_and_previous():
        step([(qb, causal), (qb - 1, valid_bias(qb - 1))], True)

    def decayed():
        top = carry_ref[0]
        for h in range(1, SB_HEADS):
            top = jnp.maximum(top, carry_ref[h])
        return jnp.max(top) < UNDERFLOW_LOG2

    def live(state):
        kb, done = state
        return jnp.logical_and(kb >= 0, jnp.logical_not(done))

    def body(state):
        kb, _ = state
        step([(kb, valid_bias(kb))], False)
        return kb - 1, decayed()

    lax.while_loop(live, body, (qb - 2, decayed()))
    for p in range(pairs):
        o_ref[0, :, p * LANES:(p + 1) * LANES] = jnp.where(
            lane < SB_DH, acc_ref[2 * p], acc_ref[2 * p + 1]).astype(BF16)


def _stick_breaking(sq, sk, sv):
    batch, seq_len, width = sq.shape
    assert PAD <= BLK and width == SB_HEADS * SB_DH
    return pl.pallas_call(
        _sb_kernel,
        grid=(batch, seq_len // BLK),
        in_specs=[
            pl.BlockSpec((1, BLK, width), lambda b, i: (b, i, 0)),
            pl.BlockSpec((1, seq_len, width), lambda b, i: (b, 0, 0)),
            pl.BlockSpec((1, seq_len, width), lambda b, i: (b, 0, 0)),
        ],
        out_specs=pl.BlockSpec((1, BLK, width), lambda b, i: (b, i, 0)),
        out_shape=jax.ShapeDtypeStruct((batch, seq_len, width), BF16),
        scratch_shapes=[pltpu.VMEM((SB_HEADS, BLK, LANES), F32),
                        pltpu.VMEM((SB_HEADS, BLK, LANES), F32)],
        compiler_params=_params("parallel", "parallel"),
        name="stick_breaking",
    )(sq, sk, sv)


def _merge_kernel(x_ref, g_ref, wrg_ref, wga0_ref, wga1_ref, wgb0_ref, wgb1_ref, gn_ref,
                  or_ref, os_ref, wr_ref, ws_ref, wo_ref, o_ref):
    x = x_ref[...]
    u = _rmsnorm(x, g_ref[...]).astype(BF16)

    def gate(*w_refs):
        return jnp.concatenate([_dot(u, w[...].astype(BF16)) for w in w_refs], axis=1)

    rg = gate(wrg_ref)
    gated = (rg * _sigmoid(rg)) * (or_ref[...].astype(F32) * gn_ref[...])
    y_ret = _dot(gated.astype(BF16), wr_ref[...].astype(BF16))
    y_sb = _dot(os_ref[...], ws_ref[...].astype(BF16))
    y = _sigmoid(gate(wga0_ref, wga1_ref)) * y_ret + _sigmoid(gate(wgb0_ref, wgb1_ref)) * y_sb
    o_ref[...] = x + _dot(y.astype(BF16), wo_ref[...].astype(BF16))


def _merge(h, layer, g, w_in, gn, o_r, o_s, w_ret, w_sb, w_out, *, tile, in_map, out_rows):
    def in_spec(width):
        return pl.BlockSpec((tile, width), lambda i: (in_map(i), 0))

    half = D_MODEL // 2
    assert C_RG % D_MODEL == 0 and C_GA % half == 0
    ret_w, sb_w = RET_HEADS * RET_DV, SB_HEADS * SB_DH
    return pl.pallas_call(
        _merge_kernel,
        grid=(out_rows // tile,),
        in_specs=[
            in_spec(D_MODEL), _layer_spec(layer, (1, D_MODEL)),
            _layer_spec(layer, (D_MODEL, D_MODEL), col_block=C_RG // D_MODEL),
            _layer_spec(layer, (D_MODEL, half), col_block=C_GA // half),
            _layer_spec(layer, (D_MODEL, half), col_block=C_GA // half + 1),
            _layer_spec(layer, (D_MODEL, half), col_block=C_GB // half),
            _layer_spec(layer, (D_MODEL, half), col_block=C_GB // half + 1),
            _layer_spec(layer, (1, ret_w)), in_spec(ret_w), in_spec(sb_w),
            _layer_spec(layer, (ret_w, D_MODEL)), _layer_spec(layer, (sb_w, D_MODEL)),
            _layer_spec(layer, (D_MODEL, D_MODEL)),
        ],
        out_specs=pl.BlockSpec((tile, D_MODEL), lambda i: (i, 0)),
        out_shape=jax.ShapeDtypeStruct((out_rows, D_MODEL), F32),
        compiler_params=_params("parallel"),
        name="mixer_merge",
    )(h, g, w_in, w_in, w_in, w_in, w_in, gn, o_r, o_s, w_ret, w_sb, w_out)


def kernel(x, meta, ffn1_norm, ffn1_w_gu, ffn1_w_down, mix_norm, w_in, ret_gn, w_ret_proj,
           w_sb_proj, w_out, ffn2_norm, ffn2_w_gu, ffn2_w_down, final_norm):
    batch, seq, d = x.shape
    seq_len = seq + BLK
    rows = batch * seq_len
    assert d == D_MODEL and seq % ROW_TILE == 0 and rows % ROW_TILE == 0

    h = jnp.concatenate([jnp.zeros((batch, PAD, d), x.dtype),
                         jnp.broadcast_to(meta.astype(x.dtype)[None], (batch, N_META, d)), x], axis=1)
    h = h.reshape(rows, d)

    pos = (jnp.arange(seq_len) - PAD).astype(F32)
    freqs = ROPE_BASE ** (-jnp.arange(0, RET_DK, 2, dtype=F32) / RET_DK)
    ang = pos[:, None] * freqs[None, :]
    cos_h, sin_h = jnp.cos(ang), jnp.sin(ang)
    cos_t = jnp.tile(jnp.concatenate([cos_h, cos_h] * 2, axis=1), (batch, 1))
    sin_t = jnp.tile(jnp.concatenate([-sin_h, sin_h] * 2, axis=1), (batch, 1))
    valid_t = jnp.tile(jnp.broadcast_to((pos >= 0).astype(F32)[:, None], (seq_len, LANES)), (batch, 1))
    log_gamma = jnp.log(1.0 - 2.0 ** (-5.0 - jnp.arange(RET_HEADS, dtype=F32)))
    lg_tab = jnp.broadcast_to(log_gamma.reshape(RET_HEADS // 2, 2, 1), (RET_HEADS // 2, 2, BLK))

    gains = lambda t: t.reshape(t.shape[0], 1, t.shape[1])
    final_gain = final_norm.reshape(1, d)
    shape3 = lambda t: t.reshape(batch, seq_len, t.shape[-1])
    real_blocks = seq // BLK
    for l in range(DEPTH):
        last = l == DEPTH - 1
        h = _ffn(h, l, gains(ffn1_norm), ffn1_w_gu, ffn1_w_down, final_gain, final_norm=False)
        rq, rk, rv, sq, sk, sv = _proj(h, l, gains(mix_norm), w_in, cos_t, sin_t, valid_t)
        o_r = _retention(lg_tab, shape3(rq), shape3(rk), shape3(rv)).reshape(rows, -1)
        o_s = _stick_breaking(shape3(sq), shape3(sk), shape3(sv)).reshape(rows, -1)
        merge = functools.partial(_merge, h, l, gains(mix_norm), w_in, gains(ret_gn), o_r, o_s,
                                  w_ret_proj, w_sb_proj, w_out)
        if last:
            h = merge(tile=BLK, out_rows=batch * seq, in_map=lambda i: i + i // real_blocks + 1)
        else:
            h = merge(tile=ROW_TILE, out_rows=rows, in_map=lambda i: i)
        h = _ffn(h, l, gains(ffn2_norm), ffn2_w_gu, ffn2_w_down, final_gain, final_norm=last)
    return h.reshape(batch, seq, d)
```

```python
import functools

import jax
import jax.numpy as jnp
import numpy as np
from jax import lax
from jax.experimental import pallas as pl
from jax.experimental.pallas import tpu as pltpu

D_MODEL = 1024
DEPTH = 2
N_META = 16
RET_HEADS = 8
RET_DK = 64
RET_DV = 128
SB_HEADS = 8
SB_DH = 64
D_FF = 2816
ROPE_BASE = 10000.0
EPS = 1e-6

LANES = 128
BLK = 256
PAD = BLK - N_META
FF_CHUNK = 256
ROW_TILE = 512
MASK_BIG = 1e30
LOG2E = 1.4426950408889634
UNDERFLOW_LOG2 = -160.0
VMEM_LIMIT = 56 * 1024 * 1024

_COLS = np.cumsum([0, 512, 512, 1024, 1024, 512, 512, 512, 1024, 1024])
C_RQ, C_RK, C_RV, C_RG, C_SQ, C_SK, C_SV, C_GA, C_GB, C_END = (int(c) for c in _COLS)

F32 = jnp.float32
BF16 = jnp.bfloat16


def _dot(a, b):
    return jnp.dot(a, b, preferred_element_type=F32)


def _dot_nt(a, b):
    return lax.dot_general(a, b, (((1,), (1,)), ((), ())), preferred_element_type=F32)


def _rmsnorm(x, g):
    r = lax.rsqrt(jnp.mean(x * x, axis=-1, keepdims=True) + EPS)
    return (x * r) * g


def _sigmoid(x):
    return 1.0 / (1.0 + jnp.exp(-x))


def _params(*sem):
    return pltpu.CompilerParams(dimension_semantics=sem, vmem_limit_bytes=VMEM_LIMIT)


def _resident_spec(block, index_map):
    return pl.BlockSpec(block, index_map, pipeline_mode=pl.Buffered(1))


def _layer_spec(layer, shape, col_block=0):
    return _resident_spec((None,) + shape, lambda *_: (layer, 0, col_block))


def _ffn_kernel(x_ref, g_ref, wgu_ref, wd_ref, gf_ref, o_ref, gate_ref, *, final_norm):
    x = x_ref[...]
    u = _rmsnorm(x, g_ref[...]).astype(BF16)
    for c in range(0, D_FF, FF_CHUNK):
        a = _dot(u, wgu_ref[:, c:c + FF_CHUNK].astype(BF16))
        b = _dot(u, wgu_ref[:, D_FF + c:D_FF + c + FF_CHUNK].astype(BF16))
        gate_ref[:, c:c + FF_CHUNK] = (0.5 * a * _sigmoid(a) * b).astype(BF16)
    y = x + _dot(gate_ref[...], wd_ref[...].astype(BF16))
    if final_norm:
        y = _rmsnorm(y, gf_ref[...])
    o_ref[...] = y


def _ffn(h, layer, g, w_gu, w_down, gf, *, final_norm):
    rows = h.shape[0]
    return pl.pallas_call(
        functools.partial(_ffn_kernel, final_norm=final_norm),
        grid=(rows // ROW_TILE,),
        in_specs=[
            pl.BlockSpec((ROW_TILE, D_MODEL), lambda i: (i, 0)),
            _layer_spec(layer, (1, D_MODEL)),
            _layer_spec(layer, (D_MODEL, 2 * D_FF)),
            _layer_spec(layer, (D_FF, D_MODEL)),
            _resident_spec((1, D_MODEL), lambda i: (0, 0)),
        ],
        out_specs=pl.BlockSpec((ROW_TILE, D_MODEL), lambda i: (i, 0)),
        out_shape=jax.ShapeDtypeStruct((rows, D_MODEL), F32),
        scratch_shapes=[pltpu.VMEM((ROW_TILE, D_FF), BF16)],
        compiler_params=_params("parallel"),
        name="ffn_final" if final_norm else "ffn",
    )(h, g, w_gu, w_down, gf)


def _proj_kernel(x_ref, g_ref, wr_ref, ws_ref, cos_ref, sin_ref, valid_ref,
                 rq_ref, rk_ref, rv_ref, sq_ref, sk_ref, sv_ref):
    u = _rmsnorm(x_ref[...], g_ref[...]).astype(BF16)
    cos = jnp.concatenate([cos_ref[...]] * (512 // LANES), axis=1)
    sin = jnp.concatenate([sin_ref[...]] * (512 // LANES), axis=1)
    valid = jnp.concatenate([valid_ref[...]] * (512 // LANES), axis=1)
    lane = lax.broadcasted_iota(jnp.int32, (ROW_TILE, 512), 1)
    first_half = (lane % RET_DK) < (RET_DK // 2)

    def rotary(t):
        partner = jnp.where(first_half, pltpu.roll(t, 512 - RET_DK // 2, 1),
                            pltpu.roll(t, RET_DK // 2, 1))
        return t * cos + partner * sin

    def proj(w_ref, lo, hi):
        return _dot(u, w_ref[:, lo:hi].astype(BF16))

    rq_ref[...] = rotary(proj(wr_ref, C_RQ, C_RK)).astype(BF16)
    rk_ref[...] = (rotary(proj(wr_ref, C_RK, C_RV)) * (valid * RET_DK ** -0.5)).astype(BF16)
    rv_ref[...] = (proj(wr_ref, C_RV, C_RG) * jnp.concatenate([valid, valid], axis=1)).astype(BF16)
    sq_ref[...] = (proj(ws_ref, 0, 512) * (-LOG2E * SB_DH ** -0.5)).astype(BF16)
    sk_ref[...] = proj(ws_ref, 512, 1024).astype(BF16)
    sv_ref[...] = proj(ws_ref, 1024, 1536).astype(BF16)


def _proj(h, layer, g, w_in, cos_t, sin_t, valid_t):
    rows = h.shape[0]

    def row_spec(width):
        return pl.BlockSpec((ROW_TILE, width), lambda i: (i, 0))

    tab_spec = row_spec(LANES)
    widths = (512, 512, 1024, 512, 512, 512)
    sb_cols = C_GA - C_SQ
    assert C_SQ % sb_cols == 0
    return pl.pallas_call(
        _proj_kernel,
        grid=(rows // ROW_TILE,),
        in_specs=[row_spec(D_MODEL), _layer_spec(layer, (1, D_MODEL)),
                  _layer_spec(layer, (D_MODEL, C_RG)),
                  _layer_spec(layer, (D_MODEL, sb_cols), col_block=C_SQ // sb_cols),
                  tab_spec, tab_spec, tab_spec],
        out_specs=[row_spec(w) for w in widths],
        out_shape=[jax.ShapeDtypeStruct((rows, w), BF16) for w in widths],
        compiler_params=_params("parallel"),
        name="mixer_proj",
    )(h, g, w_in, w_in, cos_t, sin_t, valid_t)


def _ret_kernel(lg_ref, q_ref, k_ref, v_ref, o_ref,
                state_ref, dec_ref, kdec_ref, qdec_ref, sdec_ref):
    c = pl.program_id(1)
    pairs = RET_HEADS // 2

    @pl.when(c == 0)
    def _init():
        t = lax.broadcasted_iota(jnp.int32, (BLK, BLK), 0)
        s = lax.broadcasted_iota(jnp.int32, (BLK, BLK), 1)
        diff = (t - s).astype(F32)
        lane = lax.broadcasted_iota(jnp.int32, (BLK, LANES), 1)
        pos = lax.broadcasted_iota(jnp.int32, (BLK, LANES), 0).astype(F32)
        col = lax.broadcasted_iota(jnp.int32, (BLK, 2 * RET_DV), 1)
        posv = lax.broadcasted_iota(jnp.int32, (BLK, 2 * RET_DV), 0).astype(F32)
        srow = lax.broadcasted_iota(jnp.int32, (LANES, 2 * RET_DV), 0)
        scol = lax.broadcasted_iota(jnp.int32, (LANES, 2 * RET_DV), 1)
        own = (srow // RET_DK) == (scol // RET_DV)
        for p in range(pairs):
            lg = lg_ref[p]
            for i in range(2):
                dec_ref[2 * p + i] = jnp.where(
                    diff >= 0, jnp.exp(lg[i:i + 1, :] * jnp.maximum(diff, 0.0)), 0.0)
            lg_k = jnp.where(lane < RET_DK, lg[0:1, :LANES], lg[1:2, :LANES])
            kdec_ref[p] = jnp.exp(lg_k * (BLK - 1.0 - pos))
            lg_v = jnp.where(col < RET_DV, lg[0:1, :], lg[1:2, :])
            qdec_ref[p] = jnp.exp(lg_v * (posv + 1.0))
            lg_s = jnp.where(scol < RET_DV, lg[0:1, :], lg[1:2, :])
            sdec_ref[2 * p] = jnp.where(own, jnp.exp(lg_s * float(BLK)), 0.0)
            sdec_ref[2 * p + 1] = jnp.where(own, 1.0, 0.0)
        state_ref[...] = jnp.zeros_like(state_ref)

    lane = lax.broadcasted_iota(jnp.int32, (BLK, LANES), 1)
    for p in range(pairs):
        q = q_ref[0, :, p * LANES:(p + 1) * LANES]
        k = k_ref[0, :, p * LANES:(p + 1) * LANES]
        v = v_ref[0, :, 2 * p * RET_DV:2 * (p + 1) * RET_DV]
        state = state_ref[p]
        cross = _dot(q, state.astype(BF16)) * qdec_ref[p]
        for i in range(2):
            in_head = (lane < RET_DK) if i == 0 else (lane >= RET_DK)
            qi = jnp.where(in_head, q, jnp.zeros_like(q))
            scores = _dot_nt(qi, k) * dec_ref[2 * p + i]
            o = _dot(scores.astype(BF16), v[:, i * RET_DV:(i + 1) * RET_DV])
            o = o + cross[:, i * RET_DV:(i + 1) * RET_DV]
            mu = jnp.mean(o, axis=-1, keepdims=True)
            d = o - mu
            var = jnp.mean(d * d, axis=-1, keepdims=True)
            head = 2 * p + i
            o_ref[0, :, head * RET_DV:(head + 1) * RET_DV] = (d * lax.rsqrt(var + EPS)).astype(BF16)
        kd_t = (k.astype(F32) * kdec_ref[p]).T.astype(BF16)
        state_ref[p] = state * sdec_ref[2 * p] + _dot(kd_t, v) * sdec_ref[2 * p + 1]


def _retention(lg_tab, rq, rk, rv):
    batch, seq_len, _ = rq.shape
    pairs = RET_HEADS // 2
    qk_w, v_w = RET_HEADS * RET_DK, RET_HEADS * RET_DV
    return pl.pallas_call(
        _ret_kernel,
        grid=(batch, seq_len // BLK),
        in_specs=[
            pl.BlockSpec((pairs, 2, BLK), lambda b, c: (0, 0, 0)),
            pl.BlockSpec((1, BLK, qk_w), lambda b, c: (b, c, 0)),
            pl.BlockSpec((1, BLK, qk_w), lambda b, c: (b, c, 0)),
            pl.BlockSpec((1, BLK, v_w), lambda b, c: (b, c, 0)),
        ],
        out_specs=pl.BlockSpec((1, BLK, v_w), lambda b, c: (b, c, 0)),
        out_shape=jax.ShapeDtypeStruct((batch, seq_len, v_w), BF16),
        scratch_shapes=[
            pltpu.VMEM((pairs, LANES, 2 * RET_DV), F32),
            pltpu.VMEM((RET_HEADS, BLK, BLK), F32),
            pltpu.VMEM((pairs, BLK, LANES), F32),
            pltpu.VMEM((pairs, BLK, 2 * RET_DV), F32),
            pltpu.VMEM((2 * pairs, LANES, 2 * RET_DV), F32),
        ],
        compiler_params=_params("parallel", "arbitrary"),
        name="retention",
    )(lg_tab, rq, rk, rv)


def _sb_kernel(q_ref, k_ref, v_ref, o_ref, acc_ref, carry_ref):
    qb = pl.program_id(1)
    pairs = SB_HEADS // 2
    lane = lax.broadcasted_iota(jnp.int32, (BLK, LANES), 1)
    q_heads = []
    for p in range(pairs):
        q = q_ref[0, :, p * LANES:(p + 1) * LANES]
        zero = jnp.zeros_like(q)
        q_heads += [jnp.where(lane < SB_DH, q, zero), jnp.where(lane >= SB_DH, q, zero)]
    j = lax.broadcasted_iota(jnp.int32, (BLK, BLK), 0)
    c = lax.broadcasted_iota(jnp.int32, (BLK, BLK), 1)
    suffix = jnp.where(j >= c, 1.0, 0.0).astype(BF16)
    key_col = lax.broadcasted_iota(jnp.int32, (1, BLK), 1)

    def step(blocks, first):
        chains = [(kb, bias, h) for kb, bias in blocks for h in range(SB_HEADS)]
        ys, logs, totals, sums = {}, {}, {}, {}
        carries = [None if first else carry_ref[h] for h in range(SB_HEADS)]
        accs = [None] * SB_HEADS

        def front(n):
            kb, bias, h = chains[n]
            cols = slice((h // 2) * LANES, (h // 2 + 1) * LANES)
            start = pl.multiple_of(kb * BLK, BLK)
            y = _dot_nt(q_heads[h], k_ref[0, pl.ds(start, BLK), cols]) + bias
            log_1mb = jnp.minimum(y, 0.0) - jnp.log2(1.0 + jnp.exp2(-jnp.abs(y)))
            ys[n], logs[n] = y, log_1mb.astype(BF16)
            totals[n] = jnp.sum(log_1mb, axis=1, keepdims=True)

        def back(n):
            kb, _, h = chains[n]
            cols = slice((h // 2) * LANES, (h // 2 + 1) * LANES)
            start = pl.multiple_of(kb * BLK, BLK)
            incl = sums.pop(n) if carries[h] is None else sums.pop(n) + carries[h]
            a = jnp.exp2(incl - ys.pop(n)).astype(BF16)
            pv = _dot(a, v_ref[0, pl.ds(start, BLK), cols])
            carries[h] = totals[n] if carries[h] is None else carries[h] + totals[n]
            accs[h] = pv if accs[h] is None else accs[h] + pv

        for t in range(len(chains) + 2):
            if t < len(chains):
                front(t)
            if 0 <= t - 1 < len(chains):
                sums[t - 1] = _dot(logs.pop(t - 1), suffix)
            if 0 <= t - 2 < len(chains):
                back(t - 2)
        for h in range(SB_HEADS):
            carry_ref[h] = carries[h]
            if first:
                acc_ref[h] = accs[h]
            else:
                acc_ref[h] += accs[h]

    def valid_bias(kb):
        return jnp.where(kb * BLK + key_col < PAD, MASK_BIG, 0.0).astype(F32)

    r = lax.broadcasted_iota(jnp.int32, (BLK, BLK), 0)
    s = lax.broadcasted_iota(jnp.int32, (BLK, BLK), 1)
    causal = jnp.where(s < r, 0.0, MASK_BIG).astype(F32)

    @pl.when(qb == 0)
    def _first_block():
        step([(0, causal + valid_bias(0))], True)

    @pl.when(qb > 0)
    def _diagonal_and_previous():
        step([(qb, causal), (qb - 1, valid_bias(qb - 1))], True)

    def decayed():
        top = carry_ref[0]
        for h in range(1, SB_HEADS):
            top = jnp.maximum(top, carry_ref[h])
        return jnp.max(top) < UNDERFLOW_LOG2

    def live(state):
        kb, done = state
        return jnp.logical_and(kb >= 0, jnp.logical_not(done))

    def body(state):
        kb, _ = state
        step([(kb, valid_bias(kb))], False)
        return kb - 1, decayed()

    lax.while_loop(live, body, (qb - 2, decayed()))
    for p in range(pairs):
        o_ref[0, :, p * LANES:(p + 1) * LANES] = jnp.where(
            lane < SB_DH, acc_ref[2 * p], acc_ref[2 * p + 1]).astype(BF16)


def _stick_breaking(sq, sk, sv):
    batch, seq_len, width = sq.shape
    assert PAD <= BLK and width == SB_HEADS * SB_DH
    return pl.pallas_call(
        _sb_kernel,
        grid=(batch, seq_len // BLK),
        in_specs=[
            pl.BlockSpec((1, BLK, width), lambda b, i: (b, i, 0)),
            pl.BlockSpec((1, seq_len, width), lambda b, i: (b, 0, 0)),
            pl.BlockSpec((1, seq_len, width), lambda b, i: (b, 0, 0)),
        ],
        out_specs=pl.BlockSpec((1, BLK, width), lambda b, i: (b, i, 0)),
        out_shape=jax.ShapeDtypeStruct((batch, seq_len, width), BF16),
        scratch_shapes=[pltpu.VMEM((SB_HEADS, BLK, LANES), F32),
                        pltpu.VMEM((SB_HEADS, BLK, 1), F32)],
        compiler_params=_params("parallel", "parallel"),
        name="stick_breaking",
    )(sq, sk, sv)


def _merge_kernel(x_ref, g_ref, wrg_ref, wga0_ref, wga1_ref, wgb0_ref, wgb1_ref, gn_ref,
                  or_ref, os_ref, wr_ref, ws_ref, wo_ref, o_ref):
    x = x_ref[...]
    u = _rmsnorm(x, g_ref[...]).astype(BF16)

    def gate(*w_refs):
        return jnp.concatenate([_dot(u, w[...].astype(BF16)) for w in w_refs], axis=1)

    rg = gate(wrg_ref)
    gated = (rg * _sigmoid(rg)) * (or_ref[...].astype(F32) * gn_ref[...])
    y_ret = _dot(gated.astype(BF16), wr_ref[...].astype(BF16))
    y_sb = _dot(os_ref[...], ws_ref[...].astype(BF16))
    y = _sigmoid(gate(wga0_ref, wga1_ref)) * y_ret + _sigmoid(gate(wgb0_ref, wgb1_ref)) * y_sb
    o_ref[...] = x + _dot(y.astype(BF16), wo_ref[...].astype(BF16))


def _merge(h, layer, g, w_in, gn, o_r, o_s, w_ret, w_sb, w_out, *, tile, in_map, out_rows):
    def in_spec(width):
        return pl.BlockSpec((tile, width), lambda i: (in_map(i), 0))

    half = D_MODEL // 2
    assert C_RG % D_MODEL == 0 and C_GA % half == 0
    ret_w, sb_w = RET_HEADS * RET_DV, SB_HEADS * SB_DH
    return pl.pallas_call(
        _merge_kernel,
        grid=(out_rows // tile,),
        in_specs=[
            in_spec(D_MODEL), _layer_spec(layer, (1, D_MODEL)),
            _layer_spec(layer, (D_MODEL, D_MODEL), col_block=C_RG // D_MODEL),
            _layer_spec(layer, (D_MODEL, half), col_block=C_GA // half),
            _layer_spec(layer, (D_MODEL, half), col_block=C_GA // half + 1),
            _layer_spec(layer, (D_MODEL, half), col_block=C_GB // half),
            _layer_spec(layer, (D_MODEL, half), col_block=C_GB // half + 1),
            _layer_spec(layer, (1, ret_w)), in_spec(ret_w), in_spec(sb_w),
            _layer_spec(layer, (ret_w, D_MODEL)), _layer_spec(layer, (sb_w, D_MODEL)),
            _layer_spec(layer, (D_MODEL, D_MODEL)),
        ],
        out_specs=pl.BlockSpec((tile, D_MODEL), lambda i: (i, 0)),
        out_shape=jax.ShapeDtypeStruct((out_rows, D_MODEL), F32),
        compiler_params=_params("parallel"),
        name="mixer_merge",
    )(h, g, w_in, w_in, w_in, w_in, w_in, gn, o_r, o_s, w_ret, w_sb, w_out)


def kernel(x, meta, ffn1_norm, ffn1_w_gu, ffn1_w_down, mix_norm, w_in, ret_gn, w_ret_proj,
           w_sb_proj, w_out, ffn2_norm, ffn2_w_gu, ffn2_w_down, final_norm):
    batch, seq, d = x.shape
    seq_len = seq + BLK
    rows = batch * seq_len
    assert d == D_MODEL and seq % ROW_TILE == 0 and rows % ROW_TILE == 0

    h = jnp.concatenate([jnp.zeros((batch, PAD, d), x.dtype),
                         jnp.broadcast_to(meta.astype(x.dtype)[None], (batch, N_META, d)), x], axis=1)
    h = h.reshape(rows, d)

    pos = (jnp.arange(seq_len) - PAD).astype(F32)
    freqs = ROPE_BASE ** (-jnp.arange(0, RET_DK, 2, dtype=F32) / RET_DK)
    ang = pos[:, None] * freqs[None, :]
    cos_h, sin_h = jnp.cos(ang), jnp.sin(ang)
    cos_t = jnp.tile(jnp.concatenate([cos_h, cos_h] * 2, axis=1), (batch, 1))
    sin_t = jnp.tile(jnp.concatenate([-sin_h, sin_h] * 2, axis=1), (batch, 1))
    valid_t = jnp.tile(jnp.broadcast_to((pos >= 0).astype(F32)[:, None], (seq_len, LANES)), (batch, 1))
    log_gamma = jnp.log(1.0 - 2.0 ** (-5.0 - jnp.arange(RET_HEADS, dtype=F32)))
    lg_tab = jnp.broadcast_to(log_gamma.reshape(RET_HEADS // 2, 2, 1), (RET_HEADS // 2, 2, BLK))

    gains = lambda t: t.reshape(t.shape[0], 1, t.shape[1])
    final_gain = final_norm.reshape(1, d)
    shape3 = lambda t: t.reshape(batch, seq_len, t.shape[-1])
    real_blocks = seq // BLK
    for l in range(DEPTH):
        last = l == DEPTH - 1
        h = _ffn(h, l, gains(ffn1_norm), ffn1_w_gu, ffn1_w_down, final_gain, final_norm=False)
        rq, rk, rv, sq, sk, sv = _proj(h, l, gains(mix_norm), w_in, cos_t, sin_t, valid_t)
        o_r = _retention(lg_tab, shape3(rq), shape3(rk), shape3(rv)).reshape(rows, -1)
        o_s = _stick_breaking(shape3(sq), shape3(sk), shape3(sv)).reshape(rows, -1)
        merge = functools.partial(_merge, h, l, gains(mix_norm), w_in, gains(ret_gn), o_r, o_s,
                                  w_ret_proj, w_sb_proj, w_out)
        if last:
            h = merge(tile=BLK, out_rows=batch * seq, in_map=lambda i: i + i // real_blocks + 1)
        else:
            h = merge(tile=ROW_TILE, out_rows=rows, in_map=lambda i: i)
        h = _ffn(h, l, gains(ffn2_norm), ffn2_w_gu, ffn2_w_down, final_gain, final_norm=last)
    return h.reshape(batch, seq, d)
```

```python
import functools

import jax
import jax.numpy as jnp
import numpy as np
from jax import lax
from jax.experimental import pallas as pl
from jax.experimental.pallas import tpu as pltpu

D_MODEL = 1024
DEPTH = 2
N_META = 16
RET_HEADS = 8
RET_DK = 64
RET_DV = 128
SB_HEADS = 8
SB_DH = 64
D_FF = 2816
ROPE_BASE = 10000.0
EPS = 1e-6

LANES = 128
BLK = 256
PAD = BLK - N_META
FF_CHUNK = 256
ROW_TILE = 512
MASK_BIG = 1e30
LOG2E = 1.4426950408889634
UNDERFLOW_LOG2 = -160.0
VMEM_LIMIT = 56 * 1024 * 1024

_COLS = np.cumsum([0, 512, 512, 1024, 1024, 512, 512, 512, 1024, 1024])
C_RQ, C_RK, C_RV, C_RG, C_SQ, C_SK, C_SV, C_GA, C_GB, C_END = (int(c) for c in _COLS)

F32 = jnp.float32
BF16 = jnp.bfloat16


def _dot(a, b):
    return jnp.dot(a, b, preferred_element_type=F32)


def _dot_nt(a, b):
    return lax.dot_general(a, b, (((1,), (1,)), ((), ())), preferred_element_type=F32)


def _rmsnorm(x, g):
    r = lax.rsqrt(jnp.mean(x * x, axis=-1, keepdims=True) + EPS)
    return (x * r) * g


def _sigmoid(x):
    return 1.0 / (1.0 + jnp.exp(-x))


def _params(*sem):
    return pltpu.CompilerParams(dimension_semantics=sem, vmem_limit_bytes=VMEM_LIMIT)


def _resident_spec(block, index_map):
    return pl.BlockSpec(block, index_map, pipeline_mode=pl.Buffered(1))


def _layer_spec(layer, shape, col_block=0):
    return _resident_spec((None,) + shape, lambda *_: (layer, 0, col_block))


def _ffn_kernel(x_ref, tail_ref, g_ref, wgu_ref, wd_ref, gf_ref, o_ref, gate_ref, *, final_norm, tail_step):
    x = x_ref[...]
    if tail_step is not None:
        x = jnp.where(pl.program_id(0) == tail_step, tail_ref[...], x)
    u = _rmsnorm(x, g_ref[...]).astype(BF16)
    for c in range(0, D_FF, FF_CHUNK):
        a = _dot(u, wgu_ref[:, c:c + FF_CHUNK].astype(BF16))
        b = _dot(u, wgu_ref[:, D_FF + c:D_FF + c + FF_CHUNK].astype(BF16))
        gate_ref[:, c:c + FF_CHUNK] = (0.5 * a * _sigmoid(a) * b).astype(BF16)
    y = x + _dot(gate_ref[...], wd_ref[...].astype(BF16))
    if final_norm:
        y = _rmsnorm(y, gf_ref[...])
    o_ref[...] = y


def _ffn(h, layer, g, w_gu, w_down, gf, *, final_norm, tail=None):
    steps = h.shape[0] // ROW_TILE
    tail_step = None
    if tail is None:
        tail = gf
    else:
        tail_step, steps = steps, steps + 1
    last_in = h.shape[0] // ROW_TILE - 1
    return pl.pallas_call(
        functools.partial(_ffn_kernel, final_norm=final_norm, tail_step=tail_step),
        grid=(steps,),
        in_specs=[
            pl.BlockSpec((ROW_TILE, D_MODEL), lambda i: (jnp.minimum(i, last_in), 0)),
            _resident_spec(tail.shape, lambda i: (0, 0)),
            _layer_spec(layer, (1, D_MODEL)),
            _layer_spec(layer, (D_MODEL, 2 * D_FF)),
            _layer_spec(layer, (D_FF, D_MODEL)),
            _resident_spec((1, D_MODEL), lambda i: (0, 0)),
        ],
        out_specs=pl.BlockSpec((ROW_TILE, D_MODEL), lambda i: (i, 0)),
        out_shape=jax.ShapeDtypeStruct((steps * ROW_TILE, D_MODEL), F32),
        scratch_shapes=[pltpu.VMEM((ROW_TILE, D_FF), BF16)],
        compiler_params=_params("parallel"),
        name="ffn_final" if final_norm else "ffn",
    )(h, tail, g, w_gu, w_down, gf)


def _proj_kernel(x_ref, g_ref, wr_ref, ws_ref, cos_ref, sin_ref, valid_ref,
                 rq_ref, rk_ref, rv_ref, sq_ref, sk_ref, sv_ref):
    u = _rmsnorm(x_ref[...], g_ref[...]).astype(BF16)
    cos = jnp.concatenate([cos_ref[...]] * (512 // LANES), axis=1)
    sin = jnp.concatenate([sin_ref[...]] * (512 // LANES), axis=1)
    valid = jnp.concatenate([valid_ref[...]] * (512 // LANES), axis=1)
    lane = lax.broadcasted_iota(jnp.int32, (ROW_TILE, 512), 1)
    first_half = (lane % RET_DK) < (RET_DK // 2)

    def rotary(t):
        partner = jnp.where(first_half, pltpu.roll(t, 512 - RET_DK // 2, 1),
                            pltpu.roll(t, RET_DK // 2, 1))
        return t * cos + partner * sin

    def proj(w_ref, lo, hi):
        return _dot(u, w_ref[:, lo:hi].astype(BF16))

    rq_ref[...] = rotary(proj(wr_ref, C_RQ, C_RK)).astype(BF16)
    rk_ref[...] = (rotary(proj(wr_ref, C_RK, C_RV)) * (valid * RET_DK ** -0.5)).astype(BF16)
    rv_ref[...] = (proj(wr_ref, C_RV, C_RG) * jnp.concatenate([valid, valid], axis=1)).astype(BF16)
    sq_ref[...] = (proj(ws_ref, 0, 512) * (-LOG2E * SB_DH ** -0.5)).astype(BF16)
    sk_ref[...] = proj(ws_ref, 512, 1024).astype(BF16)
    sv_ref[...] = proj(ws_ref, 1024, 1536).astype(BF16)


def _proj(h, layer, g, w_in, cos_t, sin_t, valid_t):
    rows = h.shape[0]

    def row_spec(width):
        return pl.BlockSpec((ROW_TILE, width), lambda i: (i, 0))

    tab_spec = row_spec(LANES)
    widths = (512, 512, 1024, 512, 512, 512)
    sb_cols = C_GA - C_SQ
    assert C_SQ % sb_cols == 0
    return pl.pallas_call(
        _proj_kernel,
        grid=(rows // ROW_TILE,),
        in_specs=[row_spec(D_MODEL), _layer_spec(layer, (1, D_MODEL)),
                  _layer_spec(layer, (D_MODEL, C_RG)),
                  _layer_spec(layer, (D_MODEL, sb_cols), col_block=C_SQ // sb_cols),
                  tab_spec, tab_spec, tab_spec],
        out_specs=[row_spec(w) for w in widths],
        out_shape=[jax.ShapeDtypeStruct((rows, w), BF16) for w in widths],
        compiler_params=_params("parallel"),
        name="mixer_proj",
    )(h, g, w_in, w_in, cos_t, sin_t, valid_t)


def _ret_kernel(lg_ref, q_ref, k_ref, v_ref, *rest):
    o_ref, state_ref, dec_ref, kdec_ref, qdec_ref, sdec_ref = rest[-6:]
    c = pl.program_id(1)
    pairs = RET_HEADS // 2

    @pl.when(c == 0)
    def _init():
        t = lax.broadcasted_iota(jnp.int32, (BLK, BLK), 0)
        s = lax.broadcasted_iota(jnp.int32, (BLK, BLK), 1)
        diff = (t - s).astype(F32)
        lane = lax.broadcasted_iota(jnp.int32, (BLK, LANES), 1)
        pos = lax.broadcasted_iota(jnp.int32, (BLK, LANES), 0).astype(F32)
        col = lax.broadcasted_iota(jnp.int32, (BLK, 2 * RET_DV), 1)
        posv = lax.broadcasted_iota(jnp.int32, (BLK, 2 * RET_DV), 0).astype(F32)
        srow = lax.broadcasted_iota(jnp.int32, (LANES, 2 * RET_DV), 0)
        scol = lax.broadcasted_iota(jnp.int32, (LANES, 2 * RET_DV), 1)
        own = (srow // RET_DK) == (scol // RET_DV)
        for p in range(pairs):
            lg = lg_ref[p]
            for i in range(2):
                dec_ref[2 * p + i] = jnp.where(
                    diff >= 0, jnp.exp(lg[i:i + 1, :] * jnp.maximum(diff, 0.0)), 0.0)
            lg_k = jnp.where(lane < RET_DK, lg[0:1, :LANES], lg[1:2, :LANES])
            kdec_ref[p] = jnp.exp(lg_k * (BLK - 1.0 - pos))
            lg_v = jnp.where(col < RET_DV, lg[0:1, :], lg[1:2, :])
            qdec_ref[p] = jnp.exp(lg_v * (posv + 1.0))
            lg_s = jnp.where(scol < RET_DV, lg[0:1, :], lg[1:2, :])
            sdec_ref[2 * p] = jnp.where(own, jnp.exp(lg_s * float(BLK)), 0.0)
            sdec_ref[2 * p + 1] = jnp.where(own, 1.0, 0.0)
        state_ref[...] = jnp.zeros_like(state_ref)

    lane = lax.broadcasted_iota(jnp.int32, (BLK, LANES), 1)
    qs, ks, crosses, scores, outs = {}, {}, {}, {}, {}

    def pair_start(p):
        qs[p] = q_ref[:, p * LANES:(p + 1) * LANES]
        ks[p] = k_ref[:, p * LANES:(p + 1) * LANES]
        crosses[p] = _dot(qs[p], state_ref[p].astype(BF16)) * qdec_ref[p]

    def head_scores(head):
        p, i = divmod(head, 2)
        in_head = (lane < RET_DK) if i == 0 else (lane >= RET_DK)
        qi = jnp.where(in_head, qs[p], jnp.zeros_like(qs[p]))
        scores[head] = (_dot_nt(qi, ks[p]) * dec_ref[head]).astype(BF16)

    def head_values(head):
        p, i = divmod(head, 2)
        o = _dot(scores.pop(head), v_ref[:, head * RET_DV:(head + 1) * RET_DV])
        outs[head] = o + crosses[p][:, i * RET_DV:(i + 1) * RET_DV]

    def head_norm(head):
        o = outs.pop(head)
        mu = jnp.mean(o, axis=-1, keepdims=True)
        d = o - mu
        var = jnp.mean(d * d, axis=-1, keepdims=True)
        o_ref[:, head * RET_DV:(head + 1) * RET_DV] = (d * lax.rsqrt(var + EPS)).astype(BF16)

    def pair_state(p):
        v = v_ref[:, 2 * p * RET_DV:2 * (p + 1) * RET_DV]
        kd_t = (ks[p].astype(F32) * kdec_ref[p]).T.astype(BF16)
        state_ref[p] = state_ref[p] * sdec_ref[2 * p] + _dot(kd_t, v) * sdec_ref[2 * p + 1]

    for t in range(RET_HEADS + 2):
        if t < RET_HEADS:
            if t % 2 == 0:
                pair_start(t // 2)
            head_scores(t)
        if 0 <= t - 1 < RET_HEADS:
            head_values(t - 1)
        if 0 <= t - 2 < RET_HEADS:
            head_norm(t - 2)
            if (t - 2) % 2 == 1:
                pair_state((t - 2) // 2)


def _seq_block(batch_index, step, blocks_per_seq, meta_block):
    return jnp.where(step == 0, meta_block, batch_index * blocks_per_seq + step - 1)


def _retention(lg_tab, rq, rk, rv, batch):
    rows = rq.shape[0]
    pairs = RET_HEADS // 2
    qk_w, v_w = RET_HEADS * RET_DK, RET_HEADS * RET_DV
    meta_block = rows // BLK - 2
    per_seq = meta_block // batch

    def call(grid, rows_in, rows_out, filled, name):
        return pl.pallas_call(
            _ret_kernel,
            grid=grid,
            in_specs=[
                pl.BlockSpec((pairs, 2, BLK), lambda b, c: (0, 0, 0)),
                pl.BlockSpec((BLK, qk_w), rows_in),
                pl.BlockSpec((BLK, qk_w), rows_in),
                pl.BlockSpec((BLK, v_w), rows_in),
            ] + [pl.BlockSpec(memory_space=pl.ANY)] * len(filled),
            out_specs=pl.BlockSpec((BLK, v_w), rows_out),
            out_shape=jax.ShapeDtypeStruct((rows, v_w), BF16),
            input_output_aliases={4: 0} if filled else {},
            scratch_shapes=[
                pltpu.VMEM((pairs, LANES, 2 * RET_DV), F32),
                pltpu.VMEM((RET_HEADS, BLK, BLK), F32),
                pltpu.VMEM((pairs, BLK, LANES), F32),
                pltpu.VMEM((pairs, BLK, 2 * RET_DV), F32),
                pltpu.VMEM((2 * pairs, LANES, 2 * RET_DV), F32),
            ],
            compiler_params=_params("parallel", "arbitrary"),
            name=name,
        )(lg_tab, rq, rk, rv, *filled)

    out = call((2, 1), lambda b, c: (meta_block, 0), lambda b, c: (meta_block + b, 0),
               (), "retention_meta")
    return call((batch, per_seq + 1),
                lambda b, c: (_seq_block(b, c, per_seq, meta_block), 0),
                lambda b, c: (b * per_seq + jnp.maximum(c - 1, 0), 0), (out,), "retention")


def _sb_kernel(q_ref, km_ref, vm_ref, kr_ref, vr_ref, *rest, meta_only):
    o_ref, k_ref, v_ref, acc_ref, carry_ref = rest[-5:]
    step_id = pl.program_id(1)

    @pl.when(step_id == 0)
    def _assemble_sequence():
        k_ref[0:BLK] = km_ref[...]
        v_ref[0:BLK] = vm_ref[...]
        if not meta_only:
            k_ref[BLK:] = kr_ref[...]
            v_ref[BLK:] = vr_ref[...]

    pairs = SB_HEADS // 2
    lane = lax.broadcasted_iota(jnp.int32, (BLK, LANES), 1)
    q_heads = []
    for p in range(pairs):
        q = q_ref[:, p * LANES:(p + 1) * LANES]
        zero = jnp.zeros_like(q)
        q_heads += [jnp.where(lane < SB_DH, q, zero), jnp.where(lane >= SB_DH, q, zero)]
    j = lax.broadcasted_iota(jnp.int32, (BLK, BLK), 0)
    c = lax.broadcasted_iota(jnp.int32, (BLK, BLK), 1)
    suffix = jnp.where(j >= c, 1.0, 0.0).astype(BF16)
    key_col = lax.broadcasted_iota(jnp.int32, (1, BLK), 1)

    def step(blocks, first):
        chains = [(kb, bias, h) for kb, bias in blocks for h in range(SB_HEADS)]
        ys, logs, totals, sums = {}, {}, {}, {}
        carries = [None if first else carry_ref[h] for h in range(SB_HEADS)]
        accs = [None] * SB_HEADS

        def front(n):
            kb, bias, h = chains[n]
            cols = slice((h // 2) * LANES, (h // 2 + 1) * LANES)
            start = kb * BLK if isinstance(kb, int) else pl.multiple_of(kb * BLK, BLK)
            y = _dot_nt(q_heads[h], k_ref[pl.ds(start, BLK), cols]) + bias
            log_1mb = jnp.minimum(y, 0.0) - jnp.log2(1.0 + jnp.exp2(-jnp.abs(y)))
            ys[n], logs[n] = y, log_1mb.astype(BF16)
            totals[n] = jnp.sum(log_1mb, axis=1, keepdims=True)

        def back(n):
            kb, _, h = chains[n]
            cols = slice((h // 2) * LANES, (h // 2 + 1) * LANES)
            start = kb * BLK if isinstance(kb, int) else pl.multiple_of(kb * BLK, BLK)
            incl = sums.pop(n) if carries[h] is None else sums.pop(n) + carries[h]
            a = jnp.exp2(incl - ys.pop(n)).astype(BF16)
            pv = _dot(a, v_ref[pl.ds(start, BLK), cols])
            carries[h] = totals[n] if carries[h] is None else carries[h] + totals[n]
            accs[h] = pv if accs[h] is None else accs[h] + pv

        for t in range(len(chains) + 2):
            if t < len(chains):
                front(t)
            if 0 <= t - 1 < len(chains):
                sums[t - 1] = _dot(logs.pop(t - 1), suffix)
            if 0 <= t - 2 < len(chains):
                back(t - 2)
        for h in range(SB_HEADS):
            carry_ref[h] = carries[h]
            if first:
                acc_ref[h] = accs[h]
            else:
                acc_ref[h] += accs[h]

    def valid_bias(kb):
        return jnp.where(kb * BLK + key_col < PAD, MASK_BIG, 0.0).astype(F32)

    r = lax.broadcasted_iota(jnp.int32, (BLK, BLK), 0)
    s = lax.broadcasted_iota(jnp.int32, (BLK, BLK), 1)
    causal = jnp.where(s < r, 0.0, MASK_BIG).astype(F32)

    if meta_only:
        step([(0, causal + valid_bias(0))], True)
    else:
        qb = step_id + 1
        step([(qb, causal), (qb - 1, valid_bias(qb - 1))], True)

        def decayed():
            top = carry_ref[0]
            for h in range(1, SB_HEADS):
                top = jnp.maximum(top, carry_ref[h])
            return jnp.max(top) < UNDERFLOW_LOG2

        def live(state):
            kb, done = state
            return jnp.logical_and(kb >= 0, jnp.logical_not(done))

        def body(state):
            kb, _ = state
            step([(kb, valid_bias(kb))], False)
            return kb - 1, decayed()

        lax.while_loop(live, body, (qb - 2, decayed()))
    for p in range(pairs):
        o_ref[:, p * LANES:(p + 1) * LANES] = jnp.where(
            lane < SB_DH, acc_ref[2 * p], acc_ref[2 * p + 1]).astype(BF16)


def _stick_breaking(sq, sk, sv, batch):
    rows, width = sq.shape
    assert PAD <= BLK and width == SB_HEADS * SB_DH
    meta_block = rows // BLK - 2
    per_seq = meta_block // batch
    seq = per_seq * BLK
    meta_spec = pl.BlockSpec((BLK, width), lambda b, i: (meta_block, 0))

    def call(grid, q_rows, out_rows, seq_of, filled, name):
        real_spec = pl.BlockSpec((seq, width), lambda b, i: (seq_of(b), 0))
        return pl.pallas_call(
            functools.partial(_sb_kernel, meta_only=not filled),
            grid=grid,
            in_specs=[pl.BlockSpec((BLK, width), q_rows), meta_spec, meta_spec, real_spec, real_spec]
            + [pl.BlockSpec(memory_space=pl.ANY)] * len(filled),
            out_specs=pl.BlockSpec((BLK, width), out_rows),
            out_shape=jax.ShapeDtypeStruct((rows, width), BF16),
            input_output_aliases={5: 0} if filled else {},
            scratch_shapes=[pltpu.VMEM((seq + BLK, width), BF16),
                            pltpu.VMEM((seq + BLK, width), BF16),
                            pltpu.VMEM((SB_HEADS, BLK, LANES), F32),
                            pltpu.VMEM((SB_HEADS, BLK, 1), F32)],
            compiler_params=_params("parallel", "arbitrary"),
            name=name,
        )(sq, sk, sv, sk, sv, *filled)

    out = call((2, 1), lambda b, i: (meta_block, 0), lambda b, i: (meta_block + b, 0),
               lambda b: 0, (), "stick_breaking_meta")
    return call((batch, per_seq), lambda b, i: (b * per_seq + i, 0), lambda b, i: (b * per_seq + i, 0),
                lambda b: b, (out,), "stick_breaking")


def _merge_kernel(x_ref, g_ref, wrg_ref, wga0_ref, wga1_ref, wgb0_ref, wgb1_ref, gn_ref,
                  or_ref, os_ref, wr_ref, ws_ref, wo_ref, o_ref):
    x = x_ref[...]
    u = _rmsnorm(x, g_ref[...]).astype(BF16)

    def gate(*w_refs):
        return jnp.concatenate([_dot(u, w[...].astype(BF16)) for w in w_refs], axis=1)

    rg = gate(wrg_ref)
    y_sb = _dot(os_ref[...], ws_ref[...].astype(BF16))
    gb = _sigmoid(gate(wgb0_ref, wgb1_ref)) * y_sb
    ga = _sigmoid(gate(wga0_ref, wga1_ref))
    gated = (rg * _sigmoid(rg)) * (or_ref[...].astype(F32) * gn_ref[...])
    y_ret = _dot(gated.astype(BF16), wr_ref[...].astype(BF16))
    y = ga * y_ret + gb
    o_ref[...] = x + _dot(y.astype(BF16), wo_ref[...].astype(BF16))


def _merge(h, layer, g, w_in, gn, o_r, o_s, w_ret, w_sb, w_out, *, out_rows):
    def in_spec(width):
        return pl.BlockSpec((ROW_TILE, width), lambda i: (i, 0))

    half = D_MODEL // 2
    assert C_RG % D_MODEL == 0 and C_GA % half == 0
    ret_w, sb_w = RET_HEADS * RET_DV, SB_HEADS * SB_DH
    return pl.pallas_call(
        _merge_kernel,
        grid=(out_rows // ROW_TILE,),
        in_specs=[
            in_spec(D_MODEL), _layer_spec(layer, (1, D_MODEL)),
            _layer_spec(layer, (D_MODEL, D_MODEL), col_block=C_RG // D_MODEL),
            _layer_spec(layer, (D_MODEL, half), col_block=C_GA // half),
            _layer_spec(layer, (D_MODEL, half), col_block=C_GA // half + 1),
            _layer_spec(layer, (D_MODEL, half), col_block=C_GB // half),
            _layer_spec(layer, (D_MODEL, half), col_block=C_GB // half + 1),
            _layer_spec(layer, (1, ret_w)), in_spec(ret_w), in_spec(sb_w),
            _layer_spec(layer, (ret_w, D_MODEL)), _layer_spec(layer, (sb_w, D_MODEL)),
            _layer_spec(layer, (D_MODEL, D_MODEL)),
        ],
        out_specs=pl.BlockSpec((ROW_TILE, D_MODEL), lambda i: (i, 0)),
        out_shape=jax.ShapeDtypeStruct((out_rows, D_MODEL), F32),
        compiler_params=_params("parallel"),
        name="mixer_merge",
    )(h, g, w_in, w_in, w_in, w_in, w_in, gn, o_r, o_s, w_ret, w_sb, w_out)


def kernel(x, meta, ffn1_norm, ffn1_w_gu, ffn1_w_down, mix_norm, w_in, ret_gn, w_ret_proj,
           w_sb_proj, w_out, ffn2_norm, ffn2_w_gu, ffn2_w_down, final_norm):
    batch, seq, d = x.shape
    real_rows = batch * seq
    rows = real_rows + ROW_TILE
    assert d == D_MODEL and seq % ROW_TILE == 0 and ROW_TILE == 2 * BLK and batch >= 2

    meta_block = jnp.concatenate([jnp.zeros((PAD, d), x.dtype), meta.astype(x.dtype)], axis=0)
    meta_tile = jnp.concatenate([meta_block, meta_block], axis=0)

    pos_meta = (jnp.arange(BLK) - PAD).astype(F32)
    pos_real = (jnp.arange(seq) + N_META).astype(F32)
    pos = jnp.concatenate([jnp.tile(pos_real, batch), pos_meta, pos_meta])
    freqs = ROPE_BASE ** (-jnp.arange(0, RET_DK, 2, dtype=F32) / RET_DK)
    ang = pos[:, None] * freqs[None, :]
    cos_h, sin_h = jnp.cos(ang), jnp.sin(ang)
    cos_t = jnp.concatenate([cos_h, cos_h] * 2, axis=1)
    sin_t = jnp.concatenate([-sin_h, sin_h] * 2, axis=1)
    valid_t = jnp.broadcast_to((pos >= 0).astype(F32)[:, None], (rows, LANES))
    log_gamma = jnp.log(1.0 - 2.0 ** (-5.0 - jnp.arange(RET_HEADS, dtype=F32)))
    lg_tab = jnp.broadcast_to(log_gamma.reshape(RET_HEADS // 2, 2, 1), (RET_HEADS // 2, 2, BLK))

    gains = lambda t: t.reshape(t.shape[0], 1, t.shape[1])
    final_gain = final_norm.reshape(1, d)
    h = x.reshape(real_rows, d)
    for l in range(DEPTH):
        last = l == DEPTH - 1
        h = _ffn(h, l, gains(ffn1_norm), ffn1_w_gu, ffn1_w_down, final_gain, final_norm=False,
                 tail=meta_tile if l == 0 else None)
        rq, rk, rv, sq, sk, sv = _proj(h, l, gains(mix_norm), w_in, cos_t, sin_t, valid_t)
        o_r = _retention(lg_tab, rq, rk, rv, batch)
        o_s = _stick_breaking(sq, sk, sv, batch)
        h = _merge(h, l, gains(mix_norm), w_in, gains(ret_gn), o_r, o_s, w_ret_proj, w_sb_proj, w_out,
                   out_rows=real_rows if last else rows)
        h = _ffn(h, l, gains(ffn2_norm), ffn2_w_gu, ffn2_w_down, final_gain, final_norm=last)
    return h.reshape(batch, seq, d)
```

```python
import functools

import jax
import jax.numpy as jnp
import numpy as np
from jax import lax
from jax.experimental import pallas as pl
from jax.experimental.pallas import tpu as pltpu

D_MODEL = 1024
DEPTH = 2
N_META = 16
RET_HEADS = 8
RET_DK = 64
RET_DV = 128
SB_HEADS = 8
SB_DH = 64
D_FF = 2816
ROPE_BASE = 10000.0
EPS = 1e-6

LANES = 128
BLK = 256
PAD = BLK - N_META
FF_CHUNK = 256
ROW_TILE = 512
MASK_BIG = 1e30
LOG2E = 1.4426950408889634
UNDERFLOW_LOG2 = -160.0
VMEM_LIMIT = 56 * 1024 * 1024

_COLS = np.cumsum([0, 512, 512, 1024, 1024, 512, 512, 512, 1024, 1024])
C_RQ, C_RK, C_RV, C_RG, C_SQ, C_SK, C_SV, C_GA, C_GB, C_END = (int(c) for c in _COLS)

F32 = jnp.float32
BF16 = jnp.bfloat16


def _dot(a, b):
    return jnp.dot(a, b, preferred_element_type=F32)


def _dot_nt(a, b):
    return lax.dot_general(a, b, (((1,), (1,)), ((), ())), preferred_element_type=F32)


def _rmsnorm(x, g):
    r = lax.rsqrt(jnp.mean(x * x, axis=-1, keepdims=True) + EPS)
    return (x * r) * g


def _sigmoid(x):
    return 1.0 / (1.0 + jnp.exp(-x))


def _params(*sem):
    return pltpu.CompilerParams(dimension_semantics=sem, vmem_limit_bytes=VMEM_LIMIT)


def _resident_spec(block, index_map):
    return pl.BlockSpec(block, index_map, pipeline_mode=pl.Buffered(1))


def _layer_spec(layer, shape, col_block=0):
    return _resident_spec((None,) + shape, lambda *_: (layer, 0, col_block))


def _ffn_kernel(x_ref, tail_ref, g_ref, wgu_ref, wd_ref, gf_ref, o_ref, gate_ref, *, final_norm, tail_step):
    x = x_ref[...]
    if tail_step is not None:
        x = jnp.where(pl.program_id(0) == tail_step, tail_ref[...], x)
    u = _rmsnorm(x, g_ref[...]).astype(BF16)
    for c in range(0, D_FF, FF_CHUNK):
        a = _dot(u, wgu_ref[:, c:c + FF_CHUNK].astype(BF16))
        b = _dot(u, wgu_ref[:, D_FF + c:D_FF + c + FF_CHUNK].astype(BF16))
        gate_ref[:, c:c + FF_CHUNK] = (0.5 * a * _sigmoid(a) * b).astype(BF16)
    y = x + _dot(gate_ref[...], wd_ref[...].astype(BF16))
    if final_norm:
        y = _rmsnorm(y, gf_ref[...])
    o_ref[...] = y


def _ffn(h, layer, g, w_gu, w_down, gf, *, final_norm, tail=None):
    steps = h.shape[0] // ROW_TILE
    tail_step = None
    if tail is None:
        tail = gf
    else:
        tail_step, steps = steps, steps + 1
    last_in = h.shape[0] // ROW_TILE - 1
    return pl.pallas_call(
        functools.partial(_ffn_kernel, final_norm=final_norm, tail_step=tail_step),
        grid=(steps,),
        in_specs=[
            pl.BlockSpec((ROW_TILE, D_MODEL), lambda i: (jnp.minimum(i, last_in), 0)),
            _resident_spec(tail.shape, lambda i: (0, 0)),
            _layer_spec(layer, (1, D_MODEL)),
            _layer_spec(layer, (D_MODEL, 2 * D_FF)),
            _layer_spec(layer, (D_FF, D_MODEL)),
            _resident_spec((1, D_MODEL), lambda i: (0, 0)),
        ],
        out_specs=pl.BlockSpec((ROW_TILE, D_MODEL), lambda i: (i, 0)),
        out_shape=jax.ShapeDtypeStruct((steps * ROW_TILE, D_MODEL), F32),
        scratch_shapes=[pltpu.VMEM((ROW_TILE, D_FF), BF16)],
        compiler_params=_params("parallel"),
        name="ffn_final" if final_norm else "ffn",
    )(h, tail, g, w_gu, w_down, gf)


def _proj_kernel(x_ref, g_ref, wr_ref, ws_ref, cos_ref, sin_ref, valid_ref,
                 rq_ref, rk_ref, rv_ref, sq_ref, sk_ref, sv_ref):
    u = _rmsnorm(x_ref[...], g_ref[...]).astype(BF16)
    cos = jnp.concatenate([cos_ref[...]] * (512 // LANES), axis=1)
    sin = jnp.concatenate([sin_ref[...]] * (512 // LANES), axis=1)
    valid = jnp.concatenate([valid_ref[...]] * (512 // LANES), axis=1)
    lane = lax.broadcasted_iota(jnp.int32, (ROW_TILE, 512), 1)
    first_half = (lane % RET_DK) < (RET_DK // 2)

    def rotary(t):
        partner = jnp.where(first_half, pltpu.roll(t, 512 - RET_DK // 2, 1),
                            pltpu.roll(t, RET_DK // 2, 1))
        return t * cos + partner * sin

    def proj(w_ref, lo, hi):
        return _dot(u, w_ref[:, lo:hi].astype(BF16))

    rq_ref[...] = rotary(proj(wr_ref, C_RQ, C_RK)).astype(BF16)
    rk_ref[...] = (rotary(proj(wr_ref, C_RK, C_RV)) * (valid * RET_DK ** -0.5)).astype(BF16)
    rv_ref[...] = (proj(wr_ref, C_RV, C_RG) * jnp.concatenate([valid, valid], axis=1)).astype(BF16)
    sq_ref[...] = (proj(ws_ref, 0, 512) * (-LOG2E * SB_DH ** -0.5)).astype(BF16)
    sk_ref[...] = proj(ws_ref, 512, 1024).astype(BF16)
    sv_ref[...] = proj(ws_ref, 1024, 1536).astype(BF16)


def _proj(h, layer, g, w_in, cos_t, sin_t, valid_t):
    rows = h.shape[0]

    def row_spec(width):
        return pl.BlockSpec((ROW_TILE, width), lambda i: (i, 0))

    tab_spec = row_spec(LANES)
    widths = (512, 512, 1024, 512, 512, 512)
    sb_cols = C_GA - C_SQ
    assert C_SQ % sb_cols == 0
    return pl.pallas_call(
        _proj_kernel,
        grid=(rows // ROW_TILE,),
        in_specs=[row_spec(D_MODEL), _layer_spec(layer, (1, D_MODEL)),
                  _layer_spec(layer, (D_MODEL, C_RG)),
                  _layer_spec(layer, (D_MODEL, sb_cols), col_block=C_SQ // sb_cols),
                  tab_spec, tab_spec, tab_spec],
        out_specs=[row_spec(w) for w in widths],
        out_shape=[jax.ShapeDtypeStruct((rows, w), BF16) for w in widths],
        compiler_params=_params("parallel"),
        name="mixer_proj",
    )(h, g, w_in, w_in, cos_t, sin_t, valid_t)


def _ret_kernel(lg_ref, q_ref, k_ref, v_ref, *rest):
    o_ref, state_ref, dec_ref, kdec_ref, qdec_ref, sdec_ref = rest[-6:]
    c = pl.program_id(1)
    pairs = RET_HEADS // 2

    @pl.when(c == 0)
    def _init():
        t = lax.broadcasted_iota(jnp.int32, (BLK, BLK), 0)
        s = lax.broadcasted_iota(jnp.int32, (BLK, BLK), 1)
        diff = (t - s).astype(F32)
        lane = lax.broadcasted_iota(jnp.int32, (BLK, LANES), 1)
        pos = lax.broadcasted_iota(jnp.int32, (BLK, LANES), 0).astype(F32)
        col = lax.broadcasted_iota(jnp.int32, (BLK, 2 * RET_DV), 1)
        posv = lax.broadcasted_iota(jnp.int32, (BLK, 2 * RET_DV), 0).astype(F32)
        srow = lax.broadcasted_iota(jnp.int32, (LANES, 2 * RET_DV), 0)
        scol = lax.broadcasted_iota(jnp.int32, (LANES, 2 * RET_DV), 1)
        own = (srow // RET_DK) == (scol // RET_DV)
        for p in range(pairs):
            lg = lg_ref[p]
            for i in range(2):
                dec_ref[2 * p + i] = jnp.where(
                    diff >= 0, jnp.exp(lg[i:i + 1, :] * jnp.maximum(diff, 0.0)), 0.0)
            lg_k = jnp.where(lane < RET_DK, lg[0:1, :LANES], lg[1:2, :LANES])
            kdec_ref[p] = jnp.exp(lg_k * (BLK - 1.0 - pos))
            lg_v = jnp.where(col < RET_DV, lg[0:1, :], lg[1:2, :])
            qdec_ref[p] = jnp.exp(lg_v * (posv + 1.0))
            lg_s = jnp.where(scol < RET_DV, lg[0:1, :], lg[1:2, :])
            sdec_ref[2 * p] = jnp.where(own, jnp.exp(lg_s * float(BLK)), 0.0)
            sdec_ref[2 * p + 1] = jnp.where(own, 1.0, 0.0)
        state_ref[...] = jnp.zeros_like(state_ref)

    lane = lax.broadcasted_iota(jnp.int32, (BLK, LANES), 1)
    qs, ks, crosses, scores, outs = {}, {}, {}, {}, {}

    def pair_start(p):
        qs[p] = q_ref[:, p * LANES:(p + 1) * LANES]
        ks[p] = k_ref[:, p * LANES:(p + 1) * LANES]
        crosses[p] = _dot(qs[p], state_ref[p].astype(BF16)) * qdec_ref[p]

    def head_scores(head):
        p, i = divmod(head, 2)
        in_head = (lane < RET_DK) if i == 0 else (lane >= RET_DK)
        qi = jnp.where(in_head, qs[p], jnp.zeros_like(qs[p]))
        scores[head] = (_dot_nt(qi, ks[p]) * dec_ref[head]).astype(BF16)

    def head_values(head):
        p, i = divmod(head, 2)
        o = _dot(scores.pop(head), v_ref[:, head * RET_DV:(head + 1) * RET_DV])
        outs[head] = o + crosses[p][:, i * RET_DV:(i + 1) * RET_DV]

    def head_norm(head):
        o = outs.pop(head)
        mu = jnp.mean(o, axis=-1, keepdims=True)
        d = o - mu
        var = jnp.mean(d * d, axis=-1, keepdims=True)
        o_ref[:, head * RET_DV:(head + 1) * RET_DV] = (d * lax.rsqrt(var + EPS)).astype(BF16)

    def pair_state(p):
        v = v_ref[:, 2 * p * RET_DV:2 * (p + 1) * RET_DV]
        kd_t = (ks[p].astype(F32) * kdec_ref[p]).T.astype(BF16)
        state_ref[p] = state_ref[p] * sdec_ref[2 * p] + _dot(kd_t, v) * sdec_ref[2 * p + 1]

    for t in range(RET_HEADS + 2):
        if t < RET_HEADS:
            if t % 2 == 0:
                pair_start(t // 2)
            head_scores(t)
        if 0 <= t - 1 < RET_HEADS:
            head_values(t - 1)
        if 0 <= t - 2 < RET_HEADS:
            head_norm(t - 2)
            if (t - 2) % 2 == 1:
                pair_state((t - 2) // 2)


def _seq_block(batch_index, step, blocks_per_seq, meta_block):
    return jnp.where(step == 0, meta_block, batch_index * blocks_per_seq + step - 1)


def _retention(lg_tab, rq, rk, rv, batch):
    rows = rq.shape[0]
    pairs = RET_HEADS // 2
    qk_w, v_w = RET_HEADS * RET_DK, RET_HEADS * RET_DV
    meta_block = rows // BLK - 2
    per_seq = meta_block // batch

    def call(grid, rows_in, rows_out, filled, name):
        return pl.pallas_call(
            _ret_kernel,
            grid=grid,
            in_specs=[
                pl.BlockSpec((pairs, 2, BLK), lambda b, c: (0, 0, 0)),
                pl.BlockSpec((BLK, qk_w), rows_in),
                pl.BlockSpec((BLK, qk_w), rows_in),
                pl.BlockSpec((BLK, v_w), rows_in),
            ] + [pl.BlockSpec(memory_space=pl.ANY)] * len(filled),
            out_specs=pl.BlockSpec((BLK, v_w), rows_out),
            out_shape=jax.ShapeDtypeStruct((rows, v_w), BF16),
            input_output_aliases={4: 0} if filled else {},
            scratch_shapes=[
                pltpu.VMEM((pairs, LANES, 2 * RET_DV), F32),
                pltpu.VMEM((RET_HEADS, BLK, BLK), F32),
                pltpu.VMEM((pairs, BLK, LANES), F32),
                pltpu.VMEM((pairs, BLK, 2 * RET_DV), F32),
                pltpu.VMEM((2 * pairs, LANES, 2 * RET_DV), F32),
            ],
            compiler_params=_params("parallel", "arbitrary"),
            name=name,
        )(lg_tab, rq, rk, rv, *filled)

    out = call((2, 1), lambda b, c: (meta_block, 0), lambda b, c: (meta_block + b, 0),
               (), "retention_meta")
    return call((batch, per_seq + 1),
                lambda b, c: (_seq_block(b, c, per_seq, meta_block), 0),
                lambda b, c: (b * per_seq + jnp.maximum(c - 1, 0), 0), (out,), "retention")


def _sb_kernel(q_ref, km_ref, vm_ref, kr_ref, vr_ref, *rest, meta_only):
    o_ref, k_ref, v_ref, acc_ref, carry_ref = rest[-5:]
    step_id = pl.program_id(1)

    @pl.when(step_id == 0)
    def _assemble_sequence():
        k_ref[0:BLK] = km_ref[...]
        v_ref[0:BLK] = vm_ref[...]
        if not meta_only:
            k_ref[BLK:] = kr_ref[...]
            v_ref[BLK:] = vr_ref[...]

    pairs = SB_HEADS // 2
    lane = lax.broadcasted_iota(jnp.int32, (BLK, LANES), 1)
    q_heads = []
    for p in range(pairs):
        q = q_ref[:, p * LANES:(p + 1) * LANES]
        zero = jnp.zeros_like(q)
        q_heads += [jnp.where(lane < SB_DH, q, zero), jnp.where(lane >= SB_DH, q, zero)]
    j = lax.broadcasted_iota(jnp.int32, (BLK, BLK), 0)
    c = lax.broadcasted_iota(jnp.int32, (BLK, BLK), 1)
    suffix = jnp.where(j >= c, 1.0, 0.0).astype(BF16)
    key_col = lax.broadcasted_iota(jnp.int32, (1, BLK), 1)

    def step(blocks, first):
        chains = [(kb, bias, h) for kb, bias in blocks for h in range(SB_HEADS)]
        ys, logs, totals, sums = {}, {}, {}, {}
        carries = [None if first else carry_ref[h] for h in range(SB_HEADS)]
        accs = [None] * SB_HEADS

        def front(n):
            kb, bias, h = chains[n]
            cols = slice((h // 2) * LANES, (h // 2 + 1) * LANES)
            start = kb * BLK if isinstance(kb, int) else pl.multiple_of(kb * BLK, BLK)
            y = _dot_nt(q_heads[h], k_ref[pl.ds(start, BLK), cols]) + bias
            log_1mb = jnp.minimum(y, 0.0) - jnp.log2(1.0 + jnp.exp2(-jnp.abs(y)))
            ys[n], logs[n] = y, log_1mb.astype(BF16)
            totals[n] = jnp.sum(log_1mb, axis=1, keepdims=True)

        def back(n):
            kb, _, h = chains[n]
            cols = slice((h // 2) * LANES, (h // 2 + 1) * LANES)
            start = kb * BLK if isinstance(kb, int) else pl.multiple_of(kb * BLK, BLK)
            incl = sums.pop(n) if carries[h] is None else sums.pop(n) + carries[h]
            a = jnp.exp2(incl - ys.pop(n)).astype(BF16)
            pv = _dot(a, v_ref[pl.ds(start, BLK), cols])
            carries[h] = totals[n] if carries[h] is None else carries[h] + totals[n]
            accs[h] = pv if accs[h] is None else accs[h] + pv

        for t in range(len(chains) + 2):
            if t < len(chains):
                front(t)
            if 0 <= t - 1 < len(chains):
                sums[t - 1] = _dot(logs.pop(t - 1), suffix)
            if 0 <= t - 2 < len(chains):
                back(t - 2)
        for h in range(SB_HEADS):
            carry_ref[h] = carries[h]
            if first:
                acc_ref[h] = accs[h]
            else:
                acc_ref[h] += accs[h]

    def valid_bias(kb):
        return jnp.where(kb * BLK + key_col < PAD, MASK_BIG, 0.0).astype(F32)

    r = lax.broadcasted_iota(jnp.int32, (BLK, BLK), 0)
    s = lax.broadcasted_iota(jnp.int32, (BLK, BLK), 1)
    causal = jnp.where(s < r, 0.0, MASK_BIG).astype(F32)

    if meta_only:
        step([(0, causal + valid_bias(0))], True)
    else:
        qb = step_id + 1
        step([(qb, causal), (qb - 1, valid_bias(qb - 1))], True)

        def decayed():
            top = carry_ref[0]
            for h in range(1, SB_HEADS):
                top = jnp.maximum(top, carry_ref[h])
            return jnp.max(top) < UNDERFLOW_LOG2

        def live(state):
            kb, done = state
            return jnp.logical_and(kb >= 0, jnp.logical_not(done))

        def body(state):
            kb, _ = state
            step([(kb, valid_bias(kb))], False)
            return kb - 1, decayed()

        lax.while_loop(live, body, (qb - 2, decayed()))
    for p in range(pairs):
        o_ref[:, p * LANES:(p + 1) * LANES] = jnp.where(
            lane < SB_DH, acc_ref[2 * p], acc_ref[2 * p + 1]).astype(BF16)


def _stick_breaking(sq, sk, sv, batch):
    rows, width = sq.shape
    assert PAD <= BLK and width == SB_HEADS * SB_DH
    meta_block = rows // BLK - 2
    per_seq = meta_block // batch
    seq = per_seq * BLK
    meta_spec = pl.BlockSpec((BLK, width), lambda b, i: (meta_block, 0))

    def call(grid, q_rows, out_rows, seq_of, filled, name):
        real_spec = pl.BlockSpec((seq, width), lambda b, i: (seq_of(b), 0))
        return pl.pallas_call(
            functools.partial(_sb_kernel, meta_only=not filled),
            grid=grid,
            in_specs=[pl.BlockSpec((BLK, width), q_rows), meta_spec, meta_spec, real_spec, real_spec]
            + [pl.BlockSpec(memory_space=pl.ANY)] * len(filled),
            out_specs=pl.BlockSpec((BLK, width), out_rows),
            out_shape=jax.ShapeDtypeStruct((rows, width), BF16),
            input_output_aliases={5: 0} if filled else {},
            scratch_shapes=[pltpu.VMEM((seq + BLK, width), BF16),
                            pltpu.VMEM((seq + BLK, width), BF16),
                            pltpu.VMEM((SB_HEADS, BLK, LANES), F32),
                            pltpu.VMEM((SB_HEADS, BLK, 1), F32)],
            compiler_params=_params("parallel", "arbitrary"),
            name=name,
        )(sq, sk, sv, sk, sv, *filled)

    out = call((2, 1), lambda b, i: (meta_block, 0), lambda b, i: (meta_block + b, 0),
               lambda b: 0, (), "stick_breaking_meta")
    return call((batch, per_seq), lambda b, i: (b * per_seq + i, 0), lambda b, i: (b * per_seq + i, 0),
                lambda b: b, (out,), "stick_breaking")


def _merge_kernel(x_ref, g_ref, wrg_ref, wga0_ref, wga1_ref, wgb0_ref, wgb1_ref, gn_ref,
                  or_ref, os_ref, wr_ref, ws_ref, wo_ref, o_ref):
    x = x_ref[...]
    u = _rmsnorm(x, g_ref[...]).astype(BF16)

    def gate(*w_refs):
        return jnp.concatenate([_dot(u, w[...].astype(BF16)) for w in w_refs], axis=1)

    rg = gate(wrg_ref)
    y_sb = _dot(os_ref[...], ws_ref[...].astype(BF16))
    gb = _sigmoid(gate(wgb0_ref, wgb1_ref)) * y_sb
    ga = _sigmoid(gate(wga0_ref, wga1_ref))
    gated = (rg * _sigmoid(rg)) * (or_ref[...].astype(F32) * gn_ref[...])
    y_ret = _dot(gated.astype(BF16), wr_ref[...].astype(BF16))
    y = ga * y_ret + gb
    o_ref[...] = x + _dot(y.astype(BF16), wo_ref[...].astype(BF16))


def _merge(h, layer, g, w_in, gn, o_r, o_s, w_ret, w_sb, w_out, *, out_rows):
    def in_spec(width):
        return pl.BlockSpec((ROW_TILE, width), lambda i: (i, 0))

    half = D_MODEL // 2
    assert C_RG % D_MODEL == 0 and C_GA % half == 0
    ret_w, sb_w = RET_HEADS * RET_DV, SB_HEADS * SB_DH
    return pl.pallas_call(
        _merge_kernel,
        grid=(out_rows // ROW_TILE,),
        in_specs=[
            in_spec(D_MODEL), _layer_spec(layer, (1, D_MODEL)),
            _layer_spec(layer, (D_MODEL, D_MODEL), col_block=C_RG // D_MODEL),
            _layer_spec(layer, (D_MODEL, half), col_block=C_GA // half),
            _layer_spec(layer, (D_MODEL, half), col_block=C_GA // half + 1),
            _layer_spec(layer, (D_MODEL, half), col_block=C_GB // half),
            _layer_spec(layer, (D_MODEL, half), col_block=C_GB // half + 1),
            _layer_spec(layer, (1, ret_w)), in_spec(ret_w), in_spec(sb_w),
            _layer_spec(layer, (ret_w, D_MODEL)), _layer_spec(layer, (sb_w, D_MODEL)),
            _layer_spec(layer, (D_MODEL, D_MODEL)),
        ],
        out_specs=pl.BlockSpec((ROW_TILE, D_MODEL), lambda i: (i, 0)),
        out_shape=jax.ShapeDtypeStruct((out_rows, D_MODEL), F32),
        compiler_params=_params("parallel"),
        name="mixer_merge",
    )(h, g, w_in, w_in, w_in, w_in, w_in, gn, o_r, o_s, w_ret, w_sb, w_out)


def kernel(x, meta, ffn1_norm, ffn1_w_gu, ffn1_w_down, mix_norm, w_in, ret_gn, w_ret_proj,
           w_sb_proj, w_out, ffn2_norm, ffn2_w_gu, ffn2_w_down, final_norm):
    batch, seq, d = x.shape
    real_rows = batch * seq
    rows = real_rows + ROW_TILE
    assert d == D_MODEL and seq % ROW_TILE == 0 and ROW_TILE == 2 * BLK and batch >= 2

    meta_block = jnp.concatenate([jnp.zeros((PAD, d), x.dtype), meta.astype(x.dtype)], axis=0)
    meta_tile = jnp.concatenate([meta_block, meta_block], axis=0)

    f32 = np.float32
    pos_meta = (np.arange(BLK) - PAD).astype(f32)
    pos_real = (np.arange(seq) + N_META).astype(f32)
    pos = np.concatenate([np.tile(pos_real, batch), pos_meta, pos_meta])
    freqs = f32(ROPE_BASE) ** (-np.arange(0, RET_DK, 2, dtype=f32) / f32(RET_DK))
    ang = pos[:, None] * freqs[None, :]
    cos_h, sin_h = np.cos(ang).astype(f32), np.sin(ang).astype(f32)
    cos_t = jnp.asarray(np.concatenate([cos_h, cos_h] * 2, axis=1))
    sin_t = jnp.asarray(np.concatenate([-sin_h, sin_h] * 2, axis=1))
    valid_t = jnp.asarray(np.broadcast_to((pos >= 0).astype(f32)[:, None], (rows, LANES)))
    log_gamma = np.log(f32(1.0) - f32(2.0) ** (-5.0 - np.arange(RET_HEADS, dtype=f32))).astype(f32)
    lg_tab = jnp.asarray(np.broadcast_to(log_gamma.reshape(RET_HEADS // 2, 2, 1), (RET_HEADS // 2, 2, BLK)))

    gains = lambda t: t.reshape(t.shape[0], 1, t.shape[1])
    final_gain = final_norm.reshape(1, d)
    h = x.reshape(real_rows, d)
    for l in range(DEPTH):
        last = l == DEPTH - 1
        h = _ffn(h, l, gains(ffn1_norm), ffn1_w_gu, ffn1_w_down, final_gain, final_norm=False,
                 tail=meta_tile if l == 0 else None)
        rq, rk, rv, sq, sk, sv = _proj(h, l, gains(mix_norm), w_in, cos_t, sin_t, valid_t)
        o_r = _retention(lg_tab, rq, rk, rv, batch)
        o_s = _stick_breaking(sq, sk, sv, batch)
        h = _merge(h, l, gains(mix_norm), w_in, gains(ret_gn), o_r, o_s, w_ret_proj, w_sb_proj, w_out,
                   out_rows=real_rows if last else rows)
        h = _ffn(h, l, gains(ffn2_norm), ffn2_w_gu, ffn2_w_down, final_gain, final_norm=last)
    return h.reshape(batch, seq, d)
```

```python
import functools

import jax
import jax.numpy as jnp
import numpy as np
from jax import lax
from jax.experimental import pallas as pl
from jax.experimental.pallas import tpu as pltpu

D_MODEL = 1024
DEPTH = 2
N_META = 16
RET_HEADS = 8
RET_DK = 64
RET_DV = 128
SB_HEADS = 8
SB_DH = 64
D_FF = 2816
ROPE_BASE = 10000.0
EPS = 1e-6

LANES = 128
BLK = 256
PAD = BLK - N_META
FF_CHUNK = 256
ROW_TILE = 512
MASK_BIG = 1e30
LOG2E = 1.4426950408889634
HALF = BLK // 2
CAUSAL = "causal"
UNDERFLOW_LOG2 = -160.0
VMEM_LIMIT = 56 * 1024 * 1024

_COLS = np.cumsum([0, 512, 512, 1024, 1024, 512, 512, 512, 1024, 1024])
C_RQ, C_RK, C_RV, C_RG, C_SQ, C_SK, C_SV, C_GA, C_GB, C_END = (int(c) for c in _COLS)

F32 = jnp.float32
BF16 = jnp.bfloat16


def _dot(a, b):
    return jnp.dot(a, b, preferred_element_type=F32)


def _dot_nt(a, b):
    return lax.dot_general(a, b, (((1,), (1,)), ((), ())), preferred_element_type=F32)


def _rmsnorm(x, g):
    r = lax.rsqrt(jnp.mean(x * x, axis=-1, keepdims=True) + EPS)
    return (x * r) * g


def _sigmoid(x):
    return 1.0 / (1.0 + jnp.exp(-x))


def _params(*sem):
    return pltpu.CompilerParams(dimension_semantics=sem, vmem_limit_bytes=VMEM_LIMIT)


def _resident_spec(block, index_map):
    return pl.BlockSpec(block, index_map, pipeline_mode=pl.Buffered(1))


def _layer_spec(layer, shape, col_block=0):
    return _resident_spec((None,) + shape, lambda *_: (layer, 0, col_block))


def _ffn_kernel(x_ref, tail_ref, g_ref, wgu_ref, wd_ref, gf_ref, o_ref, gate_ref, *, final_norm, tail_step):
    x = x_ref[...]
    if tail_step is not None:
        x = jnp.where(pl.program_id(0) == tail_step, tail_ref[...], x)
    u = _rmsnorm(x, g_ref[...]).astype(BF16)
    for c in range(0, D_FF, FF_CHUNK):
        a = _dot(u, wgu_ref[:, c:c + FF_CHUNK].astype(BF16))
        b = _dot(u, wgu_ref[:, D_FF + c:D_FF + c + FF_CHUNK].astype(BF16))
        gate_ref[:, c:c + FF_CHUNK] = (0.5 * a * _sigmoid(a) * b).astype(BF16)
    y = x + _dot(gate_ref[...], wd_ref[...].astype(BF16))
    if final_norm:
        y = _rmsnorm(y, gf_ref[...])
    o_ref[...] = y


def _ffn(h, layer, g, w_gu, w_down, gf, *, final_norm, tail=None):
    steps = h.shape[0] // ROW_TILE
    tail_step = None
    if tail is None:
        tail = gf
    else:
        tail_step, steps = steps, steps + 1
    last_in = h.shape[0] // ROW_TILE - 1
    return pl.pallas_call(
        functools.partial(_ffn_kernel, final_norm=final_norm, tail_step=tail_step),
        grid=(steps,),
        in_specs=[
            pl.BlockSpec((ROW_TILE, D_MODEL), lambda i: (jnp.minimum(i, last_in), 0)),
            _resident_spec(tail.shape, lambda i: (0, 0)),
            _layer_spec(layer, (1, D_MODEL)),
            _layer_spec(layer, (D_MODEL, 2 * D_FF)),
            _layer_spec(layer, (D_FF, D_MODEL)),
            _resident_spec((1, D_MODEL), lambda i: (0, 0)),
        ],
        out_specs=pl.BlockSpec((ROW_TILE, D_MODEL), lambda i: (i, 0)),
        out_shape=jax.ShapeDtypeStruct((steps * ROW_TILE, D_MODEL), F32),
        scratch_shapes=[pltpu.VMEM((ROW_TILE, D_FF), BF16)],
        compiler_params=_params("parallel"),
        name="ffn_final" if final_norm else "ffn",
    )(h, tail, g, w_gu, w_down, gf)


def _proj_kernel(x_ref, g_ref, wr_ref, ws_ref, cos_ref, sin_ref, valid_ref,
                 rq_ref, rk_ref, rv_ref, sq_ref, sk_ref, sv_ref):
    u = _rmsnorm(x_ref[...], g_ref[...]).astype(BF16)
    cos = jnp.concatenate([cos_ref[...]] * (512 // LANES), axis=1)
    sin = jnp.concatenate([sin_ref[...]] * (512 // LANES), axis=1)
    valid = jnp.concatenate([valid_ref[...]] * (512 // LANES), axis=1)
    lane = lax.broadcasted_iota(jnp.int32, (ROW_TILE, 512), 1)
    first_half = (lane % RET_DK) < (RET_DK // 2)

    def rotary(t):
        partner = jnp.where(first_half, pltpu.roll(t, 512 - RET_DK // 2, 1),
                            pltpu.roll(t, RET_DK // 2, 1))
        return t * cos + partner * sin

    def proj(w_ref, lo, hi):
        return _dot(u, w_ref[:, lo:hi].astype(BF16))

    rq_ref[...] = rotary(proj(wr_ref, C_RQ, C_RK)).astype(BF16)
    rk_ref[...] = (rotary(proj(wr_ref, C_RK, C_RV)) * (valid * RET_DK ** -0.5)).astype(BF16)
    rv_ref[...] = (proj(wr_ref, C_RV, C_RG) * jnp.concatenate([valid, valid], axis=1)).astype(BF16)
    sq_ref[...] = (proj(ws_ref, 0, 512) * (-LOG2E * SB_DH ** -0.5)).astype(BF16)
    sk_ref[...] = proj(ws_ref, 512, 1024).astype(BF16)
    sv_ref[...] = proj(ws_ref, 1024, 1536).astype(BF16)


def _proj(h, layer, g, w_in, cos_t, sin_t, valid_t):
    rows = h.shape[0]

    def row_spec(width):
        return pl.BlockSpec((ROW_TILE, width), lambda i: (i, 0))

    tab_spec = row_spec(LANES)
    widths = (512, 512, 1024, 512, 512, 512)
    sb_cols = C_GA - C_SQ
    assert C_SQ % sb_cols == 0
    return pl.pallas_call(
        _proj_kernel,
        grid=(rows // ROW_TILE,),
        in_specs=[row_spec(D_MODEL), _layer_spec(layer, (1, D_MODEL)),
                  _layer_spec(layer, (D_MODEL, C_RG)),
                  _layer_spec(layer, (D_MODEL, sb_cols), col_block=C_SQ // sb_cols),
                  tab_spec, tab_spec, tab_spec],
        out_specs=[row_spec(w) for w in widths],
        out_shape=[jax.ShapeDtypeStruct((rows, w), BF16) for w in widths],
        compiler_params=_params("parallel"),
        name="mixer_proj",
    )(h, g, w_in, w_in, cos_t, sin_t, valid_t)


def _ret_kernel(lg_ref, q_ref, k_ref, v_ref, *rest):
    o_ref, state_ref, dec_ref, kdec_ref, qdec_ref, sdec_ref = rest[-6:]
    c = pl.program_id(1)
    pairs = RET_HEADS // 2

    @pl.when(c == 0)
    def _init():
        t = lax.broadcasted_iota(jnp.int32, (BLK, BLK), 0)
        s = lax.broadcasted_iota(jnp.int32, (BLK, BLK), 1)
        diff = (t - s).astype(F32)
        lane = lax.broadcasted_iota(jnp.int32, (BLK, LANES), 1)
        pos = lax.broadcasted_iota(jnp.int32, (BLK, LANES), 0).astype(F32)
        col = lax.broadcasted_iota(jnp.int32, (BLK, 2 * RET_DV), 1)
        posv = lax.broadcasted_iota(jnp.int32, (BLK, 2 * RET_DV), 0).astype(F32)
        srow = lax.broadcasted_iota(jnp.int32, (LANES, 2 * RET_DV), 0)
        scol = lax.broadcasted_iota(jnp.int32, (LANES, 2 * RET_DV), 1)
        own = (srow // RET_DK) == (scol // RET_DV)
        for p in range(pairs):
            lg = lg_ref[p]
            for i in range(2):
                dec_ref[2 * p + i] = jnp.where(
                    diff >= 0, jnp.exp(lg[i:i + 1, :] * jnp.maximum(diff, 0.0)), 0.0)
            lg_k = jnp.where(lane < RET_DK, lg[0:1, :LANES], lg[1:2, :LANES])
            kdec_ref[p] = jnp.exp(lg_k * (BLK - 1.0 - pos))
            lg_v = jnp.where(col < RET_DV, lg[0:1, :], lg[1:2, :])
            qdec_ref[p] = jnp.exp(lg_v * (posv + 1.0))
            lg_s = jnp.where(scol < RET_DV, lg[0:1, :], lg[1:2, :])
            sdec_ref[2 * p] = jnp.where(own, jnp.exp(lg_s * float(BLK)), 0.0)
            sdec_ref[2 * p + 1] = jnp.where(own, 1.0, 0.0)
        state_ref[...] = jnp.zeros_like(state_ref)

    lane = lax.broadcasted_iota(jnp.int32, (BLK, LANES), 1)
    qs, ks, crosses, scores, outs = {}, {}, {}, {}, {}

    def pair_start(p):
        qs[p] = q_ref[:, p * LANES:(p + 1) * LANES]
        ks[p] = k_ref[:, p * LANES:(p + 1) * LANES]
        crosses[p] = _dot(qs[p], state_ref[p].astype(BF16)) * qdec_ref[p]

    def head_scores(head):
        p, i = divmod(head, 2)
        in_head = (lane < RET_DK) if i == 0 else (lane >= RET_DK)
        qi = jnp.where(in_head, qs[p], jnp.zeros_like(qs[p]))
        scores[head] = (_dot_nt(qi, ks[p]) * dec_ref[head]).astype(BF16)

    def head_values(head):
        p, i = divmod(head, 2)
        o = _dot(scores.pop(head), v_ref[:, head * RET_DV:(head + 1) * RET_DV])
        outs[head] = o + crosses[p][:, i * RET_DV:(i + 1) * RET_DV]

    def head_norm(head):
        o = outs.pop(head)
        mu = jnp.mean(o, axis=-1, keepdims=True)
        d = o - mu
        var = jnp.mean(d * d, axis=-1, keepdims=True)
        o_ref[:, head * RET_DV:(head + 1) * RET_DV] = (d * lax.rsqrt(var + EPS)).astype(BF16)

    def pair_state(p):
        v = v_ref[:, 2 * p * RET_DV:2 * (p + 1) * RET_DV]
        kd_t = (ks[p].astype(F32) * kdec_ref[p]).T.astype(BF16)
        state_ref[p] = state_ref[p] * sdec_ref[2 * p] + _dot(kd_t, v) * sdec_ref[2 * p + 1]

    for t in range(RET_HEADS + 2):
        if t < RET_HEADS:
            if t % 2 == 0:
                pair_start(t // 2)
            head_scores(t)
        if 0 <= t - 1 < RET_HEADS:
            head_values(t - 1)
        if 0 <= t - 2 < RET_HEADS:
            head_norm(t - 2)
            if (t - 2) % 2 == 1:
                pair_state((t - 2) // 2)


def _seq_block(batch_index, step, blocks_per_seq, meta_block):
    return jnp.where(step == 0, meta_block, batch_index * blocks_per_seq + step - 1)


def _retention(lg_tab, rq, rk, rv, batch):
    rows = rq.shape[0]
    pairs = RET_HEADS // 2
    qk_w, v_w = RET_HEADS * RET_DK, RET_HEADS * RET_DV
    meta_block = rows // BLK - 2
    per_seq = meta_block // batch

    def call(grid, rows_in, rows_out, filled, name):
        return pl.pallas_call(
            _ret_kernel,
            grid=grid,
            in_specs=[
                pl.BlockSpec((pairs, 2, BLK), lambda b, c: (0, 0, 0)),
                pl.BlockSpec((BLK, qk_w), rows_in),
                pl.BlockSpec((BLK, qk_w), rows_in),
                pl.BlockSpec((BLK, v_w), rows_in),
            ] + [pl.BlockSpec(memory_space=pl.ANY)] * len(filled),
            out_specs=pl.BlockSpec((BLK, v_w), rows_out),
            out_shape=jax.ShapeDtypeStruct((rows, v_w), BF16),
            input_output_aliases={4: 0} if filled else {},
            scratch_shapes=[
                pltpu.VMEM((pairs, LANES, 2 * RET_DV), F32),
                pltpu.VMEM((RET_HEADS, BLK, BLK), F32),
                pltpu.VMEM((pairs, BLK, LANES), F32),
                pltpu.VMEM((pairs, BLK, 2 * RET_DV), F32),
                pltpu.VMEM((2 * pairs, LANES, 2 * RET_DV), F32),
            ],
            compiler_params=_params("parallel", "arbitrary"),
            name=name,
        )(lg_tab, rq, rk, rv, *filled)

    out = call((2, 1), lambda b, c: (meta_block, 0), lambda b, c: (meta_block + b, 0),
               (), "retention_meta")
    return call((batch, per_seq + 1),
                lambda b, c: (_seq_block(b, c, per_seq, meta_block), 0),
                lambda b, c: (b * per_seq + jnp.maximum(c - 1, 0), 0), (out,), "retention")


def _sb_kernel(q_ref, km_ref, vm_ref, kr_ref, vr_ref, *rest, meta_only):
    o_ref, k_ref, v_ref, acc_ref, carry_ref = rest[-5:]
    step_id = pl.program_id(1)

    @pl.when(step_id == 0)
    def _assemble_sequence():
        k_ref[0:BLK] = km_ref[...]
        v_ref[0:BLK] = vm_ref[...]
        if not meta_only:
            k_ref[BLK:] = kr_ref[...]
            v_ref[BLK:] = vr_ref[...]

    pairs = SB_HEADS // 2
    lane = lax.broadcasted_iota(jnp.int32, (BLK, LANES), 1)
    q_heads = []
    for p in range(pairs):
        q = q_ref[:, p * LANES:(p + 1) * LANES]
        zero = jnp.zeros_like(q)
        q_heads += [jnp.where(lane < SB_DH, q, zero), jnp.where(lane >= SB_DH, q, zero)]
    j = lax.broadcasted_iota(jnp.int32, (BLK, BLK), 0)
    c = lax.broadcasted_iota(jnp.int32, (BLK, BLK), 1)
    suffix = jnp.where(j >= c, 1.0, 0.0).astype(BF16)
    key_col = lax.broadcasted_iota(jnp.int32, (1, BLK), 1)

    def step(blocks, first):
        chains = [(kb, bias, h) for kb, bias in blocks for h in range(SB_HEADS)]
        ys, logs, totals, sums = {}, {}, {}, {}
        carries = [None if first else carry_ref[h] for h in range(SB_HEADS)]
        accs = [None] * SB_HEADS

        def soft_log(y):
            return jnp.minimum(y, 0.0) - jnp.log2(1.0 + jnp.exp2(-jnp.abs(y)))

        def rows_of(kb, offset, size):
            start = kb * BLK + offset
            return pl.ds(start if isinstance(start, int) else pl.multiple_of(start, size), size)

        def front(n):
            kb, bias, h = chains[n]
            cols = slice((h // 2) * LANES, (h // 2 + 1) * LANES)
            if bias is CAUSAL:
                assert carries[h] is None
                y_e = _dot_nt(q_heads[h], k_ref[rows_of(kb, 0, HALF), cols]) + causal[:, :HALF]
                y_l = _dot_nt(q_heads[h][HALF:], k_ref[rows_of(kb, HALF, HALF), cols]) + causal[HALF:, HALF:]
                log_e, log_l = soft_log(y_e), soft_log(y_l)
                late = jnp.concatenate([jnp.zeros((HALF, HALF), BF16), log_l.astype(BF16)], axis=0)
                ys[n] = (y_e, y_l)
                logs[n] = jnp.concatenate([log_e.astype(BF16), late], axis=1)
                totals[n] = jnp.sum(log_e, axis=1, keepdims=True) + jnp.concatenate(
                    [jnp.zeros((HALF, 1), F32), jnp.sum(log_l, axis=1, keepdims=True)], axis=0)
            else:
                y = _dot_nt(q_heads[h], k_ref[rows_of(kb, 0, BLK), cols]) + bias
                log_1mb = soft_log(y)
                ys[n], logs[n] = y, log_1mb.astype(BF16)
                totals[n] = jnp.sum(log_1mb, axis=1, keepdims=True)

        def back(n):
            kb, bias, h = chains[n]
            cols = slice((h // 2) * LANES, (h // 2 + 1) * LANES)
            if bias is CAUSAL:
                incl = sums.pop(n)
                y_e, y_l = ys.pop(n)
                a_e = jnp.exp2(incl[:, :HALF] - y_e).astype(BF16)
                a_l = jnp.exp2(incl[HALF:, HALF:] - y_l).astype(BF16)
                pv_l = _dot(a_l, v_ref[rows_of(kb, HALF, HALF), cols])
                pv = _dot(a_e, v_ref[rows_of(kb, 0, HALF), cols]) + jnp.concatenate(
                    [jnp.zeros((HALF, LANES), F32), pv_l], axis=0)
            else:
                incl = sums.pop(n) if carries[h] is None else sums.pop(n) + carries[h]
                a = jnp.exp2(incl - ys.pop(n)).astype(BF16)
                pv = _dot(a, v_ref[rows_of(kb, 0, BLK), cols])
            carries[h] = totals[n] if carries[h] is None else carries[h] + totals[n]
            accs[h] = pv if accs[h] is None else accs[h] + pv

        for t in range(len(chains) + 2):
            if t < len(chains):
                front(t)
            if 0 <= t - 1 < len(chains):
                sums[t - 1] = _dot(logs.pop(t - 1), suffix)
            if 0 <= t - 2 < len(chains):
                back(t - 2)
        for h in range(SB_HEADS):
            carry_ref[h] = carries[h]
            if first:
                acc_ref[h] = accs[h]
            else:
                acc_ref[h] += accs[h]

    def valid_bias(kb):
        return jnp.where(kb * BLK + key_col < PAD, MASK_BIG, 0.0).astype(F32)

    r = lax.broadcasted_iota(jnp.int32, (BLK, BLK), 0)
    s = lax.broadcasted_iota(jnp.int32, (BLK, BLK), 1)
    causal = jnp.where(s < r, 0.0, MASK_BIG).astype(F32)

    if meta_only:
        step([(0, causal + valid_bias(0))], True)
    else:
        qb = step_id + 1
        step([(qb, CAUSAL), (qb - 1, valid_bias(qb - 1))], True)

        def decayed():
            top = carry_ref[0]
            for h in range(1, SB_HEADS):
                top = jnp.maximum(top, carry_ref[h])
            return jnp.max(top) < UNDERFLOW_LOG2

        def live(state):
            kb, done = state
            return jnp.logical_and(kb >= 0, jnp.logical_not(done))

        def body(state):
            kb, _ = state
            step([(kb, valid_bias(kb))], False)
            return kb - 1, decayed()

        lax.while_loop(live, body, (qb - 2, decayed()))
    for p in range(pairs):
        o_ref[:, p * LANES:(p + 1) * LANES] = jnp.where(
            lane < SB_DH, acc_ref[2 * p], acc_ref[2 * p + 1]).astype(BF16)


def _stick_breaking(sq, sk, sv, batch):
    rows, width = sq.shape
    assert PAD <= BLK and width == SB_HEADS * SB_DH
    meta_block = rows // BLK - 2
    per_seq = meta_block // batch
    seq = per_seq * BLK
    meta_spec = pl.BlockSpec((BLK, width), lambda b, i: (meta_block, 0))

    def call(grid, q_rows, out_rows, seq_of, filled, name):
        real_spec = pl.BlockSpec((seq if filled else BLK, width), lambda b, i: (seq_of(b), 0))
        return pl.pallas_call(
            functools.partial(_sb_kernel, meta_only=not filled),
            grid=grid,
            in_specs=[pl.BlockSpec((BLK, width), q_rows), meta_spec, meta_spec, real_spec, real_spec]
            + [pl.BlockSpec(memory_space=pl.ANY)] * len(filled),
            out_specs=pl.BlockSpec((BLK, width), out_rows),
            out_shape=jax.ShapeDtypeStruct((rows, width), BF16),
            input_output_aliases={5: 0} if filled else {},
            scratch_shapes=[pltpu.VMEM((seq + BLK, width), BF16),
                            pltpu.VMEM((seq + BLK, width), BF16),
                            pltpu.VMEM((SB_HEADS, BLK, LANES), F32),
                            pltpu.VMEM((SB_HEADS, BLK, 1), F32)],
            compiler_params=_params("parallel", "arbitrary"),
            name=name,
        )(sq, sk, sv, sk, sv, *filled)

    out = call((2, 1), lambda b, i: (meta_block, 0), lambda b, i: (meta_block + b, 0),
               lambda b: 0, (), "stick_breaking_meta")
    return call((batch, per_seq), lambda b, i: (b * per_seq + i, 0), lambda b, i: (b * per_seq + i, 0),
                lambda b: b, (out,), "stick_breaking")


def _merge_kernel(x_ref, g_ref, wrg_ref, wga0_ref, wga1_ref, wgb0_ref, wgb1_ref, gn_ref,
                  or_ref, os_ref, wr_ref, ws_ref, wo_ref, o_ref):
    x = x_ref[...]
    u = _rmsnorm(x, g_ref[...]).astype(BF16)

    def gate(*w_refs):
        return jnp.concatenate([_dot(u, w[...].astype(BF16)) for w in w_refs], axis=1)

    rg = gate(wrg_ref)
    y_sb = _dot(os_ref[...], ws_ref[...].astype(BF16))
    gb = _sigmoid(gate(wgb0_ref, wgb1_ref)) * y_sb
    ga = _sigmoid(gate(wga0_ref, wga1_ref))
    gated = (rg * _sigmoid(rg)) * (or_ref[...].astype(F32) * gn_ref[...])
    y_ret = _dot(gated.astype(BF16), wr_ref[...].astype(BF16))
    y = ga * y_ret + gb
    o_ref[...] = x + _dot(y.astype(BF16), wo_ref[...].astype(BF16))


def _merge(h, layer, g, w_in, gn, o_r, o_s, w_ret, w_sb, w_out, *, out_rows):
    def in_spec(width):
        return pl.BlockSpec((ROW_TILE, width), lambda i: (i, 0))

    half = D_MODEL // 2
    assert C_RG % D_MODEL == 0 and C_GA % half == 0
    ret_w, sb_w = RET_HEADS * RET_DV, SB_HEADS * SB_DH
    return pl.pallas_call(
        _merge_kernel,
        grid=(out_rows // ROW_TILE,),
        in_specs=[
            in_spec(D_MODEL), _layer_spec(layer, (1, D_MODEL)),
            _layer_spec(layer, (D_MODEL, D_MODEL), col_block=C_RG // D_MODEL),
            _layer_spec(layer, (D_MODEL, half), col_block=C_GA // half),
            _layer_spec(layer, (D_MODEL, half), col_block=C_GA // half + 1),
            _layer_spec(layer, (D_MODEL, half), col_block=C_GB // half),
            _layer_spec(layer, (D_MODEL, half), col_block=C_GB // half + 1),
            _layer_spec(layer, (1, ret_w)), in_spec(ret_w), in_spec(sb_w),
            _layer_spec(layer, (ret_w, D_MODEL)), _layer_spec(layer, (sb_w, D_MODEL)),
            _layer_spec(layer, (D_MODEL, D_MODEL)),
        ],
        out_specs=pl.BlockSpec((ROW_TILE, D_MODEL), lambda i: (i, 0)),
        out_shape=jax.ShapeDtypeStruct((out_rows, D_MODEL), F32),
        compiler_params=_params("parallel"),
        name="mixer_merge",
    )(h, g, w_in, w_in, w_in, w_in, w_in, gn, o_r, o_s, w_ret, w_sb, w_out)


def kernel(x, meta, ffn1_norm, ffn1_w_gu, ffn1_w_down, mix_norm, w_in, ret_gn, w_ret_proj,
           w_sb_proj, w_out, ffn2_norm, ffn2_w_gu, ffn2_w_down, final_norm):
    batch, seq, d = x.shape
    real_rows = batch * seq
    rows = real_rows + ROW_TILE
    assert d == D_MODEL and seq % ROW_TILE == 0 and ROW_TILE == 2 * BLK and batch >= 2

    meta_block = jnp.concatenate([jnp.zeros((PAD, d), x.dtype), meta.astype(x.dtype)], axis=0)
    meta_tile = jnp.concatenate([meta_block, meta_block], axis=0)

    f32 = np.float32
    pos_meta = (np.arange(BLK) - PAD).astype(f32)
    pos_real = (np.arange(seq) + N_META).astype(f32)
    pos = np.concatenate([np.tile(pos_real, batch), pos_meta, pos_meta])
    freqs = f32(ROPE_BASE) ** (-np.arange(0, RET_DK, 2, dtype=f32) / f32(RET_DK))
    ang = pos[:, None] * freqs[None, :]
    cos_h, sin_h = np.cos(ang).astype(f32), np.sin(ang).astype(f32)
    cos_t = jnp.asarray(np.concatenate([cos_h, cos_h] * 2, axis=1))
    sin_t = jnp.asarray(np.concatenate([-sin_h, sin_h] * 2, axis=1))
    valid_t = jnp.asarray(np.broadcast_to((pos >= 0).astype(f32)[:, None], (rows, LANES)))
    log_gamma = np.log(f32(1.0) - f32(2.0) ** (-5.0 - np.arange(RET_HEADS, dtype=f32))).astype(f32)
    lg_tab = jnp.asarray(np.broadcast_to(log_gamma.reshape(RET_HEADS // 2, 2, 1), (RET_HEADS // 2, 2, BLK)))

    gains = lambda t: t.reshape(t.shape[0], 1, t.shape[1])
    final_gain = final_norm.reshape(1, d)
    h = x.reshape(real_rows, d)
    for l in range(DEPTH):
        last = l == DEPTH - 1
        h = _ffn(h, l, gains(ffn1_norm), ffn1_w_gu, ffn1_w_down, final_gain, final_norm=False,
                 tail=meta_tile if l == 0 else None)
        rq, rk, rv, sq, sk, sv = _proj(h, l, gains(mix_norm), w_in, cos_t, sin_t, valid_t)
        o_r = _retention(lg_tab, rq, rk, rv, batch)
        o_s = _stick_breaking(sq, sk, sv, batch)
        h = _merge(h, l, gains(mix_norm), w_in, gains(ret_gn), o_r, o_s, w_ret_proj, w_sb_proj, w_out,
                   out_rows=real_rows if last else rows)
        h = _ffn(h, l, gains(ffn2_norm), ffn2_w_gu, ffn2_w_down, final_gain, final_norm=last)
    return h.reshape(batch, seq, d)
```

```python
import functools

import jax
import jax.numpy as jnp
import numpy as np
from jax import lax
from jax.experimental import pallas as pl
from jax.experimental.pallas import tpu as pltpu

D_MODEL = 1024
DEPTH = 2
N_META = 16
RET_HEADS = 8
RET_DK = 64
RET_DV = 128
SB_HEADS = 8
SB_DH = 64
D_FF = 2816
ROPE_BASE = 10000.0
EPS = 1e-6

LANES = 128
BLK = 256
PAD = BLK - N_META
FF_CHUNK = 256
ROW_TILE = 512
MID_TILE = 656
MASK_BIG = 1e30
LOG2E = 1.4426950408889634
UNDERFLOW_LOG2 = -160.0
VMEM_LIMIT = 56 * 1024 * 1024

_COLS = np.cumsum([0, 512, 512, 1024, 1024, 512, 512, 512, 1024, 1024])
C_RQ, C_RK, C_RV, C_RG, C_SQ, C_SK, C_SV, C_GA, C_GB, C_END = (int(c) for c in _COLS)

F32 = jnp.float32
BF16 = jnp.bfloat16


def _dot(a, b):
    return jnp.dot(a, b, preferred_element_type=F32)


def _dot_nt(a, b):
    return lax.dot_general(a, b, (((1,), (1,)), ((), ())), preferred_element_type=F32)


def _rmsnorm(x, g):
    r = lax.rsqrt(jnp.mean(x * x, axis=-1, keepdims=True) + EPS)
    return (x * r) * g


def _sigmoid(x):
    return 1.0 / (1.0 + jnp.exp(-x))


def _params(*sem):
    return pltpu.CompilerParams(dimension_semantics=sem, vmem_limit_bytes=VMEM_LIMIT)


def _resident_spec(block, index_map):
    return pl.BlockSpec(block, index_map, pipeline_mode=pl.Buffered(1))


def _layer_spec(layer, shape, col_block=0):
    return _resident_spec((None,) + shape, lambda *_: (layer, 0, col_block))


def _ffn_kernel(x_ref, g_ref, wgu_ref, wd_ref, gf_ref, *rest, final_norm):
    o_ref, gate_ref = rest[-2:]
    x = x_ref[...]
    u = _rmsnorm(x, g_ref[...]).astype(BF16)
    for c in range(0, D_FF, FF_CHUNK):
        a = _dot(u, wgu_ref[:, c:c + FF_CHUNK].astype(BF16))
        b = _dot(u, wgu_ref[:, D_FF + c:D_FF + c + FF_CHUNK].astype(BF16))
        gate_ref[:, c:c + FF_CHUNK] = (0.5 * a * _sigmoid(a) * b).astype(BF16)
    y = x + _dot(gate_ref[...], wd_ref[...].astype(BF16))
    if final_norm:
        y = _rmsnorm(y, gf_ref[...])
    o_ref[...] = y


def _ffn(h, layer, g, w_gu, w_down, gf, *, final_norm, tile, out_rows=None, out_offset=0, filled=()):
    out_rows = h.shape[0] if out_rows is None else out_rows
    return pl.pallas_call(
        functools.partial(_ffn_kernel, final_norm=final_norm),
        grid=(h.shape[0] // tile,),
        in_specs=[
            pl.BlockSpec((tile, D_MODEL), lambda i: (i, 0)),
            _layer_spec(layer, (1, D_MODEL)),
            _layer_spec(layer, (D_MODEL, 2 * D_FF)),
            _layer_spec(layer, (D_FF, D_MODEL)),
            _resident_spec((1, D_MODEL), lambda i: (0, 0)),
        ] + [pl.BlockSpec(memory_space=pl.ANY)] * len(filled),
        out_specs=pl.BlockSpec((tile, D_MODEL), lambda i: (i + out_offset, 0)),
        out_shape=jax.ShapeDtypeStruct((out_rows, D_MODEL), F32),
        input_output_aliases={5: 0} if filled else {},
        scratch_shapes=[pltpu.VMEM((tile, D_FF), BF16)],
        compiler_params=_params("parallel"),
        name="ffn_final" if final_norm else "ffn",
    )(h, g, w_gu, w_down, gf, *filled)


def _proj_kernel(x_ref, g_ref, wr_ref, ws_ref, cos_ref, sin_ref,
                 rq_ref, rk_ref, rv_ref, sq_ref, sk_ref, sv_ref):
    u = _rmsnorm(x_ref[...], g_ref[...]).astype(BF16)
    qk_w = RET_HEADS * RET_DK
    cos = jnp.concatenate([cos_ref[...]] * (qk_w // LANES), axis=1)
    sin = jnp.concatenate([sin_ref[...]] * (qk_w // LANES), axis=1)
    lane = lax.broadcasted_iota(jnp.int32, (x_ref.shape[0], qk_w), 1)
    first_half = (lane % RET_DK) < (RET_DK // 2)

    def rotary(t):
        partner = jnp.where(first_half, pltpu.roll(t, qk_w - RET_DK // 2, 1),
                            pltpu.roll(t, RET_DK // 2, 1))
        return t * cos + partner * sin

    def proj(w_ref, lo, hi):
        return _dot(u, w_ref[:, lo:hi].astype(BF16))

    rq_ref[...] = rotary(proj(wr_ref, C_RQ, C_RK)).astype(BF16)
    rk_ref[...] = (rotary(proj(wr_ref, C_RK, C_RV)) * RET_DK ** -0.5).astype(BF16)
    rv_ref[...] = proj(wr_ref, C_RV, C_RG).astype(BF16)
    sq_ref[...] = (proj(ws_ref, 0, C_SK - C_SQ) * (-LOG2E * SB_DH ** -0.5)).astype(BF16)
    sk_ref[...] = proj(ws_ref, C_SK - C_SQ, C_SV - C_SQ).astype(BF16)
    sv_ref[...] = proj(ws_ref, C_SV - C_SQ, C_GA - C_SQ).astype(BF16)


def _proj(h, layer, g, w_in, cos_t, sin_t, *, tile):
    rows = h.shape[0]

    def row_spec(width):
        return pl.BlockSpec((tile, width), lambda i: (i, 0))

    tab_spec = row_spec(LANES)
    widths = (C_RK - C_RQ, C_RV - C_RK, C_RG - C_RV, C_SK - C_SQ, C_SV - C_SK, C_GA - C_SV)
    sb_cols = C_GA - C_SQ
    assert C_SQ % sb_cols == 0
    return pl.pallas_call(
        _proj_kernel,
        grid=(rows // tile,),
        in_specs=[row_spec(D_MODEL), _layer_spec(layer, (1, D_MODEL)),
                  _layer_spec(layer, (D_MODEL, C_RG)),
                  _layer_spec(layer, (D_MODEL, sb_cols), col_block=C_SQ // sb_cols),
                  tab_spec, tab_spec],
        out_specs=[row_spec(w) for w in widths],
        out_shape=[jax.ShapeDtypeStruct((rows, w), BF16) for w in widths],
        compiler_params=_params("parallel"),
        name="mixer_proj",
    )(h, g, w_in, w_in, cos_t, sin_t)


def _meta_chunk(m_ref):
    m = m_ref[...]
    return jnp.concatenate([jnp.zeros((PAD, m.shape[1]), m.dtype), m], axis=0)


def _ret_kernel(lg_ref, *refs, meta_only):
    o_ref, state_ref, dec_ref, kdec_ref, qdec_ref, sdec_ref = refs[-6:]
    c = pl.program_id(1)
    pairs = RET_HEADS // 2
    if meta_only:
        q_all, k_all, v_all = (_meta_chunk(r) for r in refs[:3])
    else:
        q_all, k_all, v_all = (jnp.where(c == 0, _meta_chunk(m), r[...])
                               for r, m in zip(refs[:3], refs[3:6]))

    @pl.when(c == 0)
    def _init():
        t = lax.broadcasted_iota(jnp.int32, (BLK, BLK), 0)
        s = lax.broadcasted_iota(jnp.int32, (BLK, BLK), 1)
        diff = (t - s).astype(F32)
        lane = lax.broadcasted_iota(jnp.int32, (BLK, LANES), 1)
        pos = lax.broadcasted_iota(jnp.int32, (BLK, LANES), 0).astype(F32)
        col = lax.broadcasted_iota(jnp.int32, (BLK, 2 * RET_DV), 1)
        posv = lax.broadcasted_iota(jnp.int32, (BLK, 2 * RET_DV), 0).astype(F32)
        srow = lax.broadcasted_iota(jnp.int32, (LANES, 2 * RET_DV), 0)
        scol = lax.broadcasted_iota(jnp.int32, (LANES, 2 * RET_DV), 1)
        own = (srow // RET_DK) == (scol // RET_DV)
        for p in range(pairs):
            lg = lg_ref[p]
            for i in range(2):
                dec_ref[2 * p + i] = jnp.where(
                    diff >= 0, jnp.exp(lg[i:i + 1, :] * jnp.maximum(diff, 0.0)), 0.0)
            lg_k = jnp.where(lane < RET_DK, lg[0:1, :LANES], lg[1:2, :LANES])
            kdec_ref[p] = jnp.exp(lg_k * (BLK - 1.0 - pos))
            lg_v = jnp.where(col < RET_DV, lg[0:1, :], lg[1:2, :])
            qdec_ref[p] = jnp.exp(lg_v * (posv + 1.0))
            lg_s = jnp.where(scol < RET_DV, lg[0:1, :], lg[1:2, :])
            sdec_ref[2 * p] = jnp.where(own, jnp.exp(lg_s * float(BLK)), 0.0)
            sdec_ref[2 * p + 1] = jnp.where(own, 1.0, 0.0)
        state_ref[...] = jnp.zeros_like(state_ref)

    lane = lax.broadcasted_iota(jnp.int32, (BLK, LANES), 1)
    qs, ks, crosses, scores, outs = {}, {}, {}, {}, {}

    def pair_start(p):
        qs[p] = q_all[:, p * LANES:(p + 1) * LANES]
        ks[p] = k_all[:, p * LANES:(p + 1) * LANES]
        crosses[p] = _dot(qs[p], state_ref[p].astype(BF16)) * qdec_ref[p]

    def head_scores(head):
        p, i = divmod(head, 2)
        in_head = (lane < RET_DK) if i == 0 else (lane >= RET_DK)
        qi = jnp.where(in_head, qs[p], jnp.zeros_like(qs[p]))
        scores[head] = (_dot_nt(qi, ks[p]) * dec_ref[head]).astype(BF16)

    def head_values(head):
        p, i = divmod(head, 2)
        o = _dot(scores.pop(head), v_all[:, head * RET_DV:(head + 1) * RET_DV])
        outs[head] = o + crosses[p][:, i * RET_DV:(i + 1) * RET_DV]

    def head_norm(head):
        o = outs.pop(head)
        mu = jnp.mean(o, axis=-1, keepdims=True)
        d = o - mu
        var = jnp.mean(d * d, axis=-1, keepdims=True)
        normed = (d * lax.rsqrt(var + EPS)).astype(BF16)
        o_ref[:, head * RET_DV:(head + 1) * RET_DV] = normed[PAD:] if meta_only else normed

    def pair_state(p):
        v = v_all[:, 2 * p * RET_DV:2 * (p + 1) * RET_DV]
        kd_t = (ks[p].astype(F32) * kdec_ref[p]).T.astype(BF16)
        state_ref[p] = state_ref[p] * sdec_ref[2 * p] + _dot(kd_t, v) * sdec_ref[2 * p + 1]

    for t in range(RET_HEADS + 2):
        if t < RET_HEADS:
            if t % 2 == 0:
                pair_start(t // 2)
            head_scores(t)
        if 0 <= t - 1 < RET_HEADS:
            head_values(t - 1)
        if 0 <= t - 2 < RET_HEADS:
            head_norm(t - 2)
            if (t - 2) % 2 == 1:
                pair_state((t - 2) // 2)


def _retention(lg_tab, rq, rk, rv, batch):
    rows = rq.shape[0]
    pairs = RET_HEADS // 2
    qk_w, v_w = RET_HEADS * RET_DK, RET_HEADS * RET_DV
    real_rows = rows - N_META
    per_seq = real_rows // BLK // batch
    meta_rows = lambda *_: (real_rows // N_META, 0)
    meta_specs = [pl.BlockSpec((N_META, w), meta_rows) for w in (qk_w, qk_w, v_w)]
    lg_spec = pl.BlockSpec((pairs, 2, BLK), lambda b, c: (0, 0, 0))
    scratch = [
        pltpu.VMEM((pairs, LANES, 2 * RET_DV), F32),
        pltpu.VMEM((RET_HEADS, BLK, BLK), F32),
        pltpu.VMEM((pairs, BLK, LANES), F32),
        pltpu.VMEM((pairs, BLK, 2 * RET_DV), F32),
        pltpu.VMEM((2 * pairs, LANES, 2 * RET_DV), F32),
    ]
    out_shape = jax.ShapeDtypeStruct((rows, v_w), BF16)
    out = pl.pallas_call(
        functools.partial(_ret_kernel, meta_only=True),
        grid=(1, 1),
        in_specs=[lg_spec] + meta_specs,
        out_specs=pl.BlockSpec((N_META, v_w), meta_rows),
        out_shape=out_shape,
        scratch_shapes=scratch,
        compiler_params=_params("arbitrary", "arbitrary"),
        name="retention_meta",
    )(lg_tab, rq, rk, rv)

    def real_rows_of(b, c):
        return b * per_seq + jnp.maximum(c - 1, 0), 0

    return pl.pallas_call(
        functools.partial(_ret_kernel, meta_only=False),
        grid=(batch, per_seq + 1),
        in_specs=[lg_spec] + [pl.BlockSpec((BLK, w), real_rows_of) for w in (qk_w, qk_w, v_w)]
        + meta_specs + [pl.BlockSpec(memory_space=pl.ANY)],
        out_specs=pl.BlockSpec((BLK, v_w), real_rows_of),
        out_shape=out_shape,
        input_output_aliases={7: 0},
        scratch_shapes=scratch,
        compiler_params=_params("parallel", "arbitrary"),
        name="retention",
    )(lg_tab, rq, rk, rv, rq, rk, rv, out)


def _sb_kernel(q_ref, km_ref, vm_ref, *rest, meta_only):
    o_ref, k_ref, v_ref, acc_ref, carry_ref = rest[-5:]
    step_id = pl.program_id(1)

    @pl.when(step_id == 0)
    def _assemble_sequence():
        k_ref[0:BLK] = _meta_chunk(km_ref)
        v_ref[0:BLK] = _meta_chunk(vm_ref)
        if not meta_only:
            k_ref[BLK:] = rest[0][...]
            v_ref[BLK:] = rest[1][...]

    pairs = SB_HEADS // 2
    lane = lax.broadcasted_iota(jnp.int32, (BLK, LANES), 1)
    q_all = _meta_chunk(q_ref) if meta_only else q_ref[...]
    q_heads = []
    for p in range(pairs):
        q = q_all[:, p * LANES:(p + 1) * LANES]
        zero = jnp.zeros_like(q)
        q_heads += [jnp.where(lane < SB_DH, q, zero), jnp.where(lane >= SB_DH, q, zero)]
    j = lax.broadcasted_iota(jnp.int32, (BLK, BLK), 0)
    c = lax.broadcasted_iota(jnp.int32, (BLK, BLK), 1)
    suffix = jnp.where(j >= c, 1.0, 0.0).astype(BF16)
    key_col = lax.broadcasted_iota(jnp.int32, (1, BLK), 1)

    def step(blocks, first):
        chains = [(kb, bias, h) for kb, bias in blocks for h in range(SB_HEADS)]
        ys, logs, totals, sums = {}, {}, {}, {}
        carries = [None if first else carry_ref[h] for h in range(SB_HEADS)]
        accs = [None] * SB_HEADS

        def soft_log(y):
            return jnp.minimum(y, 0.0) - jnp.log2(1.0 + jnp.exp2(-jnp.abs(y)))

        def rows_of(kb, offset, size):
            start = kb * BLK + offset
            return pl.ds(start if isinstance(start, int) else pl.multiple_of(start, size), size)

        def front(n):
            kb, bias, h = chains[n]
            cols = slice((h // 2) * LANES, (h // 2 + 1) * LANES)
            y = _dot_nt(q_heads[h], k_ref[rows_of(kb, 0, BLK), cols]) + bias
            log_1mb = soft_log(y)
            ys[n], logs[n] = y, log_1mb.astype(BF16)
            totals[n] = jnp.sum(log_1mb, axis=1, keepdims=True)

        def back(n):
            kb, _, h = chains[n]
            cols = slice((h // 2) * LANES, (h // 2 + 1) * LANES)
            incl = sums.pop(n) if carries[h] is None else sums.pop(n) + carries[h]
            a = jnp.exp2(incl - ys.pop(n)).astype(BF16)
            pv = _dot(a, v_ref[rows_of(kb, 0, BLK), cols])
            carries[h] = totals[n] if carries[h] is None else carries[h] + totals[n]
            accs[h] = pv if accs[h] is None else accs[h] + pv

        for t in range(len(chains) + 2):
            if t < len(chains):
                front(t)
            if 0 <= t - 1 < len(chains):
                sums[t - 1] = _dot(logs.pop(t - 1), suffix)
            if 0 <= t - 2 < len(chains):
                back(t - 2)
        for h in range(SB_HEADS):
            carry_ref[h] = carries[h]
            if first:
                acc_ref[h] = accs[h]
            else:
                acc_ref[h] += accs[h]

    def valid_bias(kb):
        return jnp.where(kb * BLK + key_col < PAD, MASK_BIG, 0.0).astype(F32)

    r = lax.broadcasted_iota(jnp.int32, (BLK, BLK), 0)
    s = lax.broadcasted_iota(jnp.int32, (BLK, BLK), 1)
    causal = jnp.where(s < r, 0.0, MASK_BIG).astype(F32)

    if meta_only:
        step([(0, causal + valid_bias(0))], True)
    else:
        qb = step_id + 1
        step([(qb, causal), (qb - 1, valid_bias(qb - 1))], True)

        def decayed():
            top = carry_ref[0]
            for h in range(1, SB_HEADS):
                top = jnp.maximum(top, carry_ref[h])
            return jnp.max(top) < UNDERFLOW_LOG2

        def live(state):
            kb, done = state
            return jnp.logical_and(kb >= 0, jnp.logical_not(done))

        def body(state):
            kb, _ = state
            step([(kb, valid_bias(kb))], False)
            return kb - 1, decayed()

        lax.while_loop(live, body, (qb - 2, decayed()))
    for p in range(pairs):
        out = jnp.where(lane < SB_DH, acc_ref[2 * p], acc_ref[2 * p + 1]).astype(BF16)
        o_ref[:, p * LANES:(p + 1) * LANES] = out[PAD:] if meta_only else out


def _stick_breaking(sq, sk, sv, batch):
    rows, width = sq.shape
    assert PAD <= BLK and width == SB_HEADS * SB_DH
    real_rows = rows - N_META
    per_seq = real_rows // BLK // batch
    seq = per_seq * BLK
    meta_spec = pl.BlockSpec((N_META, width), lambda b, i: (real_rows // N_META, 0))
    out_shape = jax.ShapeDtypeStruct((rows, width), BF16)

    def scratch(key_rows):
        return [pltpu.VMEM((key_rows, width), BF16), pltpu.VMEM((key_rows, width), BF16),
                pltpu.VMEM((SB_HEADS, BLK, LANES), F32), pltpu.VMEM((SB_HEADS, BLK, 1), F32)]

    out = pl.pallas_call(
        functools.partial(_sb_kernel, meta_only=True),
        grid=(1, 1),
        in_specs=[meta_spec] * 3,
        out_specs=meta_spec,
        out_shape=out_shape,
        scratch_shapes=scratch(BLK),
        compiler_params=_params("arbitrary", "arbitrary"),
        name="stick_breaking_meta",
    )(sq, sk, sv)
    q_spec = pl.BlockSpec((BLK, width), lambda b, i: (b * per_seq + i, 0))
    real_spec = pl.BlockSpec((seq, width), lambda b, i: (b, 0))
    return pl.pallas_call(
        functools.partial(_sb_kernel, meta_only=False),
        grid=(batch, per_seq),
        in_specs=[q_spec, meta_spec, meta_spec, real_spec, real_spec, pl.BlockSpec(memory_space=pl.ANY)],
        out_specs=q_spec,
        out_shape=out_shape,
        input_output_aliases={5: 0},
        scratch_shapes=scratch(seq + BLK),
        compiler_params=_params("parallel", "arbitrary"),
        name="stick_breaking",
    )(sq, sk, sv, sk, sv, out)


def _merge_kernel(x_ref, g_ref, wrg_ref, wga0_ref, wga1_ref, wgb0_ref, wgb1_ref, gn_ref,
                  or_ref, os_ref, wr_ref, ws_ref, wo_ref, o_ref):
    x = x_ref[...]
    u = _rmsnorm(x, g_ref[...]).astype(BF16)

    def gate(*w_refs):
        return jnp.concatenate([_dot(u, w[...].astype(BF16)) for w in w_refs], axis=1)

    rg = gate(wrg_ref)
    y_sb = _dot(os_ref[...], ws_ref[...].astype(BF16))
    gb = _sigmoid(gate(wgb0_ref, wgb1_ref)) * y_sb
    ga = _sigmoid(gate(wga0_ref, wga1_ref))
    gated = (rg * _sigmoid(rg)) * (or_ref[...].astype(F32) * gn_ref[...])
    y_ret = _dot(gated.astype(BF16), wr_ref[...].astype(BF16))
    y = ga * y_ret + gb
    o_ref[...] = x + _dot(y.astype(BF16), wo_ref[...].astype(BF16))


def _merge(h, layer, g, w_in, gn, o_r, o_s, w_ret, w_sb, w_out, *, tile, out_rows):
    def in_spec(width):
        return pl.BlockSpec((tile, width), lambda i: (i, 0))

    half = D_MODEL // 2
    assert C_RG % D_MODEL == 0 and C_GA % half == 0
    ret_w, sb_w = RET_HEADS * RET_DV, SB_HEADS * SB_DH
    return pl.pallas_call(
        _merge_kernel,
        grid=(out_rows // tile,),
        in_specs=[
            in_spec(D_MODEL), _layer_spec(layer, (1, D_MODEL)),
            _layer_spec(layer, (D_MODEL, D_MODEL), col_block=C_RG // D_MODEL),
            _layer_spec(layer, (D_MODEL, half), col_block=C_GA // half),
            _layer_spec(layer, (D_MODEL, half), col_block=C_GA // half + 1),
            _layer_spec(layer, (D_MODEL, half), col_block=C_GB // half),
            _layer_spec(layer, (D_MODEL, half), col_block=C_GB // half + 1),
            _layer_spec(layer, (1, ret_w)), in_spec(ret_w), in_spec(sb_w),
            _layer_spec(layer, (ret_w, D_MODEL)), _layer_spec(layer, (sb_w, D_MODEL)),
            _layer_spec(layer, (D_MODEL, D_MODEL)),
        ],
        out_specs=pl.BlockSpec((tile, D_MODEL), lambda i: (i, 0)),
        out_shape=jax.ShapeDtypeStruct((out_rows, D_MODEL), F32),
        compiler_params=_params("parallel"),
        name="mixer_merge",
    )(h, g, w_in, w_in, w_in, w_in, w_in, gn, o_r, o_s, w_ret, w_sb, w_out)


def kernel(x, meta, ffn1_norm, ffn1_w_gu, ffn1_w_down, mix_norm, w_in, ret_gn, w_ret_proj,
           w_sb_proj, w_out, ffn2_norm, ffn2_w_gu, ffn2_w_down, final_norm):
    batch, seq, d = x.shape
    real_rows = batch * seq
    rows = real_rows + N_META
    assert d == D_MODEL and seq % BLK == 0 and real_rows % ROW_TILE == 0 and rows % MID_TILE == 0

    f32 = np.float32
    pos = np.concatenate([np.tile(np.arange(seq) + N_META, batch), np.arange(N_META)]).astype(f32)
    freqs = f32(ROPE_BASE) ** (-np.arange(0, RET_DK, 2, dtype=f32) / f32(RET_DK))
    ang = pos[:, None] * freqs[None, :]
    cos_h, sin_h = np.cos(ang).astype(f32), np.sin(ang).astype(f32)
    cos_t = jnp.asarray(np.concatenate([cos_h, cos_h] * 2, axis=1))
    sin_t = jnp.asarray(np.concatenate([-sin_h, sin_h] * 2, axis=1))
    log_gamma = np.log(f32(1.0) - f32(2.0) ** (-5.0 - np.arange(RET_HEADS, dtype=f32))).astype(f32)
    lg_tab = jnp.asarray(np.broadcast_to(log_gamma.reshape(RET_HEADS // 2, 2, 1), (RET_HEADS // 2, 2, BLK)))

    gains = lambda t: t.reshape(t.shape[0], 1, t.shape[1])
    final_gain = final_norm.reshape(1, d)
    for l in range(DEPTH):
        last = l == DEPTH - 1
        ffn1 = functools.partial(_ffn, layer=l, g=gains(ffn1_norm), w_gu=ffn1_w_gu, w_down=ffn1_w_down,
                                 gf=final_gain, final_norm=False)
        if l == 0:
            h = ffn1(x.reshape(real_rows, d), tile=ROW_TILE, out_rows=rows)
            h = ffn1(meta.astype(x.dtype), tile=N_META, out_rows=rows,
                     out_offset=real_rows // N_META, filled=(h,))
        else:
            h = ffn1(h, tile=MID_TILE)
        rq, rk, rv, sq, sk, sv = _proj(h, l, gains(mix_norm), w_in, cos_t, sin_t, tile=MID_TILE)
        o_r = _retention(lg_tab, rq, rk, rv, batch)
        o_s = _stick_breaking(sq, sk, sv, batch)
        tile, out_rows = (ROW_TILE, real_rows) if last else (MID_TILE, rows)
        h = _merge(h, l, gains(mix_norm), w_in, gains(ret_gn), o_r, o_s, w_ret_proj, w_sb_proj, w_out,
                   tile=tile, out_rows=out_rows)
        h = _ffn(h, l, gains(ffn2_norm), ffn2_w_gu, ffn2_w_down, final_gain, final_norm=last, tile=tile)
    return h.reshape(batch, seq, d)
```

```python
import functools

import jax
import jax.numpy as jnp
import numpy as np
from jax import lax
from jax.experimental import pallas as pl
from jax.experimental.pallas import tpu as pltpu

D_MODEL = 1024
DEPTH = 2
N_META = 16
RET_HEADS = 8
RET_DK = 64
RET_DV = 128
SB_HEADS = 8
SB_DH = 64
D_FF = 2816
ROPE_BASE = 10000.0
EPS = 1e-6

LANES = 128
BLK = 256
PAD = BLK - N_META
FF_CHUNK = 256
ROW_TILE = 512
MID_TILE = 656
MASK_BIG = 1e30
LOG2E = 1.4426950408889634
UNDERFLOW_LOG2 = -160.0
VMEM_LIMIT = 56 * 1024 * 1024

_COLS = np.cumsum([0, 512, 512, 1024, 1024, 512, 512, 512, 1024, 1024])
C_RQ, C_RK, C_RV, C_RG, C_SQ, C_SK, C_SV, C_GA, C_GB, C_END = (int(c) for c in _COLS)

F32 = jnp.float32
BF16 = jnp.bfloat16


def _dot(a, b):
    return jnp.dot(a, b, preferred_element_type=F32)


def _dot_nt(a, b):
    return lax.dot_general(a, b, (((1,), (1,)), ((), ())), preferred_element_type=F32)


def _rmsnorm(x, g):
    r = lax.rsqrt(jnp.mean(x * x, axis=-1, keepdims=True) + EPS)
    return (x * r) * g


def _sigmoid(x):
    return 1.0 / (1.0 + jnp.exp(-x))


def _params(*sem):
    return pltpu.CompilerParams(dimension_semantics=sem, vmem_limit_bytes=VMEM_LIMIT)


def _resident_spec(block, index_map):
    return pl.BlockSpec(block, index_map, pipeline_mode=pl.Buffered(1))


def _layer_spec(layer, shape, col_block=0):
    return _resident_spec((None,) + shape, lambda *_: (layer, 0, col_block))


def _ffn_kernel(x_ref, g_ref, wgu_ref, wd_ref, gf_ref, *rest, final_norm):
    o_ref, gate_ref = rest[-2:]
    x = x_ref[...]
    u = _rmsnorm(x, g_ref[...]).astype(BF16)
    for c in range(0, D_FF, FF_CHUNK):
        a = _dot(u, wgu_ref[:, c:c + FF_CHUNK].astype(BF16))
        b = _dot(u, wgu_ref[:, D_FF + c:D_FF + c + FF_CHUNK].astype(BF16))
        gate_ref[:, c:c + FF_CHUNK] = (0.5 * a * _sigmoid(a) * b).astype(BF16)
    y = x + _dot(gate_ref[...], wd_ref[...].astype(BF16))
    if final_norm:
        y = _rmsnorm(y, gf_ref[...])
    o_ref[...] = y


def _ffn(h, layer, g, w_gu, w_down, gf, *, final_norm, tile, out_rows=None, out_offset=0, filled=()):
    out_rows = h.shape[0] if out_rows is None else out_rows
    return pl.pallas_call(
        functools.partial(_ffn_kernel, final_norm=final_norm),
        grid=(h.shape[0] // tile,),
        in_specs=[
            pl.BlockSpec((tile, D_MODEL), lambda i: (i, 0)),
            _layer_spec(layer, (1, D_MODEL)),
            _layer_spec(layer, (D_MODEL, 2 * D_FF)),
            _layer_spec(layer, (D_FF, D_MODEL)),
            _resident_spec((1, D_MODEL), lambda i: (0, 0)),
        ] + [pl.BlockSpec(memory_space=pl.ANY)] * len(filled),
        out_specs=pl.BlockSpec((tile, D_MODEL), lambda i: (i + out_offset, 0)),
        out_shape=jax.ShapeDtypeStruct((out_rows, D_MODEL), F32),
        input_output_aliases={5: 0} if filled else {},
        scratch_shapes=[pltpu.VMEM((tile, D_FF), BF16)],
        compiler_params=_params("parallel"),
        name="ffn_final" if final_norm else "ffn",
    )(h, g, w_gu, w_down, gf, *filled)


def _proj_kernel(x_ref, g_ref, wr_ref, ws_ref, cos_ref, sin_ref,
                 rq_ref, rk_ref, rv_ref, sq_ref, sk_ref, sv_ref):
    u = _rmsnorm(x_ref[...], g_ref[...]).astype(BF16)
    qk_w = RET_HEADS * RET_DK
    cos = jnp.concatenate([cos_ref[...]] * (qk_w // LANES), axis=1)
    sin = jnp.concatenate([sin_ref[...]] * (qk_w // LANES), axis=1)
    lane = lax.broadcasted_iota(jnp.int32, (x_ref.shape[0], qk_w), 1)
    first_half = (lane % RET_DK) < (RET_DK // 2)

    def rotary(t):
        partner = jnp.where(first_half, pltpu.roll(t, qk_w - RET_DK // 2, 1),
                            pltpu.roll(t, RET_DK // 2, 1))
        return t * cos + partner * sin

    def proj(w_ref, lo, hi):
        return _dot(u, w_ref[:, lo:hi].astype(BF16))

    rq_ref[...] = rotary(proj(wr_ref, C_RQ, C_RK)).astype(BF16)
    rk_ref[...] = (rotary(proj(wr_ref, C_RK, C_RV)) * RET_DK ** -0.5).astype(BF16)
    rv_ref[...] = proj(wr_ref, C_RV, C_RG).astype(BF16)
    sq_ref[...] = (proj(ws_ref, 0, C_SK - C_SQ) * (-LOG2E * SB_DH ** -0.5)).astype(BF16)
    sk_ref[...] = proj(ws_ref, C_SK - C_SQ, C_SV - C_SQ).astype(BF16)
    sv_ref[...] = proj(ws_ref, C_SV - C_SQ, C_GA - C_SQ).astype(BF16)


def _proj(h, layer, g, w_in, cos_t, sin_t, *, tile):
    rows = h.shape[0]

    def row_spec(width):
        return pl.BlockSpec((tile, width), lambda i: (i, 0))

    tab_spec = row_spec(LANES)
    widths = (C_RK - C_RQ, C_RV - C_RK, C_RG - C_RV, C_SK - C_SQ, C_SV - C_SK, C_GA - C_SV)
    sb_cols = C_GA - C_SQ
    assert C_SQ % sb_cols == 0
    return pl.pallas_call(
        _proj_kernel,
        grid=(rows // tile,),
        in_specs=[row_spec(D_MODEL), _layer_spec(layer, (1, D_MODEL)),
                  _layer_spec(layer, (D_MODEL, C_RG)),
                  _layer_spec(layer, (D_MODEL, sb_cols), col_block=C_SQ // sb_cols),
                  tab_spec, tab_spec],
        out_specs=[row_spec(w) for w in widths],
        out_shape=[jax.ShapeDtypeStruct((rows, w), BF16) for w in widths],
        compiler_params=_params("parallel"),
        name="mixer_proj",
    )(h, g, w_in, w_in, cos_t, sin_t)


def _meta_chunk(m_ref):
    m = m_ref[...]
    return jnp.concatenate([jnp.zeros((PAD, m.shape[1]), m.dtype), m], axis=0)


def _ret_kernel(lg_ref, *refs, meta_only):
    o_ref, state_ref, dec_ref, kdec_ref, qdec_ref, sdec_ref = refs[-6:]
    c = pl.program_id(1)
    pairs = RET_HEADS // 2

    @pl.when(c == 0)
    def _init():
        t = lax.broadcasted_iota(jnp.int32, (BLK, BLK), 0)
        s = lax.broadcasted_iota(jnp.int32, (BLK, BLK), 1)
        diff = (t - s).astype(F32)
        lane = lax.broadcasted_iota(jnp.int32, (BLK, LANES), 1)
        pos = lax.broadcasted_iota(jnp.int32, (BLK, LANES), 0).astype(F32)
        col = lax.broadcasted_iota(jnp.int32, (BLK, 2 * RET_DV), 1)
        posv = lax.broadcasted_iota(jnp.int32, (BLK, 2 * RET_DV), 0).astype(F32)
        srow = lax.broadcasted_iota(jnp.int32, (LANES, 2 * RET_DV), 0)
        scol = lax.broadcasted_iota(jnp.int32, (LANES, 2 * RET_DV), 1)
        own = (srow // RET_DK) == (scol // RET_DV)
        for p in range(pairs):
            lg = lg_ref[p]
            for i in range(2):
                dec_ref[2 * p + i] = jnp.where(
                    diff >= 0, jnp.exp(lg[i:i + 1, :] * jnp.maximum(diff, 0.0)), 0.0)
            lg_k = jnp.where(lane < RET_DK, lg[0:1, :LANES], lg[1:2, :LANES])
            kdec_ref[p] = jnp.exp(lg_k * (BLK - 1.0 - pos))
            lg_v = jnp.where(col < RET_DV, lg[0:1, :], lg[1:2, :])
            qdec_ref[p] = jnp.exp(lg_v * (posv + 1.0))
            lg_s = jnp.where(scol < RET_DV, lg[0:1, :], lg[1:2, :])
            sdec_ref[2 * p] = jnp.where(own, jnp.exp(lg_s * float(BLK)), 0.0)
            sdec_ref[2 * p + 1] = jnp.where(own, 1.0, 0.0)
        state_ref[...] = jnp.zeros_like(state_ref)

    def decayed_keys_t(k_pair, p):
        return (k_pair.astype(F32) * kdec_ref[p]).T.astype(BF16)

    def chunk(q_all, k_all, v_all):
        lane = lax.broadcasted_iota(jnp.int32, (BLK, LANES), 1)
        qs, ks, crosses, scores, outs = {}, {}, {}, {}, {}

        def pair_start(p):
            qs[p] = q_all[:, p * LANES:(p + 1) * LANES]
            ks[p] = k_all[:, p * LANES:(p + 1) * LANES]
            crosses[p] = _dot(qs[p], state_ref[p].astype(BF16)) * qdec_ref[p]

        def head_scores(head):
            p, i = divmod(head, 2)
            in_head = (lane < RET_DK) if i == 0 else (lane >= RET_DK)
            qi = jnp.where(in_head, qs[p], jnp.zeros_like(qs[p]))
            scores[head] = (_dot_nt(qi, ks[p]) * dec_ref[head]).astype(BF16)

        def head_values(head):
            p, i = divmod(head, 2)
            o = _dot(scores.pop(head), v_all[:, head * RET_DV:(head + 1) * RET_DV])
            outs[head] = o + crosses[p][:, i * RET_DV:(i + 1) * RET_DV]

        def head_norm(head):
            o = outs.pop(head)
            mu = jnp.mean(o, axis=-1, keepdims=True)
            d = o - mu
            var = jnp.mean(d * d, axis=-1, keepdims=True)
            normed = (d * lax.rsqrt(var + EPS)).astype(BF16)
            o_ref[:, head * RET_DV:(head + 1) * RET_DV] = normed[PAD:] if meta_only else normed

        def pair_state(p):
            v = v_all[:, 2 * p * RET_DV:2 * (p + 1) * RET_DV]
            state_ref[p] = (state_ref[p] * sdec_ref[2 * p]
                            + _dot(decayed_keys_t(ks[p], p), v) * sdec_ref[2 * p + 1])

        for t in range(RET_HEADS + 2):
            if t < RET_HEADS:
                if t % 2 == 0:
                    pair_start(t // 2)
                head_scores(t)
            if 0 <= t - 1 < RET_HEADS:
                head_values(t - 1)
            if 0 <= t - 2 < RET_HEADS:
                head_norm(t - 2)
                if (t - 2) % 2 == 1:
                    pair_state((t - 2) // 2)

    if meta_only:
        chunk(*(_meta_chunk(r) for r in refs[:3]))
    else:
        @pl.when(c == 0)
        def _replay_meta_chunk():
            k_m, v_m = _meta_chunk(refs[3]), _meta_chunk(refs[4])
            for p in range(pairs):
                kd_t = decayed_keys_t(k_m[:, p * LANES:(p + 1) * LANES], p)
                v = v_m[:, 2 * p * RET_DV:2 * (p + 1) * RET_DV]
                state_ref[p] = _dot(kd_t, v) * sdec_ref[2 * p + 1]
            o_ref[...] = jnp.zeros_like(o_ref)

        @pl.when(c > 0)
        def _real_chunk():
            chunk(refs[0][...], refs[1][...], refs[2][...])


def _retention(lg_tab, rq, rk, rv, batch):
    rows = rq.shape[0]
    pairs = RET_HEADS // 2
    qk_w, v_w = RET_HEADS * RET_DK, RET_HEADS * RET_DV
    real_rows = rows - N_META
    per_seq = real_rows // BLK // batch
    meta_rows = lambda *_: (real_rows // N_META, 0)
    meta_specs = [pl.BlockSpec((N_META, w), meta_rows) for w in (qk_w, qk_w, v_w)]
    lg_spec = pl.BlockSpec((pairs, 2, BLK), lambda b, c: (0, 0, 0))
    scratch = [
        pltpu.VMEM((pairs, LANES, 2 * RET_DV), F32),
        pltpu.VMEM((RET_HEADS, BLK, BLK), F32),
        pltpu.VMEM((pairs, BLK, LANES), F32),
        pltpu.VMEM((pairs, BLK, 2 * RET_DV), F32),
        pltpu.VMEM((2 * pairs, LANES, 2 * RET_DV), F32),
    ]
    out_shape = jax.ShapeDtypeStruct((rows, v_w), BF16)
    out = pl.pallas_call(
        functools.partial(_ret_kernel, meta_only=True),
        grid=(1, 1),
        in_specs=[lg_spec] + meta_specs,
        out_specs=pl.BlockSpec((N_META, v_w), meta_rows),
        out_shape=out_shape,
        scratch_shapes=scratch,
        compiler_params=_params("arbitrary", "arbitrary"),
        name="retention_meta",
    )(lg_tab, rq, rk, rv)

    def real_rows_of(b, c):
        return b * per_seq + jnp.maximum(c - 1, 0), 0

    return pl.pallas_call(
        functools.partial(_ret_kernel, meta_only=False),
        grid=(batch, per_seq + 1),
        in_specs=[lg_spec] + [pl.BlockSpec((BLK, w), real_rows_of) for w in (qk_w, qk_w, v_w)]
        + meta_specs[1:] + [pl.BlockSpec(memory_space=pl.ANY)],
        out_specs=pl.BlockSpec((BLK, v_w), real_rows_of),
        out_shape=out_shape,
        input_output_aliases={6: 0},
        scratch_shapes=scratch,
        compiler_params=_params("parallel", "arbitrary"),
        name="retention",
    )(lg_tab, rq, rk, rv, rk, rv, out)


def _sb_kernel(q_ref, km_ref, vm_ref, *rest, meta_only):
    o_ref, k_ref, v_ref, acc_ref, carry_ref = rest[-5:]
    step_id = pl.program_id(1)

    @pl.when(step_id == 0)
    def _assemble_sequence():
        k_ref[0:BLK] = _meta_chunk(km_ref)
        v_ref[0:BLK] = _meta_chunk(vm_ref)
        if not meta_only:
            k_ref[BLK:] = rest[0][...]
            v_ref[BLK:] = rest[1][...]

    pairs = SB_HEADS // 2
    lane = lax.broadcasted_iota(jnp.int32, (BLK, LANES), 1)
    q_all = _meta_chunk(q_ref) if meta_only else q_ref[...]
    q_heads = []
    for p in range(pairs):
        q = q_all[:, p * LANES:(p + 1) * LANES]
        zero = jnp.zeros_like(q)
        q_heads += [jnp.where(lane < SB_DH, q, zero), jnp.where(lane >= SB_DH, q, zero)]
    j = lax.broadcasted_iota(jnp.int32, (BLK, BLK), 0)
    c = lax.broadcasted_iota(jnp.int32, (BLK, BLK), 1)
    suffix = jnp.where(j >= c, 1.0, 0.0).astype(BF16)
    key_col = lax.broadcasted_iota(jnp.int32, (1, BLK), 1)

    def step(blocks, first):
        chains = [(kb, bias, h) for kb, bias in blocks for h in range(SB_HEADS)]
        ys, logs, totals, sums = {}, {}, {}, {}
        carries = [None if first else carry_ref[h] for h in range(SB_HEADS)]
        accs = [None] * SB_HEADS

        def soft_log(y):
            return jnp.minimum(y, 0.0) - jnp.log2(1.0 + jnp.exp2(-jnp.abs(y)))

        def rows_of(kb, offset, size):
            start = kb * BLK + offset
            return pl.ds(start if isinstance(start, int) else pl.multiple_of(start, size), size)

        def front(n):
            kb, bias, h = chains[n]
            cols = slice((h // 2) * LANES, (h // 2 + 1) * LANES)
            y = _dot_nt(q_heads[h], k_ref[rows_of(kb, 0, BLK), cols]) + bias
            log_1mb = soft_log(y)
            ys[n], logs[n] = y, log_1mb.astype(BF16)
            totals[n] = jnp.sum(log_1mb, axis=1, keepdims=True)

        def back(n):
            kb, _, h = chains[n]
            cols = slice((h // 2) * LANES, (h // 2 + 1) * LANES)
            incl = sums.pop(n) if carries[h] is None else sums.pop(n) + carries[h]
            a = jnp.exp2(incl - ys.pop(n)).astype(BF16)
            pv = _dot(a, v_ref[rows_of(kb, 0, BLK), cols])
            carries[h] = totals[n] if carries[h] is None else carries[h] + totals[n]
            accs[h] = pv if accs[h] is None else accs[h] + pv

        for t in range(len(chains) + 2):
            if t < len(chains):
                front(t)
            if 0 <= t - 1 < len(chains):
                sums[t - 1] = _dot(logs.pop(t - 1), suffix)
            if 0 <= t - 2 < len(chains):
                back(t - 2)
        for h in range(SB_HEADS):
            carry_ref[h] = carries[h]
            if first:
                acc_ref[h] = accs[h]
            else:
                acc_ref[h] += accs[h]

    def valid_bias(kb):
        return jnp.where(kb * BLK + key_col < PAD, MASK_BIG, 0.0).astype(F32)

    r = lax.broadcasted_iota(jnp.int32, (BLK, BLK), 0)
    s = lax.broadcasted_iota(jnp.int32, (BLK, BLK), 1)
    causal = jnp.where(s < r, 0.0, MASK_BIG).astype(F32)

    if meta_only:
        step([(0, causal + valid_bias(0))], True)
    else:
        qb = step_id + 1
        step([(qb, causal), (qb - 1, valid_bias(qb - 1))], True)

        def decayed():
            top = carry_ref[0]
            for h in range(1, SB_HEADS):
                top = jnp.maximum(top, carry_ref[h])
            return jnp.max(top) < UNDERFLOW_LOG2

        def live(state):
            kb, done = state
            return jnp.logical_and(kb >= 0, jnp.logical_not(done))

        def body(state):
            kb, _ = state
            step([(kb, valid_bias(kb))], False)
            return kb - 1, decayed()

        lax.while_loop(live, body, (qb - 2, decayed()))
    for p in range(pairs):
        out = jnp.where(lane < SB_DH, acc_ref[2 * p], acc_ref[2 * p + 1]).astype(BF16)
        o_ref[:, p * LANES:(p + 1) * LANES] = out[PAD:] if meta_only else out


def _stick_breaking(sq, sk, sv, batch):
    rows, width = sq.shape
    assert PAD <= BLK and width == SB_HEADS * SB_DH
    real_rows = rows - N_META
    per_seq = real_rows // BLK // batch
    seq = per_seq * BLK
    meta_spec = pl.BlockSpec((N_META, width), lambda b, i: (real_rows // N_META, 0))
    out_shape = jax.ShapeDtypeStruct((rows, width), BF16)

    def scratch(key_rows):
        return [pltpu.VMEM((key_rows, width), BF16), pltpu.VMEM((key_rows, width), BF16),
                pltpu.VMEM((SB_HEADS, BLK, LANES), F32), pltpu.VMEM((SB_HEADS, BLK, 1), F32)]

    out = pl.pallas_call(
        functools.partial(_sb_kernel, meta_only=True),
        grid=(1, 1),
        in_specs=[meta_spec] * 3,
        out_specs=meta_spec,
        out_shape=out_shape,
        scratch_shapes=scratch(BLK),
        compiler_params=_params("arbitrary", "arbitrary"),
        name="stick_breaking_meta",
    )(sq, sk, sv)
    q_spec = pl.BlockSpec((BLK, width), lambda b, i: (b * per_seq + i, 0))
    real_spec = pl.BlockSpec((seq, width), lambda b, i: (b, 0))
    return pl.pallas_call(
        functools.partial(_sb_kernel, meta_only=False),
        grid=(batch, per_seq),
        in_specs=[q_spec, meta_spec, meta_spec, real_spec, real_spec, pl.BlockSpec(memory_space=pl.ANY)],
        out_specs=q_spec,
        out_shape=out_shape,
        input_output_aliases={5: 0},
        scratch_shapes=scratch(seq + BLK),
        compiler_params=_params("parallel", "arbitrary"),
        name="stick_breaking",
    )(sq, sk, sv, sk, sv, out)


def _merge_kernel(x_ref, g_ref, wrg_ref, wga0_ref, wga1_ref, wgb0_ref, wgb1_ref, gn_ref,
                  or_ref, os_ref, wr_ref, ws_ref, wo_ref, o_ref):
    x = x_ref[...]
    u = _rmsnorm(x, g_ref[...]).astype(BF16)

    def gate(*w_refs):
        return jnp.concatenate([_dot(u, w[...].astype(BF16)) for w in w_refs], axis=1)

    rg = gate(wrg_ref)
    y_sb = _dot(os_ref[...], ws_ref[...].astype(BF16))
    gb = _sigmoid(gate(wgb0_ref, wgb1_ref)) * y_sb
    ga = _sigmoid(gate(wga0_ref, wga1_ref))
    gated = (rg * _sigmoid(rg)) * (or_ref[...].astype(F32) * gn_ref[...])
    y_ret = _dot(gated.astype(BF16), wr_ref[...].astype(BF16))
    y = ga * y_ret + gb
    o_ref[...] = x + _dot(y.astype(BF16), wo_ref[...].astype(BF16))


def _merge(h, layer, g, w_in, gn, o_r, o_s, w_ret, w_sb, w_out, *, tile, out_rows):
    def in_spec(width):
        return pl.BlockSpec((tile, width), lambda i: (i, 0))

    half = D_MODEL // 2
    assert C_RG % D_MODEL == 0 and C_GA % half == 0
    ret_w, sb_w = RET_HEADS * RET_DV, SB_HEADS * SB_DH
    return pl.pallas_call(
        _merge_kernel,
        grid=(out_rows // tile,),
        in_specs=[
            in_spec(D_MODEL), _layer_spec(layer, (1, D_MODEL)),
            _layer_spec(layer, (D_MODEL, D_MODEL), col_block=C_RG // D_MODEL),
            _layer_spec(layer, (D_MODEL, half), col_block=C_GA // half),
            _layer_spec(layer, (D_MODEL, half), col_block=C_GA // half + 1),
            _layer_spec(layer, (D_MODEL, half), col_block=C_GB // half),
            _layer_spec(layer, (D_MODEL, half), col_block=C_GB // half + 1),
            _layer_spec(layer, (1, ret_w)), in_spec(ret_w), in_spec(sb_w),
            _layer_spec(layer, (ret_w, D_MODEL)), _layer_spec(layer, (sb_w, D_MODEL)),
            _layer_spec(layer, (D_MODEL, D_MODEL)),
        ],
        out_specs=pl.BlockSpec((tile, D_MODEL), lambda i: (i, 0)),
        out_shape=jax.ShapeDtypeStruct((out_rows, D_MODEL), F32),
        compiler_params=_params("parallel"),
        name="mixer_merge",
    )(h, g, w_in, w_in, w_in, w_in, w_in, gn, o_r, o_s, w_ret, w_sb, w_out)


def kernel(x, meta, ffn1_norm, ffn1_w_gu, ffn1_w_down, mix_norm, w_in, ret_gn, w_ret_proj,
           w_sb_proj, w_out, ffn2_norm, ffn2_w_gu, ffn2_w_down, final_norm):
    batch, seq, d = x.shape
    real_rows = batch * seq
    rows = real_rows + N_META
    assert d == D_MODEL and seq % BLK == 0 and real_rows % ROW_TILE == 0 and rows % MID_TILE == 0

    f32 = np.float32
    pos = np.concatenate([np.tile(np.arange(seq) + N_META, batch), np.arange(N_META)]).astype(f32)
    freqs = f32(ROPE_BASE) ** (-np.arange(0, RET_DK, 2, dtype=f32) / f32(RET_DK))
    ang = pos[:, None] * freqs[None, :]
    cos_h, sin_h = np.cos(ang).astype(f32), np.sin(ang).astype(f32)
    cos_t = jnp.asarray(np.concatenate([cos_h, cos_h] * 2, axis=1))
    sin_t = jnp.asarray(np.concatenate([-sin_h, sin_h] * 2, axis=1))
    log_gamma = np.log(f32(1.0) - f32(2.0) ** (-5.0 - np.arange(RET_HEADS, dtype=f32))).astype(f32)
    lg_tab = jnp.asarray(np.broadcast_to(log_gamma.reshape(RET_HEADS // 2, 2, 1), (RET_HEADS // 2, 2, BLK)))

    gains = lambda t: t.reshape(t.shape[0], 1, t.shape[1])
    final_gain = final_norm.reshape(1, d)
    for l in range(DEPTH):
        last = l == DEPTH - 1
        ffn1 = functools.partial(_ffn, layer=l, g=gains(ffn1_norm), w_gu=ffn1_w_gu, w_down=ffn1_w_down,
                                 gf=final_gain, final_norm=False)
        if l == 0:
            h = ffn1(x.reshape(real_rows, d), tile=ROW_TILE, out_rows=rows)
            h = ffn1(meta.astype(x.dtype), tile=N_META, out_rows=rows,
                     out_offset=real_rows // N_META, filled=(h,))
        else:
            h = ffn1(h, tile=MID_TILE)
        rq, rk, rv, sq, sk, sv = _proj(h, l, gains(mix_norm), w_in, cos_t, sin_t, tile=MID_TILE)
        o_r = _retention(lg_tab, rq, rk, rv, batch)
        o_s = _stick_breaking(sq, sk, sv, batch)
        tile, out_rows = (ROW_TILE, real_rows) if last else (MID_TILE, rows)
        h = _merge(h, l, gains(mix_norm), w_in, gains(ret_gn), o_r, o_s, w_ret_proj, w_sb_proj, w_out,
                   tile=tile, out_rows=out_rows)
        h = _ffn(h, l, gains(ffn2_norm), ffn2_w_gu, ffn2_w_down, final_gain, final_norm=last, tile=tile)
    return h.reshape(batch, seq, d)
```

```python
import functools

import jax
import jax.numpy as jnp
import numpy as np
from jax import lax
from jax.experimental import pallas as pl
from jax.experimental.pallas import tpu as pltpu

D_MODEL = 1024
DEPTH = 2
N_META = 16
RET_HEADS = 8
RET_DK = 64
RET_DV = 128
SB_HEADS = 8
SB_DH = 64
D_FF = 2816
ROPE_BASE = 10000.0
EPS = 1e-6

LANES = 128
BLK = 256
PAD = BLK - N_META
FF_CHUNK = 256
ROW_TILE = 512
MID_TILE = 656
QUERY_BLOCKS = 2
MASK_BIG = 1e30
LOG2E = 1.4426950408889634
UNDERFLOW_LOG2 = -160.0
VMEM_LIMIT = 56 * 1024 * 1024

_COLS = np.cumsum([0, 512, 512, 1024, 1024, 512, 512, 512, 1024, 1024])
C_RQ, C_RK, C_RV, C_RG, C_SQ, C_SK, C_SV, C_GA, C_GB, C_END = (int(c) for c in _COLS)

F32 = jnp.float32
BF16 = jnp.bfloat16


def _dot(a, b):
    return jnp.dot(a, b, preferred_element_type=F32)


def _dot_nt(a, b):
    return lax.dot_general(a, b, (((1,), (1,)), ((), ())), preferred_element_type=F32)


def _rmsnorm(x, g):
    r = lax.rsqrt(jnp.mean(x * x, axis=-1, keepdims=True) + EPS)
    return (x * r) * g


def _sigmoid(x):
    return 1.0 / (1.0 + jnp.exp(-x))


def _params(*sem):
    return pltpu.CompilerParams(dimension_semantics=sem, vmem_limit_bytes=VMEM_LIMIT)


def _resident_spec(block, index_map):
    return pl.BlockSpec(block, index_map, pipeline_mode=pl.Buffered(1))


def _layer_spec(layer, shape, col_block=0):
    return _resident_spec((None,) + shape, lambda *_: (layer, 0, col_block))


def _ffn_kernel(x_ref, g_ref, wgu_ref, wd_ref, gf_ref, *rest, final_norm):
    o_ref, gate_ref = rest[-2:]
    x = x_ref[...]
    u = _rmsnorm(x, g_ref[...]).astype(BF16)
    for c in range(0, D_FF, FF_CHUNK):
        a = _dot(u, wgu_ref[:, c:c + FF_CHUNK].astype(BF16))
        b = _dot(u, wgu_ref[:, D_FF + c:D_FF + c + FF_CHUNK].astype(BF16))
        gate_ref[:, c:c + FF_CHUNK] = (0.5 * a * _sigmoid(a) * b).astype(BF16)
    y = x + _dot(gate_ref[...], wd_ref[...].astype(BF16))
    if final_norm:
        y = _rmsnorm(y, gf_ref[...])
    o_ref[...] = y


def _ffn(h, layer, g, w_gu, w_down, gf, *, final_norm, tile, out_rows=None, out_offset=0, filled=()):
    out_rows = h.shape[0] if out_rows is None else out_rows
    return pl.pallas_call(
        functools.partial(_ffn_kernel, final_norm=final_norm),
        grid=(h.shape[0] // tile,),
        in_specs=[
            pl.BlockSpec((tile, D_MODEL), lambda i: (i, 0)),
            _layer_spec(layer, (1, D_MODEL)),
            _layer_spec(layer, (D_MODEL, 2 * D_FF)),
            _layer_spec(layer, (D_FF, D_MODEL)),
            _resident_spec((1, D_MODEL), lambda i: (0, 0)),
        ] + [pl.BlockSpec(memory_space=pl.ANY)] * len(filled),
        out_specs=pl.BlockSpec((tile, D_MODEL), lambda i: (i + out_offset, 0)),
        out_shape=jax.ShapeDtypeStruct((out_rows, D_MODEL), F32),
        input_output_aliases={5: 0} if filled else {},
        scratch_shapes=[pltpu.VMEM((tile, D_FF), BF16)],
        compiler_params=_params("parallel"),
        name="ffn_final" if final_norm else "ffn",
    )(h, g, w_gu, w_down, gf, *filled)


def _proj_kernel(x_ref, g_ref, wr_ref, ws_ref, cos_ref, sin_ref,
                 rq_ref, rk_ref, rv_ref, sq_ref, sk_ref, sv_ref):
    u = _rmsnorm(x_ref[...], g_ref[...]).astype(BF16)
    qk_w = RET_HEADS * RET_DK
    cos = jnp.concatenate([cos_ref[...]] * (qk_w // LANES), axis=1)
    sin = jnp.concatenate([sin_ref[...]] * (qk_w // LANES), axis=1)
    lane = lax.broadcasted_iota(jnp.int32, (x_ref.shape[0], qk_w), 1)
    first_half = (lane % RET_DK) < (RET_DK // 2)

    def rotary(t):
        partner = jnp.where(first_half, pltpu.roll(t, qk_w - RET_DK // 2, 1),
                            pltpu.roll(t, RET_DK // 2, 1))
        return t * cos + partner * sin

    def proj(w_ref, lo, hi):
        return _dot(u, w_ref[:, lo:hi].astype(BF16))

    rq_ref[...] = rotary(proj(wr_ref, C_RQ, C_RK)).astype(BF16)
    rk_ref[...] = (rotary(proj(wr_ref, C_RK, C_RV)) * RET_DK ** -0.5).astype(BF16)
    rv_ref[...] = proj(wr_ref, C_RV, C_RG).astype(BF16)
    sq_ref[...] = (proj(ws_ref, 0, C_SK - C_SQ) * (-LOG2E * SB_DH ** -0.5)).astype(BF16)
    sk_ref[...] = proj(ws_ref, C_SK - C_SQ, C_SV - C_SQ).astype(BF16)
    sv_ref[...] = proj(ws_ref, C_SV - C_SQ, C_GA - C_SQ).astype(BF16)


def _proj(h, layer, g, w_in, cos_t, sin_t, *, tile):
    rows = h.shape[0]

    def row_spec(width):
        return pl.BlockSpec((tile, width), lambda i: (i, 0))

    tab_spec = row_spec(LANES)
    widths = (C_RK - C_RQ, C_RV - C_RK, C_RG - C_RV, C_SK - C_SQ, C_SV - C_SK, C_GA - C_SV)
    sb_cols = C_GA - C_SQ
    assert C_SQ % sb_cols == 0
    return pl.pallas_call(
        _proj_kernel,
        grid=(rows // tile,),
        in_specs=[row_spec(D_MODEL), _layer_spec(layer, (1, D_MODEL)),
                  _layer_spec(layer, (D_MODEL, C_RG)),
                  _layer_spec(layer, (D_MODEL, sb_cols), col_block=C_SQ // sb_cols),
                  tab_spec, tab_spec],
        out_specs=[row_spec(w) for w in widths],
        out_shape=[jax.ShapeDtypeStruct((rows, w), BF16) for w in widths],
        compiler_params=_params("parallel"),
        name="mixer_proj",
    )(h, g, w_in, w_in, cos_t, sin_t)


def _meta_chunk(m_ref):
    m = m_ref[...]
    return jnp.concatenate([jnp.zeros((PAD, m.shape[1]), m.dtype), m], axis=0)


def _ret_kernel(lg_ref, *refs, meta_only):
    o_ref, state_ref, dec_ref, kdec_ref, qdec_ref, sdec_ref = refs[-6:]
    c = pl.program_id(1)
    pairs = RET_HEADS // 2

    @pl.when(c == 0)
    def _init():
        t = lax.broadcasted_iota(jnp.int32, (BLK, BLK), 0)
        s = lax.broadcasted_iota(jnp.int32, (BLK, BLK), 1)
        diff = (t - s).astype(F32)
        lane = lax.broadcasted_iota(jnp.int32, (BLK, LANES), 1)
        pos = lax.broadcasted_iota(jnp.int32, (BLK, LANES), 0).astype(F32)
        col = lax.broadcasted_iota(jnp.int32, (BLK, 2 * RET_DV), 1)
        posv = lax.broadcasted_iota(jnp.int32, (BLK, 2 * RET_DV), 0).astype(F32)
        srow = lax.broadcasted_iota(jnp.int32, (LANES, 2 * RET_DV), 0)
        scol = lax.broadcasted_iota(jnp.int32, (LANES, 2 * RET_DV), 1)
        own = (srow // RET_DK) == (scol // RET_DV)
        for p in range(pairs):
            lg = lg_ref[p]
            for i in range(2):
                dec_ref[2 * p + i] = jnp.where(
                    diff >= 0, jnp.exp(lg[i:i + 1, :] * jnp.maximum(diff, 0.0)), 0.0)
            lg_k = jnp.where(lane < RET_DK, lg[0:1, :LANES], lg[1:2, :LANES])
            kdec_ref[p] = jnp.exp(lg_k * (BLK - 1.0 - pos))
            lg_v = jnp.where(col < RET_DV, lg[0:1, :], lg[1:2, :])
            qdec_ref[p] = jnp.exp(lg_v * (posv + 1.0))
            lg_s = jnp.where(scol < RET_DV, lg[0:1, :], lg[1:2, :])
            sdec_ref[2 * p] = jnp.where(own, jnp.exp(lg_s * float(BLK)), 0.0)
            sdec_ref[2 * p + 1] = jnp.where(own, 1.0, 0.0)
        state_ref[...] = jnp.zeros_like(state_ref)

    def decayed_keys_t(k_pair, p):
        return (k_pair.astype(F32) * kdec_ref[p]).T.astype(BF16)

    def chunk(q_all, k_all, v_all):
        lane = lax.broadcasted_iota(jnp.int32, (BLK, LANES), 1)
        qs, ks, crosses, scores, outs = {}, {}, {}, {}, {}

        def pair_start(p):
            qs[p] = q_all[:, p * LANES:(p + 1) * LANES]
            ks[p] = k_all[:, p * LANES:(p + 1) * LANES]
            crosses[p] = _dot(qs[p], state_ref[p].astype(BF16)) * qdec_ref[p]

        def head_scores(head):
            p, i = divmod(head, 2)
            in_head = (lane < RET_DK) if i == 0 else (lane >= RET_DK)
            qi = jnp.where(in_head, qs[p], jnp.zeros_like(qs[p]))
            scores[head] = (_dot_nt(qi, ks[p]) * dec_ref[head]).astype(BF16)

        def head_values(head):
            p, i = divmod(head, 2)
            o = _dot(scores.pop(head), v_all[:, head * RET_DV:(head + 1) * RET_DV])
            outs[head] = o + crosses[p][:, i * RET_DV:(i + 1) * RET_DV]

        def head_norm(head):
            o = outs.pop(head)
            mu = jnp.mean(o, axis=-1, keepdims=True)
            d = o - mu
            var = jnp.mean(d * d, axis=-1, keepdims=True)
            normed = (d * lax.rsqrt(var + EPS)).astype(BF16)
            o_ref[:, head * RET_DV:(head + 1) * RET_DV] = normed[PAD:] if meta_only else normed

        def pair_state(p):
            v = v_all[:, 2 * p * RET_DV:2 * (p + 1) * RET_DV]
            state_ref[p] = (state_ref[p] * sdec_ref[2 * p]
                            + _dot(decayed_keys_t(ks[p], p), v) * sdec_ref[2 * p + 1])

        for t in range(RET_HEADS + 2):
            if t < RET_HEADS:
                if t % 2 == 0:
                    pair_start(t // 2)
                head_scores(t)
            if 0 <= t - 1 < RET_HEADS:
                head_values(t - 1)
            if 0 <= t - 2 < RET_HEADS:
                head_norm(t - 2)
                if (t - 2) % 2 == 1:
                    pair_state((t - 2) // 2)

    if meta_only:
        chunk(*(_meta_chunk(r) for r in refs[:3]))
    else:
        @pl.when(c == 0)
        def _replay_meta_chunk():
            k_m, v_m = _meta_chunk(refs[3]), _meta_chunk(refs[4])
            for p in range(pairs):
                kd_t = decayed_keys_t(k_m[:, p * LANES:(p + 1) * LANES], p)
                v = v_m[:, 2 * p * RET_DV:2 * (p + 1) * RET_DV]
                state_ref[p] = _dot(kd_t, v) * sdec_ref[2 * p + 1]
            o_ref[...] = jnp.zeros_like(o_ref)

        @pl.when(c > 0)
        def _real_chunk():
            chunk(refs[0][...], refs[1][...], refs[2][...])


def _retention(lg_tab, rq, rk, rv, batch):
    rows = rq.shape[0]
    pairs = RET_HEADS // 2
    qk_w, v_w = RET_HEADS * RET_DK, RET_HEADS * RET_DV
    real_rows = rows - N_META
    per_seq = real_rows // BLK // batch
    meta_rows = lambda *_: (real_rows // N_META, 0)
    meta_specs = [pl.BlockSpec((N_META, w), meta_rows) for w in (qk_w, qk_w, v_w)]
    lg_spec = pl.BlockSpec((pairs, 2, BLK), lambda b, c: (0, 0, 0))
    scratch = [
        pltpu.VMEM((pairs, LANES, 2 * RET_DV), F32),
        pltpu.VMEM((RET_HEADS, BLK, BLK), F32),
        pltpu.VMEM((pairs, BLK, LANES), F32),
        pltpu.VMEM((pairs, BLK, 2 * RET_DV), F32),
        pltpu.VMEM((2 * pairs, LANES, 2 * RET_DV), F32),
    ]
    out_shape = jax.ShapeDtypeStruct((rows, v_w), BF16)
    out = pl.pallas_call(
        functools.partial(_ret_kernel, meta_only=True),
        grid=(1, 1),
        in_specs=[lg_spec] + meta_specs,
        out_specs=pl.BlockSpec((N_META, v_w), meta_rows),
        out_shape=out_shape,
        scratch_shapes=scratch,
        compiler_params=_params("arbitrary", "arbitrary"),
        name="retention_meta",
    )(lg_tab, rq, rk, rv)

    def real_rows_of(b, c):
        return b * per_seq + jnp.maximum(c - 1, 0), 0

    return pl.pallas_call(
        functools.partial(_ret_kernel, meta_only=False),
        grid=(batch, per_seq + 1),
        in_specs=[lg_spec] + [pl.BlockSpec((BLK, w), real_rows_of) for w in (qk_w, qk_w, v_w)]
        + meta_specs[1:] + [pl.BlockSpec(memory_space=pl.ANY)],
        out_specs=pl.BlockSpec((BLK, v_w), real_rows_of),
        out_shape=out_shape,
        input_output_aliases={6: 0},
        scratch_shapes=scratch,
        compiler_params=_params("parallel", "arbitrary"),
        name="retention",
    )(lg_tab, rq, rk, rv, rk, rv, out)


def _sb_kernel(q_ref, km_ref, vm_ref, *rest, meta_only):
    o_ref, k_ref, v_ref, acc_ref, carry_ref = rest[-5:]
    step_id = pl.program_id(1)

    @pl.when(step_id == 0)
    def _assemble_sequence():
        k_ref[0:BLK] = _meta_chunk(km_ref)
        v_ref[0:BLK] = _meta_chunk(vm_ref)
        if not meta_only:
            k_ref[BLK:] = rest[0][...]
            v_ref[BLK:] = rest[1][...]

    pairs = SB_HEADS // 2
    lane = lax.broadcasted_iota(jnp.int32, (BLK, LANES), 1)
    q_all = _meta_chunk(q_ref) if meta_only else q_ref[...]
    q_heads = []
    for qi in range(q_all.shape[0] // BLK):
        for p in range(pairs):
            q = q_all[qi * BLK:(qi + 1) * BLK, p * LANES:(p + 1) * LANES]
            zero = jnp.zeros_like(q)
            q_heads += [jnp.where(lane < SB_DH, q, zero), jnp.where(lane >= SB_DH, q, zero)]
    j = lax.broadcasted_iota(jnp.int32, (BLK, BLK), 0)
    c = lax.broadcasted_iota(jnp.int32, (BLK, BLK), 1)
    suffix = jnp.where(j >= c, 1.0, 0.0).astype(BF16)
    key_col = lax.broadcasted_iota(jnp.int32, (1, BLK), 1)

    def step(blocks, first):
        chains = [(qi * SB_HEADS + h, kb, bias, h) for qi, kb, bias in blocks for h in range(SB_HEADS)]
        slots = sorted({slot for slot, _, _, _ in chains})
        ys, logs, totals, sums = {}, {}, {}, {}
        carries = {slot: None if first else carry_ref[slot] for slot in slots}
        accs = {slot: None for slot in slots}

        def soft_log(y):
            return jnp.minimum(y, 0.0) - jnp.log2(1.0 + jnp.exp2(-jnp.abs(y)))

        def rows_of(kb):
            start = kb * BLK
            return pl.ds(start if isinstance(start, int) else pl.multiple_of(start, BLK), BLK)

        def front(n):
            slot, kb, bias, h = chains[n]
            cols = slice((h // 2) * LANES, (h // 2 + 1) * LANES)
            y = _dot_nt(q_heads[slot], k_ref[rows_of(kb), cols])
            if bias is not None:
                y = y + bias
            log_1mb = soft_log(y)
            ys[n], logs[n] = y, log_1mb.astype(BF16)
            totals[n] = jnp.sum(log_1mb, axis=1, keepdims=True)

        def back(n):
            slot, kb, _, h = chains[n]
            cols = slice((h // 2) * LANES, (h // 2 + 1) * LANES)
            incl = sums.pop(n) if carries[slot] is None else sums.pop(n) + carries[slot]
            a = jnp.exp2(incl - ys.pop(n)).astype(BF16)
            pv = _dot(a, v_ref[rows_of(kb), cols])
            carries[slot] = totals[n] if carries[slot] is None else carries[slot] + totals[n]
            accs[slot] = pv if accs[slot] is None else accs[slot] + pv

        for t in range(len(chains) + 2):
            if t < len(chains):
                front(t)
            if 0 <= t - 1 < len(chains):
                sums[t - 1] = _dot(logs.pop(t - 1), suffix)
            if 0 <= t - 2 < len(chains):
                back(t - 2)
        for slot in slots:
            carry_ref[slot] = carries[slot]
            if first:
                acc_ref[slot] = accs[slot]
            else:
                acc_ref[slot] += accs[slot]

    def valid_bias(kb):
        return jnp.where(kb * BLK + key_col < PAD, MASK_BIG, 0.0).astype(F32)

    r = lax.broadcasted_iota(jnp.int32, (BLK, BLK), 0)
    s = lax.broadcasted_iota(jnp.int32, (BLK, BLK), 1)
    causal = jnp.where(s < r, 0.0, MASK_BIG).astype(F32)

    if meta_only:
        step([(0, 0, causal + valid_bias(0))], True)
    else:
        first_q = QUERY_BLOCKS * step_id + 1
        step([(0, first_q, causal), (0, first_q - 1, valid_bias(first_q - 1)),
              (1, first_q + 1, causal), (1, first_q, None)], True)

        for qi in range(QUERY_BLOCKS):
            def decayed(qi=qi):
                top = carry_ref[qi * SB_HEADS]
                for h in range(1, SB_HEADS):
                    top = jnp.maximum(top, carry_ref[qi * SB_HEADS + h])
                return jnp.max(top) < UNDERFLOW_LOG2

            def live(state):
                kb, done = state
                return jnp.logical_and(kb >= 0, jnp.logical_not(done))

            def body(state, qi=qi, decayed=decayed):
                kb, _ = state
                step([(qi, kb, valid_bias(kb))], False)
                return kb - 1, decayed()

            lax.while_loop(live, body, (first_q + qi - 2, decayed()))
    for qi in range(q_all.shape[0] // BLK):
        for p in range(pairs):
            slot = qi * SB_HEADS + 2 * p
            out = jnp.where(lane < SB_DH, acc_ref[slot], acc_ref[slot + 1]).astype(BF16)
            rows = slice(None) if meta_only else slice(qi * BLK, (qi + 1) * BLK)
            o_ref[rows, p * LANES:(p + 1) * LANES] = out[PAD:] if meta_only else out


def _stick_breaking(sq, sk, sv, batch):
    rows, width = sq.shape
    assert PAD <= BLK and width == SB_HEADS * SB_DH
    real_rows = rows - N_META
    per_seq = real_rows // BLK // batch
    seq = per_seq * BLK
    meta_spec = pl.BlockSpec((N_META, width), lambda b, i: (real_rows // N_META, 0))
    out_shape = jax.ShapeDtypeStruct((rows, width), BF16)

    def scratch(key_rows, query_blocks):
        slots = query_blocks * SB_HEADS
        return [pltpu.VMEM((key_rows, width), BF16), pltpu.VMEM((key_rows, width), BF16),
                pltpu.VMEM((slots, BLK, LANES), F32), pltpu.VMEM((slots, BLK, 1), F32)]

    out = pl.pallas_call(
        functools.partial(_sb_kernel, meta_only=True),
        grid=(1, 1),
        in_specs=[meta_spec] * 3,
        out_specs=meta_spec,
        out_shape=out_shape,
        scratch_shapes=scratch(BLK, 1),
        compiler_params=_params("arbitrary", "arbitrary"),
        name="stick_breaking_meta",
    )(sq, sk, sv)
    assert per_seq % QUERY_BLOCKS == 0
    steps = per_seq // QUERY_BLOCKS
    q_spec = pl.BlockSpec((QUERY_BLOCKS * BLK, width), lambda b, i: (b * steps + i, 0))
    real_spec = pl.BlockSpec((seq, width), lambda b, i: (b, 0))
    return pl.pallas_call(
        functools.partial(_sb_kernel, meta_only=False),
        grid=(batch, steps),
        in_specs=[q_spec, meta_spec, meta_spec, real_spec, real_spec, pl.BlockSpec(memory_space=pl.ANY)],
        out_specs=q_spec,
        out_shape=out_shape,
        input_output_aliases={5: 0},
        scratch_shapes=scratch(seq + BLK, QUERY_BLOCKS),
        compiler_params=_params("parallel", "arbitrary"),
        name="stick_breaking",
    )(sq, sk, sv, sk, sv, out)


def _merge_kernel(x_ref, g_ref, wrg_ref, wga0_ref, wga1_ref, wgb0_ref, wgb1_ref, gn_ref,
                  or_ref, os_ref, wr_ref, ws_ref, wo_ref, o_ref):
    x = x_ref[...]
    u = _rmsnorm(x, g_ref[...]).astype(BF16)

    def gate(*w_refs):
        return jnp.concatenate([_dot(u, w[...].astype(BF16)) for w in w_refs], axis=1)

    rg = gate(wrg_ref)
    y_sb = _dot(os_ref[...], ws_ref[...].astype(BF16))
    gb = _sigmoid(gate(wgb0_ref, wgb1_ref)) * y_sb
    ga = _sigmoid(gate(wga0_ref, wga1_ref))
    gated = (rg * _sigmoid(rg)) * (or_ref[...].astype(F32) * gn_ref[...])
    y_ret = _dot(gated.astype(BF16), wr_ref[...].astype(BF16))
    y = ga * y_ret + gb
    o_ref[...] = x + _dot(y.astype(BF16), wo_ref[...].astype(BF16))


def _merge(h, layer, g, w_in, gn, o_r, o_s, w_ret, w_sb, w_out, *, tile, out_rows):
    def in_spec(width):
        return pl.BlockSpec((tile, width), lambda i: (i, 0))

    half = D_MODEL // 2
    assert C_RG % D_MODEL == 0 and C_GA % half == 0
    ret_w, sb_w = RET_HEADS * RET_DV, SB_HEADS * SB_DH
    return pl.pallas_call(
        _merge_kernel,
        grid=(out_rows // tile,),
        in_specs=[
            in_spec(D_MODEL), _layer_spec(layer, (1, D_MODEL)),
            _layer_spec(layer, (D_MODEL, D_MODEL), col_block=C_RG // D_MODEL),
            _layer_spec(layer, (D_MODEL, half), col_block=C_GA // half),
            _layer_spec(layer, (D_MODEL, half), col_block=C_GA // half + 1),
            _layer_spec(layer, (D_MODEL, half), col_block=C_GB // half),
            _layer_spec(layer, (D_MODEL, half), col_block=C_GB // half + 1),
            _layer_spec(layer, (1, ret_w)), in_spec(ret_w), in_spec(sb_w),
            _layer_spec(layer, (ret_w, D_MODEL)), _layer_spec(layer, (sb_w, D_MODEL)),
            _layer_spec(layer, (D_MODEL, D_MODEL)),
        ],
        out_specs=pl.BlockSpec((tile, D_MODEL), lambda i: (i, 0)),
        out_shape=jax.ShapeDtypeStruct((out_rows, D_MODEL), F32),
        compiler_params=_params("parallel"),
        name="mixer_merge",
    )(h, g, w_in, w_in, w_in, w_in, w_in, gn, o_r, o_s, w_ret, w_sb, w_out)


def kernel(x, meta, ffn1_norm, ffn1_w_gu, ffn1_w_down, mix_norm, w_in, ret_gn, w_ret_proj,
           w_sb_proj, w_out, ffn2_norm, ffn2_w_gu, ffn2_w_down, final_norm):
    batch, seq, d = x.shape
    real_rows = batch * seq
    rows = real_rows + N_META
    assert d == D_MODEL and seq % BLK == 0 and real_rows % ROW_TILE == 0 and rows % MID_TILE == 0

    f32 = np.float32
    pos = np.concatenate([np.tile(np.arange(seq) + N_META, batch), np.arange(N_META)]).astype(f32)
    freqs = f32(ROPE_BASE) ** (-np.arange(0, RET_DK, 2, dtype=f32) / f32(RET_DK))
    ang = pos[:, None] * freqs[None, :]
    cos_h, sin_h = np.cos(ang).astype(f32), np.sin(ang).astype(f32)
    cos_t = jnp.asarray(np.concatenate([cos_h, cos_h] * 2, axis=1))
    sin_t = jnp.asarray(np.concatenate([-sin_h, sin_h] * 2, axis=1))
    log_gamma = np.log(f32(1.0) - f32(2.0) ** (-5.0 - np.arange(RET_HEADS, dtype=f32))).astype(f32)
    lg_tab = jnp.asarray(np.broadcast_to(log_gamma.reshape(RET_HEADS // 2, 2, 1), (RET_HEADS // 2, 2, BLK)))

    gains = lambda t: t.reshape(t.shape[0], 1, t.shape[1])
    final_gain = final_norm.reshape(1, d)
    for l in range(DEPTH):
        last = l == DEPTH - 1
        ffn1 = functools.partial(_ffn, layer=l, g=gains(ffn1_norm), w_gu=ffn1_w_gu, w_down=ffn1_w_down,
                                 gf=final_gain, final_norm=False)
        if l == 0:
            h = ffn1(x.reshape(real_rows, d), tile=ROW_TILE, out_rows=rows)
            h = ffn1(meta.astype(x.dtype), tile=N_META, out_rows=rows,
                     out_offset=real_rows // N_META, filled=(h,))
        else:
            h = ffn1(h, tile=MID_TILE)
        rq, rk, rv, sq, sk, sv = _proj(h, l, gains(mix_norm), w_in, cos_t, sin_t, tile=MID_TILE)
        o_r = _retention(lg_tab, rq, rk, rv, batch)
        o_s = _stick_breaking(sq, sk, sv, batch)
        tile, out_rows = (ROW_TILE, real_rows) if last else (MID_TILE, rows)
        h = _merge(h, l, gains(mix_norm), w_in, gains(ret_gn), o_r, o_s, w_ret_proj, w_sb_proj, w_out,
                   tile=tile, out_rows=out_rows)
        h = _ffn(h, l, gains(ffn2_norm), ffn2_w_gu, ffn2_w_down, final_gain, final_norm=last, tile=tile)
    return h.reshape(batch, seq, d)
```

```python
import functools

import jax
import jax.numpy as jnp
import numpy as np
from jax import lax
from jax.experimental import pallas as pl
from jax.experimental.pallas import tpu as pltpu

D_MODEL = 1024
DEPTH = 2
N_META = 16
RET_HEADS = 8
RET_DK = 64
RET_DV = 128
SB_HEADS = 8
SB_DH = 64
D_FF = 2816
ROPE_BASE = 10000.0
EPS = 1e-6

LANES = 128
BLK = 256
PAD = BLK - N_META
FF_CHUNK = 256
ROW_TILE = 512
MID_TILE = 656
QUERY_BLOCKS = 2
RET_CHUNKS = 2
MASK_BIG = 1e30
LOG2E = 1.4426950408889634
UNDERFLOW_LOG2 = -160.0
VMEM_LIMIT = 56 * 1024 * 1024

_COLS = np.cumsum([0, 512, 512, 1024, 1024, 512, 512, 512, 1024, 1024])
C_RQ, C_RK, C_RV, C_RG, C_SQ, C_SK, C_SV, C_GA, C_GB, C_END = (int(c) for c in _COLS)

F32 = jnp.float32
BF16 = jnp.bfloat16


def _dot(a, b):
    return jnp.dot(a, b, preferred_element_type=F32)


def _dot_nt(a, b):
    return lax.dot_general(a, b, (((1,), (1,)), ((), ())), preferred_element_type=F32)


def _rmsnorm(x, g):
    r = lax.rsqrt(jnp.mean(x * x, axis=-1, keepdims=True) + EPS)
    return (x * r) * g


def _sigmoid(x):
    return 1.0 / (1.0 + jnp.exp(-x))


def _params(*sem):
    return pltpu.CompilerParams(dimension_semantics=sem, vmem_limit_bytes=VMEM_LIMIT)


def _resident_spec(block, index_map):
    return pl.BlockSpec(block, index_map, pipeline_mode=pl.Buffered(1))


def _layer_spec(layer, shape, col_block=0):
    return _resident_spec((None,) + shape, lambda *_: (layer, 0, col_block))


def _ffn_kernel(x_ref, g_ref, wgu_ref, wd_ref, gf_ref, *rest, final_norm):
    o_ref, gate_ref = rest[-2:]
    x = x_ref[...]
    u = _rmsnorm(x, g_ref[...]).astype(BF16)
    for c in range(0, D_FF, FF_CHUNK):
        a = _dot(u, wgu_ref[:, c:c + FF_CHUNK].astype(BF16))
        b = _dot(u, wgu_ref[:, D_FF + c:D_FF + c + FF_CHUNK].astype(BF16))
        gate_ref[:, c:c + FF_CHUNK] = (0.5 * a * _sigmoid(a) * b).astype(BF16)
    y = x + _dot(gate_ref[...], wd_ref[...].astype(BF16))
    if final_norm:
        y = _rmsnorm(y, gf_ref[...])
    o_ref[...] = y


def _ffn(h, layer, g, w_gu, w_down, gf, *, final_norm, tile, out_rows=None, out_offset=0, filled=()):
    out_rows = h.shape[0] if out_rows is None else out_rows
    return pl.pallas_call(
        functools.partial(_ffn_kernel, final_norm=final_norm),
        grid=(h.shape[0] // tile,),
        in_specs=[
            pl.BlockSpec((tile, D_MODEL), lambda i: (i, 0)),
            _layer_spec(layer, (1, D_MODEL)),
            _layer_spec(layer, (D_MODEL, 2 * D_FF)),
            _layer_spec(layer, (D_FF, D_MODEL)),
            _resident_spec((1, D_MODEL), lambda i: (0, 0)),
        ] + [pl.BlockSpec(memory_space=pl.ANY)] * len(filled),
        out_specs=pl.BlockSpec((tile, D_MODEL), lambda i: (i + out_offset, 0)),
        out_shape=jax.ShapeDtypeStruct((out_rows, D_MODEL), F32),
        input_output_aliases={5: 0} if filled else {},
        scratch_shapes=[pltpu.VMEM((tile, D_FF), BF16)],
        compiler_params=_params("parallel"),
        name="ffn_final" if final_norm else "ffn",
    )(h, g, w_gu, w_down, gf, *filled)


def _proj_kernel(x_ref, g_ref, wr_ref, ws_ref, cos_ref, sin_ref,
                 rq_ref, rk_ref, rv_ref, sq_ref, sk_ref, sv_ref):
    u = _rmsnorm(x_ref[...], g_ref[...]).astype(BF16)
    qk_w = RET_HEADS * RET_DK
    cos = jnp.concatenate([cos_ref[...]] * (qk_w // LANES), axis=1)
    sin = jnp.concatenate([sin_ref[...]] * (qk_w // LANES), axis=1)
    lane = lax.broadcasted_iota(jnp.int32, (x_ref.shape[0], qk_w), 1)
    first_half = (lane % RET_DK) < (RET_DK // 2)

    def rotary(t):
        partner = jnp.where(first_half, pltpu.roll(t, qk_w - RET_DK // 2, 1),
                            pltpu.roll(t, RET_DK // 2, 1))
        return t * cos + partner * sin

    def proj(w_ref, lo, hi):
        return _dot(u, w_ref[:, lo:hi].astype(BF16))

    rq_ref[...] = rotary(proj(wr_ref, C_RQ, C_RK)).astype(BF16)
    rk_ref[...] = (rotary(proj(wr_ref, C_RK, C_RV)) * RET_DK ** -0.5).astype(BF16)
    rv_ref[...] = proj(wr_ref, C_RV, C_RG).astype(BF16)
    sq_ref[...] = (proj(ws_ref, 0, C_SK - C_SQ) * (-LOG2E * SB_DH ** -0.5)).astype(BF16)
    sk_ref[...] = proj(ws_ref, C_SK - C_SQ, C_SV - C_SQ).astype(BF16)
    sv_ref[...] = proj(ws_ref, C_SV - C_SQ, C_GA - C_SQ).astype(BF16)


def _proj(h, layer, g, w_in, cos_t, sin_t, *, tile):
    rows = h.shape[0]

    def row_spec(width):
        return pl.BlockSpec((tile, width), lambda i: (i, 0))

    tab_spec = row_spec(LANES)
    widths = (C_RK - C_RQ, C_RV - C_RK, C_RG - C_RV, C_SK - C_SQ, C_SV - C_SK, C_GA - C_SV)
    sb_cols = C_GA - C_SQ
    assert C_SQ % sb_cols == 0
    return pl.pallas_call(
        _proj_kernel,
        grid=(rows // tile,),
        in_specs=[row_spec(D_MODEL), _layer_spec(layer, (1, D_MODEL)),
                  _layer_spec(layer, (D_MODEL, C_RG)),
                  _layer_spec(layer, (D_MODEL, sb_cols), col_block=C_SQ // sb_cols),
                  tab_spec, tab_spec],
        out_specs=[row_spec(w) for w in widths],
        out_shape=[jax.ShapeDtypeStruct((rows, w), BF16) for w in widths],
        compiler_params=_params("parallel"),
        name="mixer_proj",
    )(h, g, w_in, w_in, cos_t, sin_t)


def _meta_chunk(m_ref):
    m = m_ref[...]
    return jnp.concatenate([jnp.zeros((PAD, m.shape[1]), m.dtype), m], axis=0)


def _ret_kernel(lg_ref, *refs, meta_only):
    o_ref, state_ref, dec_ref, kdec_ref, qdec_ref, sdec_ref = refs[-6:]
    c = pl.program_id(1)
    pairs = RET_HEADS // 2

    @pl.when(c == 0)
    def _init():
        t = lax.broadcasted_iota(jnp.int32, (BLK, BLK), 0)
        s = lax.broadcasted_iota(jnp.int32, (BLK, BLK), 1)
        diff = (t - s).astype(F32)
        lane = lax.broadcasted_iota(jnp.int32, (BLK, LANES), 1)
        pos = lax.broadcasted_iota(jnp.int32, (BLK, LANES), 0).astype(F32)
        col = lax.broadcasted_iota(jnp.int32, (BLK, 2 * RET_DV), 1)
        posv = lax.broadcasted_iota(jnp.int32, (BLK, 2 * RET_DV), 0).astype(F32)
        srow = lax.broadcasted_iota(jnp.int32, (LANES, 2 * RET_DV), 0)
        scol = lax.broadcasted_iota(jnp.int32, (LANES, 2 * RET_DV), 1)
        own = (srow // RET_DK) == (scol // RET_DV)
        for p in range(pairs):
            lg = lg_ref[p]
            for i in range(2):
                dec_ref[2 * p + i] = jnp.where(
                    diff >= 0, jnp.exp(lg[i:i + 1, :] * jnp.maximum(diff, 0.0)), 0.0)
            lg_k = jnp.where(lane < RET_DK, lg[0:1, :LANES], lg[1:2, :LANES])
            kdec_ref[p] = jnp.exp(lg_k * (BLK - 1.0 - pos))
            lg_v = jnp.where(col < RET_DV, lg[0:1, :], lg[1:2, :])
            qdec_ref[p] = jnp.exp(lg_v * (posv + 1.0))
            lg_s = jnp.where(scol < RET_DV, lg[0:1, :], lg[1:2, :])
            sdec_ref[2 * p] = jnp.where(own, jnp.exp(lg_s * float(BLK)), 0.0)
            sdec_ref[2 * p + 1] = jnp.where(own, 1.0, 0.0)
        state_ref[...] = jnp.zeros_like(state_ref)

    def decayed_keys_t(k_pair, p):
        return (k_pair.astype(F32) * kdec_ref[p]).T.astype(BF16)

    def chunk(q_all, k_all, v_all, out_rows=slice(None)):
        lane = lax.broadcasted_iota(jnp.int32, (BLK, LANES), 1)
        qs, ks, crosses, scores, outs = {}, {}, {}, {}, {}

        def pair_start(p):
            qs[p] = q_all[:, p * LANES:(p + 1) * LANES]
            ks[p] = k_all[:, p * LANES:(p + 1) * LANES]
            crosses[p] = _dot(qs[p], state_ref[p].astype(BF16)) * qdec_ref[p]

        def head_scores(head):
            p, i = divmod(head, 2)
            in_head = (lane < RET_DK) if i == 0 else (lane >= RET_DK)
            qi = jnp.where(in_head, qs[p], jnp.zeros_like(qs[p]))
            scores[head] = (_dot_nt(qi, ks[p]) * dec_ref[head]).astype(BF16)

        def head_values(head):
            p, i = divmod(head, 2)
            o = _dot(scores.pop(head), v_all[:, head * RET_DV:(head + 1) * RET_DV])
            outs[head] = o + crosses[p][:, i * RET_DV:(i + 1) * RET_DV]

        def head_norm(head):
            o = outs.pop(head)
            mu = jnp.mean(o, axis=-1, keepdims=True)
            d = o - mu
            var = jnp.mean(d * d, axis=-1, keepdims=True)
            normed = (d * lax.rsqrt(var + EPS)).astype(BF16)
            o_ref[out_rows, head * RET_DV:(head + 1) * RET_DV] = normed[PAD:] if meta_only else normed

        def pair_state(p):
            v = v_all[:, 2 * p * RET_DV:2 * (p + 1) * RET_DV]
            state_ref[p] = (state_ref[p] * sdec_ref[2 * p]
                            + _dot(decayed_keys_t(ks[p], p), v) * sdec_ref[2 * p + 1])

        for t in range(RET_HEADS + 2):
            if t < RET_HEADS:
                if t % 2 == 0:
                    pair_start(t // 2)
                head_scores(t)
            if 0 <= t - 1 < RET_HEADS:
                head_values(t - 1)
            if 0 <= t - 2 < RET_HEADS:
                head_norm(t - 2)
                if (t - 2) % 2 == 1:
                    pair_state((t - 2) // 2)

    if meta_only:
        chunk(*(_meta_chunk(r) for r in refs[:3]))
    else:
        @pl.when(c == 0)
        def _replay_meta_chunk():
            k_m, v_m = _meta_chunk(refs[3]), _meta_chunk(refs[4])
            for p in range(pairs):
                kd_t = decayed_keys_t(k_m[:, p * LANES:(p + 1) * LANES], p)
                v = v_m[:, 2 * p * RET_DV:2 * (p + 1) * RET_DV]
                state_ref[p] = _dot(kd_t, v) * sdec_ref[2 * p + 1]
            o_ref[...] = jnp.zeros_like(o_ref)

        @pl.when(c > 0)
        def _real_chunks():
            for i in range(refs[0].shape[0] // BLK):
                rows = slice(i * BLK, (i + 1) * BLK)
                chunk(refs[0][rows, :], refs[1][rows, :], refs[2][rows, :], rows)


def _retention(lg_tab, rq, rk, rv, batch):
    rows = rq.shape[0]
    pairs = RET_HEADS // 2
    qk_w, v_w = RET_HEADS * RET_DK, RET_HEADS * RET_DV
    real_rows = rows - N_META
    per_seq = real_rows // BLK // batch
    meta_rows = lambda *_: (real_rows // N_META, 0)
    meta_specs = [pl.BlockSpec((N_META, w), meta_rows) for w in (qk_w, qk_w, v_w)]
    lg_spec = pl.BlockSpec((pairs, 2, BLK), lambda b, c: (0, 0, 0))
    scratch = [
        pltpu.VMEM((pairs, LANES, 2 * RET_DV), F32),
        pltpu.VMEM((RET_HEADS, BLK, BLK), F32),
        pltpu.VMEM((pairs, BLK, LANES), F32),
        pltpu.VMEM((pairs, BLK, 2 * RET_DV), F32),
        pltpu.VMEM((2 * pairs, LANES, 2 * RET_DV), F32),
    ]
    out_shape = jax.ShapeDtypeStruct((rows, v_w), BF16)
    out = pl.pallas_call(
        functools.partial(_ret_kernel, meta_only=True),
        grid=(1, 1),
        in_specs=[lg_spec] + meta_specs,
        out_specs=pl.BlockSpec((N_META, v_w), meta_rows),
        out_shape=out_shape,
        scratch_shapes=scratch,
        compiler_params=_params("arbitrary", "arbitrary"),
        name="retention_meta",
    )(lg_tab, rq, rk, rv)

    assert per_seq % RET_CHUNKS == 0
    steps = per_seq // RET_CHUNKS
    step_rows = RET_CHUNKS * BLK

    def real_rows_of(b, c):
        return b * steps + jnp.maximum(c - 1, 0), 0

    return pl.pallas_call(
        functools.partial(_ret_kernel, meta_only=False),
        grid=(batch, steps + 1),
        in_specs=[lg_spec] + [pl.BlockSpec((step_rows, w), real_rows_of) for w in (qk_w, qk_w, v_w)]
        + meta_specs[1:] + [pl.BlockSpec(memory_space=pl.ANY)],
        out_specs=pl.BlockSpec((step_rows, v_w), real_rows_of),
        out_shape=out_shape,
        input_output_aliases={6: 0},
        scratch_shapes=scratch,
        compiler_params=_params("parallel", "arbitrary"),
        name="retention",
    )(lg_tab, rq, rk, rv, rk, rv, out)


def _sb_kernel(q_ref, km_ref, vm_ref, *rest, meta_only):
    o_ref, k_ref, v_ref, acc_ref, carry_ref = rest[-5:]
    step_id = pl.program_id(1)

    @pl.when(step_id == 0)
    def _assemble_sequence():
        k_ref[0:BLK] = _meta_chunk(km_ref)
        v_ref[0:BLK] = _meta_chunk(vm_ref)
        if not meta_only:
            k_ref[BLK:] = rest[0][...]
            v_ref[BLK:] = rest[1][...]

    pairs = SB_HEADS // 2
    lane = lax.broadcasted_iota(jnp.int32, (BLK, LANES), 1)
    q_all = _meta_chunk(q_ref) if meta_only else q_ref[...]
    q_heads = []
    for qi in range(q_all.shape[0] // BLK):
        for p in range(pairs):
            q = q_all[qi * BLK:(qi + 1) * BLK, p * LANES:(p + 1) * LANES]
            zero = jnp.zeros_like(q)
            q_heads += [jnp.where(lane < SB_DH, q, zero), jnp.where(lane >= SB_DH, q, zero)]
    j = lax.broadcasted_iota(jnp.int32, (BLK, BLK), 0)
    c = lax.broadcasted_iota(jnp.int32, (BLK, BLK), 1)
    suffix = jnp.where(j >= c, 1.0, 0.0).astype(BF16)
    key_col = lax.broadcasted_iota(jnp.int32, (1, BLK), 1)

    def step(blocks, first):
        chains = [(qi * SB_HEADS + h, kb, bias, h) for qi, kb, bias in blocks for h in range(SB_HEADS)]
        slots = sorted({slot for slot, _, _, _ in chains})
        ys, logs, totals, sums = {}, {}, {}, {}
        carries = {slot: None if first else carry_ref[slot] for slot in slots}
        accs = {slot: None for slot in slots}

        def soft_log(y):
            return jnp.minimum(y, 0.0) - jnp.log2(1.0 + jnp.exp2(-jnp.abs(y)))

        def rows_of(kb):
            start = kb * BLK
            return pl.ds(start if isinstance(start, int) else pl.multiple_of(start, BLK), BLK)

        def front(n):
            slot, kb, bias, h = chains[n]
            cols = slice((h // 2) * LANES, (h // 2 + 1) * LANES)
            y = _dot_nt(q_heads[slot], k_ref[rows_of(kb), cols])
            if bias is not None:
                y = y + bias
            log_1mb = soft_log(y)
            ys[n], logs[n] = y, log_1mb.astype(BF16)
            totals[n] = jnp.sum(log_1mb, axis=1, keepdims=True)

        def back(n):
            slot, kb, _, h = chains[n]
            cols = slice((h // 2) * LANES, (h // 2 + 1) * LANES)
            incl = sums.pop(n) if carries[slot] is None else sums.pop(n) + carries[slot]
            a = jnp.exp2(incl - ys.pop(n)).astype(BF16)
            pv = _dot(a, v_ref[rows_of(kb), cols])
            carries[slot] = totals[n] if carries[slot] is None else carries[slot] + totals[n]
            accs[slot] = pv if accs[slot] is None else accs[slot] + pv

        for t in range(len(chains) + 2):
            if t < len(chains):
                front(t)
            if 0 <= t - 1 < len(chains):
                sums[t - 1] = _dot(logs.pop(t - 1), suffix)
            if 0 <= t - 2 < len(chains):
                back(t - 2)
        for slot in slots:
            carry_ref[slot] = carries[slot]
            if first:
                acc_ref[slot] = accs[slot]
            else:
                acc_ref[slot] += accs[slot]

    def valid_bias(kb):
        return jnp.where(kb * BLK + key_col < PAD, MASK_BIG, 0.0).astype(F32)

    r = lax.broadcasted_iota(jnp.int32, (BLK, BLK), 0)
    s = lax.broadcasted_iota(jnp.int32, (BLK, BLK), 1)
    causal = jnp.where(s < r, 0.0, MASK_BIG).astype(F32)

    if meta_only:
        step([(0, 0, causal + valid_bias(0))], True)
    else:
        first_q = QUERY_BLOCKS * step_id + 1
        step([(0, first_q, causal), (0, first_q - 1, valid_bias(first_q - 1)),
              (1, first_q + 1, causal), (1, first_q, None)], True)

        for qi in range(QUERY_BLOCKS):
            def decayed(qi=qi):
                top = carry_ref[qi * SB_HEADS]
                for h in range(1, SB_HEADS):
                    top = jnp.maximum(top, carry_ref[qi * SB_HEADS + h])
                return jnp.max(top) < UNDERFLOW_LOG2

            def live(state):
                kb, done = state
                return jnp.logical_and(kb >= 0, jnp.logical_not(done))

            def body(state, qi=qi, decayed=decayed):
                kb, _ = state
                step([(qi, kb, valid_bias(kb))], False)
                return kb - 1, decayed()

            lax.while_loop(live, body, (first_q + qi - 2, decayed()))
    for qi in range(q_all.shape[0] // BLK):
        for p in range(pairs):
            slot = qi * SB_HEADS + 2 * p
            out = jnp.where(lane < SB_DH, acc_ref[slot], acc_ref[slot + 1]).astype(BF16)
            rows = slice(None) if meta_only else slice(qi * BLK, (qi + 1) * BLK)
            o_ref[rows, p * LANES:(p + 1) * LANES] = out[PAD:] if meta_only else out


def _stick_breaking(sq, sk, sv, batch):
    rows, width = sq.shape
    assert PAD <= BLK and width == SB_HEADS * SB_DH
    real_rows = rows - N_META
    per_seq = real_rows // BLK // batch
    seq = per_seq * BLK
    meta_spec = pl.BlockSpec((N_META, width), lambda b, i: (real_rows // N_META, 0))
    out_shape = jax.ShapeDtypeStruct((rows, width), BF16)

    def scratch(key_rows, query_blocks):
        slots = query_blocks * SB_HEADS
        return [pltpu.VMEM((key_rows, width), BF16), pltpu.VMEM((key_rows, width), BF16),
                pltpu.VMEM((slots, BLK, LANES), F32), pltpu.VMEM((slots, BLK, 1), F32)]

    out = pl.pallas_call(
        functools.partial(_sb_kernel, meta_only=True),
        grid=(1, 1),
        in_specs=[meta_spec] * 3,
        out_specs=meta_spec,
        out_shape=out_shape,
        scratch_shapes=scratch(BLK, 1),
        compiler_params=_params("arbitrary", "arbitrary"),
        name="stick_breaking_meta",
    )(sq, sk, sv)
    assert per_seq % QUERY_BLOCKS == 0
    steps = per_seq // QUERY_BLOCKS
    q_spec = pl.BlockSpec((QUERY_BLOCKS * BLK, width), lambda b, i: (b * steps + i, 0))
    real_spec = pl.BlockSpec((seq, width), lambda b, i: (b, 0))
    return pl.pallas_call(
        functools.partial(_sb_kernel, meta_only=False),
        grid=(batch, steps),
        in_specs=[q_spec, meta_spec, meta_spec, real_spec, real_spec, pl.BlockSpec(memory_space=pl.ANY)],
        out_specs=q_spec,
        out_shape=out_shape,
        input_output_aliases={5: 0},
        scratch_shapes=scratch(seq + BLK, QUERY_BLOCKS),
        compiler_params=_params("parallel", "arbitrary"),
        name="stick_breaking",
    )(sq, sk, sv, sk, sv, out)


def _merge_kernel(x_ref, g_ref, wrg_ref, wga0_ref, wga1_ref, wgb0_ref, wgb1_ref, gn_ref,
                  or_ref, os_ref, wr_ref, ws_ref, wo_ref, o_ref):
    x = x_ref[...]
    u = _rmsnorm(x, g_ref[...]).astype(BF16)

    def gate(*w_refs):
        return jnp.concatenate([_dot(u, w[...].astype(BF16)) for w in w_refs], axis=1)

    rg = gate(wrg_ref)
    y_sb = _dot(os_ref[...], ws_ref[...].astype(BF16))
    gb = _sigmoid(gate(wgb0_ref, wgb1_ref)) * y_sb
    ga = _sigmoid(gate(wga0_ref, wga1_ref))
    gated = (rg * _sigmoid(rg)) * (or_ref[...].astype(F32) * gn_ref[...])
    y_ret = _dot(gated.astype(BF16), wr_ref[...].astype(BF16))
    y = ga * y_ret + gb
    o_ref[...] = x + _dot(y.astype(BF16), wo_ref[...].astype(BF16))


def _merge(h, layer, g, w_in, gn, o_r, o_s, w_ret, w_sb, w_out, *, tile, out_rows):
    def in_spec(width):
        return pl.BlockSpec((tile, width), lambda i: (i, 0))

    half = D_MODEL // 2
    assert C_RG % D_MODEL == 0 and C_GA % half == 0
    ret_w, sb_w = RET_HEADS * RET_DV, SB_HEADS * SB_DH
    return pl.pallas_call(
        _merge_kernel,
        grid=(out_rows // tile,),
        in_specs=[
            in_spec(D_MODEL), _layer_spec(layer, (1, D_MODEL)),
            _layer_spec(layer, (D_MODEL, D_MODEL), col_block=C_RG // D_MODEL),
            _layer_spec(layer, (D_MODEL, half), col_block=C_GA // half),
            _layer_spec(layer, (D_MODEL, half), col_block=C_GA // half + 1),
            _layer_spec(layer, (D_MODEL, half), col_block=C_GB // half),
            _layer_spec(layer, (D_MODEL, half), col_block=C_GB // half + 1),
            _layer_spec(layer, (1, ret_w)), in_spec(ret_w), in_spec(sb_w),
            _layer_spec(layer, (ret_w, D_MODEL)), _layer_spec(layer, (sb_w, D_MODEL)),
            _layer_spec(layer, (D_MODEL, D_MODEL)),
        ],
        out_specs=pl.BlockSpec((tile, D_MODEL), lambda i: (i, 0)),
        out_shape=jax.ShapeDtypeStruct((out_rows, D_MODEL), F32),
        compiler_params=_params("parallel"),
        name="mixer_merge",
    )(h, g, w_in, w_in, w_in, w_in, w_in, gn, o_r, o_s, w_ret, w_sb, w_out)


def kernel(x, meta, ffn1_norm, ffn1_w_gu, ffn1_w_down, mix_norm, w_in, ret_gn, w_ret_proj,
           w_sb_proj, w_out, ffn2_norm, ffn2_w_gu, ffn2_w_down, final_norm):
    batch, seq, d = x.shape
    real_rows = batch * seq
    rows = real_rows + N_META
    assert d == D_MODEL and seq % BLK == 0 and real_rows % ROW_TILE == 0 and rows % MID_TILE == 0

    f32 = np.float32
    pos = np.concatenate([np.tile(np.arange(seq) + N_META, batch), np.arange(N_META)]).astype(f32)
    freqs = f32(ROPE_BASE) ** (-np.arange(0, RET_DK, 2, dtype=f32) / f32(RET_DK))
    ang = pos[:, None] * freqs[None, :]
    cos_h, sin_h = np.cos(ang).astype(f32), np.sin(ang).astype(f32)
    cos_t = jnp.asarray(np.concatenate([cos_h, cos_h] * 2, axis=1))
    sin_t = jnp.asarray(np.concatenate([-sin_h, sin_h] * 2, axis=1))
    log_gamma = np.log(f32(1.0) - f32(2.0) ** (-5.0 - np.arange(RET_HEADS, dtype=f32))).astype(f32)
    lg_tab = jnp.asarray(np.broadcast_to(log_gamma.reshape(RET_HEADS // 2, 2, 1), (RET_HEADS // 2, 2, BLK)))

    gains = lambda t: t.reshape(t.shape[0], 1, t.shape[1])
    final_gain = final_norm.reshape(1, d)
    for l in range(DEPTH):
        last = l == DEPTH - 1
        ffn1 = functools.partial(_ffn, layer=l, g=gains(ffn1_norm), w_gu=ffn1_w_gu, w_down=ffn1_w_down,
                                 gf=final_gain, final_norm=False)
        if l == 0:
            h = ffn1(x.reshape(real_rows, d), tile=ROW_TILE, out_rows=rows)
            h = ffn1(meta.astype(x.dtype), tile=N_META, out_rows=rows,
                     out_offset=real_rows // N_META, filled=(h,))
        else:
            h = ffn1(h, tile=MID_TILE)
        rq, rk, rv, sq, sk, sv = _proj(h, l, gains(mix_norm), w_in, cos_t, sin_t, tile=MID_TILE)
        o_r = _retention(lg_tab, rq, rk, rv, batch)
        o_s = _stick_breaking(sq, sk, sv, batch)
        tile, out_rows = (ROW_TILE, real_rows) if last else (MID_TILE, rows)
        h = _merge(h, l, gains(mix_norm), w_in, gains(ret_gn), o_r, o_s, w_ret_proj, w_sb_proj, w_out,
                   tile=tile, out_rows=out_rows)
        h = _ffn(h, l, gains(ffn2_norm), ffn2_w_gu, ffn2_w_down, final_gain, final_norm=last, tile=tile)
    return h.reshape(batch, seq, d)
```

```python
import functools

import jax
import jax.numpy as jnp
import numpy as np
from jax import lax
from jax.experimental import pallas as pl
from jax.experimental.pallas import tpu as pltpu

D_MODEL = 1024
DEPTH = 2
N_META = 16
RET_HEADS = 8
RET_DK = 64
RET_DV = 128
SB_HEADS = 8
SB_DH = 64
D_FF = 2816
ROPE_BASE = 10000.0
EPS = 1e-6

LANES = 128
BLK = 256
PAD = BLK - N_META
FF_CHUNK = 256
ROW_TILE = 512
MID_TILE = 656
QUERY_BLOCKS = 4
RET_CHUNKS = 2
MASK_BIG = 1e30
LOG2E = 1.4426950408889634
UNDERFLOW_LOG2 = -160.0
VMEM_LIMIT = 56 * 1024 * 1024

_COLS = np.cumsum([0, 512, 512, 1024, 1024, 512, 512, 512, 1024, 1024])
C_RQ, C_RK, C_RV, C_RG, C_SQ, C_SK, C_SV, C_GA, C_GB, C_END = (int(c) for c in _COLS)

F32 = jnp.float32
BF16 = jnp.bfloat16


def _dot(a, b):
    return jnp.dot(a, b, preferred_element_type=F32)


def _dot_nt(a, b):
    return lax.dot_general(a, b, (((1,), (1,)), ((), ())), preferred_element_type=F32)


def _rmsnorm(x, g):
    r = lax.rsqrt(jnp.mean(x * x, axis=-1, keepdims=True) + EPS)
    return (x * r) * g


def _sigmoid(x):
    return 1.0 / (1.0 + jnp.exp(-x))


def _params(*sem):
    return pltpu.CompilerParams(dimension_semantics=sem, vmem_limit_bytes=VMEM_LIMIT)


def _resident_spec(block, index_map):
    return pl.BlockSpec(block, index_map, pipeline_mode=pl.Buffered(1))


def _layer_spec(layer, shape, col_block=0):
    return _resident_spec((None,) + shape, lambda *_: (layer, 0, col_block))


def _ffn_kernel(x_ref, g_ref, wgu_ref, wd_ref, gf_ref, *rest, final_norm):
    o_ref, gate_ref = rest[-2:]
    x = x_ref[...]
    u = _rmsnorm(x, g_ref[...]).astype(BF16)
    for c in range(0, D_FF, FF_CHUNK):
        a = _dot(u, wgu_ref[:, c:c + FF_CHUNK].astype(BF16))
        b = _dot(u, wgu_ref[:, D_FF + c:D_FF + c + FF_CHUNK].astype(BF16))
        gate_ref[:, c:c + FF_CHUNK] = (0.5 * a * _sigmoid(a) * b).astype(BF16)
    y = x + _dot(gate_ref[...], wd_ref[...].astype(BF16))
    if final_norm:
        y = _rmsnorm(y, gf_ref[...])
    o_ref[...] = y


def _ffn(h, layer, g, w_gu, w_down, gf, *, final_norm, tile, out_rows=None, out_offset=0, filled=()):
    out_rows = h.shape[0] if out_rows is None else out_rows
    return pl.pallas_call(
        functools.partial(_ffn_kernel, final_norm=final_norm),
        grid=(h.shape[0] // tile,),
        in_specs=[
            pl.BlockSpec((tile, D_MODEL), lambda i: (i, 0)),
            _layer_spec(layer, (1, D_MODEL)),
            _layer_spec(layer, (D_MODEL, 2 * D_FF)),
            _layer_spec(layer, (D_FF, D_MODEL)),
            _resident_spec((1, D_MODEL), lambda i: (0, 0)),
        ] + [pl.BlockSpec(memory_space=pl.ANY)] * len(filled),
        out_specs=pl.BlockSpec((tile, D_MODEL), lambda i: (i + out_offset, 0)),
        out_shape=jax.ShapeDtypeStruct((out_rows, D_MODEL), F32),
        input_output_aliases={5: 0} if filled else {},
        scratch_shapes=[pltpu.VMEM((tile, D_FF), BF16)],
        compiler_params=_params("parallel"),
        name="ffn_final" if final_norm else "ffn",
    )(h, g, w_gu, w_down, gf, *filled)


def _proj_kernel(x_ref, g_ref, wr_ref, ws_ref, cos_ref, sin_ref,
                 rq_ref, rk_ref, rv_ref, sq_ref, sk_ref, sv_ref):
    u = _rmsnorm(x_ref[...], g_ref[...]).astype(BF16)
    qk_w = RET_HEADS * RET_DK
    cos = jnp.concatenate([cos_ref[...]] * (qk_w // LANES), axis=1)
    sin = jnp.concatenate([sin_ref[...]] * (qk_w // LANES), axis=1)
    lane = lax.broadcasted_iota(jnp.int32, (x_ref.shape[0], qk_w), 1)
    first_half = (lane % RET_DK) < (RET_DK // 2)

    def rotary(t):
        partner = jnp.where(first_half, pltpu.roll(t, qk_w - RET_DK // 2, 1),
                            pltpu.roll(t, RET_DK // 2, 1))
        return t * cos + partner * sin

    def proj(w_ref, lo, hi):
        return _dot(u, w_ref[:, lo:hi].astype(BF16))

    rq_ref[...] = rotary(proj(wr_ref, C_RQ, C_RK)).astype(BF16)
    rk_ref[...] = (rotary(proj(wr_ref, C_RK, C_RV)) * RET_DK ** -0.5).astype(BF16)
    rv_ref[...] = proj(wr_ref, C_RV, C_RG).astype(BF16)
    sq_ref[...] = (proj(ws_ref, 0, C_SK - C_SQ) * (-LOG2E * SB_DH ** -0.5)).astype(BF16)
    sk_ref[...] = proj(ws_ref, C_SK - C_SQ, C_SV - C_SQ).astype(BF16)
    sv_ref[...] = proj(ws_ref, C_SV - C_SQ, C_GA - C_SQ).astype(BF16)


def _proj(h, layer, g, w_in, cos_t, sin_t, *, tile):
    rows = h.shape[0]

    def row_spec(width):
        return pl.BlockSpec((tile, width), lambda i: (i, 0))

    tab_spec = row_spec(LANES)
    widths = (C_RK - C_RQ, C_RV - C_RK, C_RG - C_RV, C_SK - C_SQ, C_SV - C_SK, C_GA - C_SV)
    sb_cols = C_GA - C_SQ
    assert C_SQ % sb_cols == 0
    return pl.pallas_call(
        _proj_kernel,
        grid=(rows // tile,),
        in_specs=[row_spec(D_MODEL), _layer_spec(layer, (1, D_MODEL)),
                  _layer_spec(layer, (D_MODEL, C_RG)),
                  _layer_spec(layer, (D_MODEL, sb_cols), col_block=C_SQ // sb_cols),
                  tab_spec, tab_spec],
        out_specs=[row_spec(w) for w in widths],
        out_shape=[jax.ShapeDtypeStruct((rows, w), BF16) for w in widths],
        compiler_params=_params("parallel"),
        name="mixer_proj",
    )(h, g, w_in, w_in, cos_t, sin_t)


def _meta_chunk(m_ref):
    m = m_ref[...]
    return jnp.concatenate([jnp.zeros((PAD, m.shape[1]), m.dtype), m], axis=0)


def _ret_kernel(lg_ref, *refs, meta_only):
    o_ref, state_ref, dec_ref, kdec_ref, qdec_ref, sdec_ref = refs[-6:]
    c = pl.program_id(1)
    pairs = RET_HEADS // 2

    @pl.when(c == 0)
    def _init():
        t = lax.broadcasted_iota(jnp.int32, (BLK, BLK), 0)
        s = lax.broadcasted_iota(jnp.int32, (BLK, BLK), 1)
        diff = (t - s).astype(F32)
        lane = lax.broadcasted_iota(jnp.int32, (BLK, LANES), 1)
        pos = lax.broadcasted_iota(jnp.int32, (BLK, LANES), 0).astype(F32)
        col = lax.broadcasted_iota(jnp.int32, (BLK, 2 * RET_DV), 1)
        posv = lax.broadcasted_iota(jnp.int32, (BLK, 2 * RET_DV), 0).astype(F32)
        srow = lax.broadcasted_iota(jnp.int32, (LANES, 2 * RET_DV), 0)
        scol = lax.broadcasted_iota(jnp.int32, (LANES, 2 * RET_DV), 1)
        own = (srow // RET_DK) == (scol // RET_DV)
        for p in range(pairs):
            lg = lg_ref[p]
            for i in range(2):
                dec_ref[2 * p + i] = jnp.where(
                    diff >= 0, jnp.exp(lg[i:i + 1, :] * jnp.maximum(diff, 0.0)), 0.0)
            lg_k = jnp.where(lane < RET_DK, lg[0:1, :LANES], lg[1:2, :LANES])
            kdec_ref[p] = jnp.exp(lg_k * (BLK - 1.0 - pos))
            lg_v = jnp.where(col < RET_DV, lg[0:1, :], lg[1:2, :])
            qdec_ref[p] = jnp.exp(lg_v * (posv + 1.0))
            lg_s = jnp.where(scol < RET_DV, lg[0:1, :], lg[1:2, :])
            sdec_ref[2 * p] = jnp.where(own, jnp.exp(lg_s * float(BLK)), 0.0)
            sdec_ref[2 * p + 1] = jnp.where(own, 1.0, 0.0)
        state_ref[...] = jnp.zeros_like(state_ref)

    def decayed_keys_t(k_pair, p):
        return (k_pair.astype(F32) * kdec_ref[p]).T.astype(BF16)

    def chunk(q_all, k_all, v_all, out_rows=slice(None)):
        lane = lax.broadcasted_iota(jnp.int32, (BLK, LANES), 1)
        qs, ks, crosses, scores, outs = {}, {}, {}, {}, {}

        def pair_start(p):
            qs[p] = q_all[:, p * LANES:(p + 1) * LANES]
            ks[p] = k_all[:, p * LANES:(p + 1) * LANES]
            crosses[p] = _dot(qs[p], state_ref[p].astype(BF16)) * qdec_ref[p]

        def head_scores(head):
            p, i = divmod(head, 2)
            in_head = (lane < RET_DK) if i == 0 else (lane >= RET_DK)
            qi = jnp.where(in_head, qs[p], jnp.zeros_like(qs[p]))
            scores[head] = (_dot_nt(qi, ks[p]) * dec_ref[head]).astype(BF16)

        def head_values(head):
            p, i = divmod(head, 2)
            o = _dot(scores.pop(head), v_all[:, head * RET_DV:(head + 1) * RET_DV])
            outs[head] = o + crosses[p][:, i * RET_DV:(i + 1) * RET_DV]

        def head_norm(head):
            o = outs.pop(head)
            mu = jnp.mean(o, axis=-1, keepdims=True)
            d = o - mu
            var = jnp.mean(d * d, axis=-1, keepdims=True)
            normed = (d * lax.rsqrt(var + EPS)).astype(BF16)
            o_ref[out_rows, head * RET_DV:(head + 1) * RET_DV] = normed[PAD:] if meta_only else normed

        def pair_state(p):
            v = v_all[:, 2 * p * RET_DV:2 * (p + 1) * RET_DV]
            state_ref[p] = (state_ref[p] * sdec_ref[2 * p]
                            + _dot(decayed_keys_t(ks[p], p), v) * sdec_ref[2 * p + 1])

        for t in range(RET_HEADS + 2):
            if t < RET_HEADS:
                if t % 2 == 0:
                    pair_start(t // 2)
                head_scores(t)
            if 0 <= t - 1 < RET_HEADS:
                head_values(t - 1)
            if 0 <= t - 2 < RET_HEADS:
                head_norm(t - 2)
                if (t - 2) % 2 == 1:
                    pair_state((t - 2) // 2)

    if meta_only:
        chunk(*(_meta_chunk(r) for r in refs[:3]))
    else:
        @pl.when(c == 0)
        def _replay_meta_chunk():
            k_m, v_m = _meta_chunk(refs[3]), _meta_chunk(refs[4])
            for p in range(pairs):
                kd_t = decayed_keys_t(k_m[:, p * LANES:(p + 1) * LANES], p)
                v = v_m[:, 2 * p * RET_DV:2 * (p + 1) * RET_DV]
                state_ref[p] = _dot(kd_t, v) * sdec_ref[2 * p + 1]
            o_ref[...] = jnp.zeros_like(o_ref)

        @pl.when(c > 0)
        def _real_chunks():
            for i in range(refs[0].shape[0] // BLK):
                rows = slice(i * BLK, (i + 1) * BLK)
                chunk(refs[0][rows, :], refs[1][rows, :], refs[2][rows, :], rows)


def _retention(lg_tab, rq, rk, rv, batch):
    rows = rq.shape[0]
    pairs = RET_HEADS // 2
    qk_w, v_w = RET_HEADS * RET_DK, RET_HEADS * RET_DV
    real_rows = rows - N_META
    per_seq = real_rows // BLK // batch
    meta_rows = lambda *_: (real_rows // N_META, 0)
    meta_specs = [pl.BlockSpec((N_META, w), meta_rows) for w in (qk_w, qk_w, v_w)]
    lg_spec = pl.BlockSpec((pairs, 2, BLK), lambda b, c: (0, 0, 0))
    scratch = [
        pltpu.VMEM((pairs, LANES, 2 * RET_DV), F32),
        pltpu.VMEM((RET_HEADS, BLK, BLK), F32),
        pltpu.VMEM((pairs, BLK, LANES), F32),
        pltpu.VMEM((pairs, BLK, 2 * RET_DV), F32),
        pltpu.VMEM((2 * pairs, LANES, 2 * RET_DV), F32),
    ]
    out_shape = jax.ShapeDtypeStruct((rows, v_w), BF16)
    out = pl.pallas_call(
        functools.partial(_ret_kernel, meta_only=True),
        grid=(1, 1),
        in_specs=[lg_spec] + meta_specs,
        out_specs=pl.BlockSpec((N_META, v_w), meta_rows),
        out_shape=out_shape,
        scratch_shapes=scratch,
        compiler_params=_params("arbitrary", "arbitrary"),
        name="retention_meta",
    )(lg_tab, rq, rk, rv)

    assert per_seq % RET_CHUNKS == 0
    steps = per_seq // RET_CHUNKS
    step_rows = RET_CHUNKS * BLK

    def real_rows_of(b, c):
        return b * steps + jnp.maximum(c - 1, 0), 0

    return pl.pallas_call(
        functools.partial(_ret_kernel, meta_only=False),
        grid=(batch, steps + 1),
        in_specs=[lg_spec] + [pl.BlockSpec((step_rows, w), real_rows_of) for w in (qk_w, qk_w, v_w)]
        + meta_specs[1:] + [pl.BlockSpec(memory_space=pl.ANY)],
        out_specs=pl.BlockSpec((step_rows, v_w), real_rows_of),
        out_shape=out_shape,
        input_output_aliases={6: 0},
        scratch_shapes=scratch,
        compiler_params=_params("parallel", "arbitrary"),
        name="retention",
    )(lg_tab, rq, rk, rv, rk, rv, out)


def _sb_kernel(q_ref, km_ref, vm_ref, *rest, meta_only):
    o_ref, k_ref, v_ref, acc_ref, carry_ref = rest[-5:]
    step_id = pl.program_id(1)

    @pl.when(step_id == 0)
    def _assemble_sequence():
        k_ref[0:BLK] = _meta_chunk(km_ref)
        v_ref[0:BLK] = _meta_chunk(vm_ref)
        if not meta_only:
            k_ref[BLK:] = rest[0][...]
            v_ref[BLK:] = rest[1][...]

    pairs = SB_HEADS // 2
    lane = lax.broadcasted_iota(jnp.int32, (BLK, LANES), 1)
    q_all = _meta_chunk(q_ref) if meta_only else q_ref[...]
    q_heads = []
    for qi in range(q_all.shape[0] // BLK):
        for p in range(pairs):
            q = q_all[qi * BLK:(qi + 1) * BLK, p * LANES:(p + 1) * LANES]
            zero = jnp.zeros_like(q)
            q_heads += [jnp.where(lane < SB_DH, q, zero), jnp.where(lane >= SB_DH, q, zero)]
    j = lax.broadcasted_iota(jnp.int32, (BLK, BLK), 0)
    c = lax.broadcasted_iota(jnp.int32, (BLK, BLK), 1)
    suffix = jnp.where(j >= c, 1.0, 0.0).astype(BF16)
    key_col = lax.broadcasted_iota(jnp.int32, (1, BLK), 1)

    def step(blocks, first):
        chains = [(qi * SB_HEADS + h, kb, bias, h) for qi, kb, bias in blocks for h in range(SB_HEADS)]
        slots = sorted({slot for slot, _, _, _ in chains})
        ys, logs, totals, sums = {}, {}, {}, {}
        carries = {slot: None if first else carry_ref[slot] for slot in slots}
        accs = {slot: None for slot in slots}

        def soft_log(y):
            return jnp.minimum(y, 0.0) - jnp.log2(1.0 + jnp.exp2(-jnp.abs(y)))

        def rows_of(kb):
            start = kb * BLK
            return pl.ds(start if isinstance(start, int) else pl.multiple_of(start, BLK), BLK)

        def front(n):
            slot, kb, bias, h = chains[n]
            cols = slice((h // 2) * LANES, (h // 2 + 1) * LANES)
            y = _dot_nt(q_heads[slot], k_ref[rows_of(kb), cols])
            if bias is not None:
                y = y + bias
            log_1mb = soft_log(y)
            ys[n], logs[n] = y, log_1mb.astype(BF16)
            totals[n] = jnp.sum(log_1mb, axis=1, keepdims=True)

        def back(n):
            slot, kb, _, h = chains[n]
            cols = slice((h // 2) * LANES, (h // 2 + 1) * LANES)
            incl = sums.pop(n) if carries[slot] is None else sums.pop(n) + carries[slot]
            a = jnp.exp2(incl - ys.pop(n)).astype(BF16)
            pv = _dot(a, v_ref[rows_of(kb), cols])
            carries[slot] = totals[n] if carries[slot] is None else carries[slot] + totals[n]
            accs[slot] = pv if accs[slot] is None else accs[slot] + pv

        for t in range(len(chains) + 2):
            if t < len(chains):
                front(t)
            if 0 <= t - 1 < len(chains):
                sums[t - 1] = _dot(logs.pop(t - 1), suffix)
            if 0 <= t - 2 < len(chains):
                back(t - 2)
        for slot in slots:
            carry_ref[slot] = carries[slot]
            if first:
                acc_ref[slot] = accs[slot]
            else:
                acc_ref[slot] += accs[slot]

    def valid_bias(kb):
        return jnp.where(kb * BLK + key_col < PAD, MASK_BIG, 0.0).astype(F32)

    r = lax.broadcasted_iota(jnp.int32, (BLK, BLK), 0)
    s = lax.broadcasted_iota(jnp.int32, (BLK, BLK), 1)
    causal = jnp.where(s < r, 0.0, MASK_BIG).astype(F32)

    if meta_only:
        step([(0, 0, causal + valid_bias(0))], True)
    else:
        first_q = QUERY_BLOCKS * step_id + 1
        blocks = []
        for qi in range(QUERY_BLOCKS):
            prev_bias = valid_bias(first_q - 1) if qi == 0 else None
            blocks += [(qi, first_q + qi, causal), (qi, first_q + qi - 1, prev_bias)]
        step(blocks, True)

        for qi in range(QUERY_BLOCKS):
            def decayed(qi=qi):
                top = carry_ref[qi * SB_HEADS]
                for h in range(1, SB_HEADS):
                    top = jnp.maximum(top, carry_ref[qi * SB_HEADS + h])
                return jnp.max(top) < UNDERFLOW_LOG2

            def live(state):
                kb, done = state
                return jnp.logical_and(kb >= 0, jnp.logical_not(done))

            def body(state, qi=qi, decayed=decayed):
                kb, _ = state
                step([(qi, kb, valid_bias(kb))], False)
                return kb - 1, decayed()

            lax.while_loop(live, body, (first_q + qi - 2, decayed()))
    for qi in range(q_all.shape[0] // BLK):
        for p in range(pairs):
            slot = qi * SB_HEADS + 2 * p
            out = jnp.where(lane < SB_DH, acc_ref[slot], acc_ref[slot + 1]).astype(BF16)
            rows = slice(None) if meta_only else slice(qi * BLK, (qi + 1) * BLK)
            o_ref[rows, p * LANES:(p + 1) * LANES] = out[PAD:] if meta_only else out


def _stick_breaking(sq, sk, sv, batch):
    rows, width = sq.shape
    assert PAD <= BLK and width == SB_HEADS * SB_DH
    real_rows = rows - N_META
    per_seq = real_rows // BLK // batch
    seq = per_seq * BLK
    meta_spec = pl.BlockSpec((N_META, width), lambda b, i: (real_rows // N_META, 0))
    out_shape = jax.ShapeDtypeStruct((rows, width), BF16)

    def scratch(key_rows, query_blocks):
        slots = query_blocks * SB_HEADS
        return [pltpu.VMEM((key_rows, width), BF16), pltpu.VMEM((key_rows, width), BF16),
                pltpu.VMEM((slots, BLK, LANES), F32), pltpu.VMEM((slots, BLK, 1), F32)]

    out = pl.pallas_call(
        functools.partial(_sb_kernel, meta_only=True),
        grid=(1, 1),
        in_specs=[meta_spec] * 3,
        out_specs=meta_spec,
        out_shape=out_shape,
        scratch_shapes=scratch(BLK, 1),
        compiler_params=_params("arbitrary", "arbitrary"),
        name="stick_breaking_meta",
    )(sq, sk, sv)
    assert per_seq % QUERY_BLOCKS == 0
    steps = per_seq // QUERY_BLOCKS
    q_spec = pl.BlockSpec((QUERY_BLOCKS * BLK, width), lambda b, i: (b * steps + i, 0))
    real_spec = pl.BlockSpec((seq, width), lambda b, i: (b, 0))
    return pl.pallas_call(
        functools.partial(_sb_kernel, meta_only=False),
        grid=(batch, steps),
        in_specs=[q_spec, meta_spec, meta_spec, real_spec, real_spec, pl.BlockSpec(memory_space=pl.ANY)],
        out_specs=q_spec,
        out_shape=out_shape,
        input_output_aliases={5: 0},
        scratch_shapes=scratch(seq + BLK, QUERY_BLOCKS),
        compiler_params=_params("parallel", "arbitrary"),
        name="stick_breaking",
    )(sq, sk, sv, sk, sv, out)


def _merge_kernel(x_ref, g_ref, wrg_ref, wga0_ref, wga1_ref, wgb0_ref, wgb1_ref, gn_ref,
                  or_ref, os_ref, wr_ref, ws_ref, wo_ref, o_ref):
    x = x_ref[...]
    u = _rmsnorm(x, g_ref[...]).astype(BF16)

    def gate(*w_refs):
        return jnp.concatenate([_dot(u, w[...].astype(BF16)) for w in w_refs], axis=1)

    rg = gate(wrg_ref)
    y_sb = _dot(os_ref[...], ws_ref[...].astype(BF16))
    gb = _sigmoid(gate(wgb0_ref, wgb1_ref)) * y_sb
    ga = _sigmoid(gate(wga0_ref, wga1_ref))
    gated = (rg * _sigmoid(rg)) * (or_ref[...].astype(F32) * gn_ref[...])
    y_ret = _dot(gated.astype(BF16), wr_ref[...].astype(BF16))
    y = ga * y_ret + gb
    o_ref[...] = x + _dot(y.astype(BF16), wo_ref[...].astype(BF16))


def _merge(h, layer, g, w_in, gn, o_r, o_s, w_ret, w_sb, w_out, *, tile, out_rows):
    def in_spec(width):
        return pl.BlockSpec((tile, width), lambda i: (i, 0))

    half = D_MODEL // 2
    assert C_RG % D_MODEL == 0 and C_GA % half == 0
    ret_w, sb_w = RET_HEADS * RET_DV, SB_HEADS * SB_DH
    return pl.pallas_call(
        _merge_kernel,
        grid=(out_rows // tile,),
        in_specs=[
            in_spec(D_MODEL), _layer_spec(layer, (1, D_MODEL)),
            _layer_spec(layer, (D_MODEL, D_MODEL), col_block=C_RG // D_MODEL),
            _layer_spec(layer, (D_MODEL, half), col_block=C_GA // half),
            _layer_spec(layer, (D_MODEL, half), col_block=C_GA // half + 1),
            _layer_spec(layer, (D_MODEL, half), col_block=C_GB // half),
            _layer_spec(layer, (D_MODEL, half), col_block=C_GB // half + 1),
            _layer_spec(layer, (1, ret_w)), in_spec(ret_w), in_spec(sb_w),
            _layer_spec(layer, (ret_w, D_MODEL)), _layer_spec(layer, (sb_w, D_MODEL)),
            _layer_spec(layer, (D_MODEL, D_MODEL)),
        ],
        out_specs=pl.BlockSpec((tile, D_MODEL), lambda i: (i, 0)),
        out_shape=jax.ShapeDtypeStruct((out_rows, D_MODEL), F32),
        compiler_params=_params("parallel"),
        name="mixer_merge",
    )(h, g, w_in, w_in, w_in, w_in, w_in, gn, o_r, o_s, w_ret, w_sb, w_out)


def kernel(x, meta, ffn1_norm, ffn1_w_gu, ffn1_w_down, mix_norm, w_in, ret_gn, w_ret_proj,
           w_sb_proj, w_out, ffn2_norm, ffn2_w_gu, ffn2_w_down, final_norm):
    batch, seq, d = x.shape
    real_rows = batch * seq
    rows = real_rows + N_META
    assert d == D_MODEL and seq % BLK == 0 and real_rows % ROW_TILE == 0 and rows % MID_TILE == 0

    f32 = np.float32
    pos = np.concatenate([np.tile(np.arange(seq) + N_META, batch), np.arange(N_META)]).astype(f32)
    freqs = f32(ROPE_BASE) ** (-np.arange(0, RET_DK, 2, dtype=f32) / f32(RET_DK))
    ang = pos[:, None] * freqs[None, :]
    cos_h, sin_h = np.cos(ang).astype(f32), np.sin(ang).astype(f32)
    cos_t = jnp.asarray(np.concatenate([cos_h, cos_h] * 2, axis=1))
    sin_t = jnp.asarray(np.concatenate([-sin_h, sin_h] * 2, axis=1))
    log_gamma = np.log(f32(1.0) - f32(2.0) ** (-5.0 - np.arange(RET_HEADS, dtype=f32))).astype(f32)
    lg_tab = jnp.asarray(np.broadcast_to(log_gamma.reshape(RET_HEADS // 2, 2, 1), (RET_HEADS // 2, 2, BLK)))

    gains = lambda t: t.reshape(t.shape[0], 1, t.shape[1])
    final_gain = final_norm.reshape(1, d)
    for l in range(DEPTH):
        last = l == DEPTH - 1
        ffn1 = functools.partial(_ffn, layer=l, g=gains(ffn1_norm), w_gu=ffn1_w_gu, w_down=ffn1_w_down,
                                 gf=final_gain, final_norm=False)
        if l == 0:
            h = ffn1(x.reshape(real_rows, d), tile=ROW_TILE, out_rows=rows)
            h = ffn1(meta.astype(x.dtype), tile=N_META, out_rows=rows,
                     out_offset=real_rows // N_META, filled=(h,))
        else:
            h = ffn1(h, tile=MID_TILE)
        rq, rk, rv, sq, sk, sv = _proj(h, l, gains(mix_norm), w_in, cos_t, sin_t, tile=MID_TILE)
        o_r = _retention(lg_tab, rq, rk, rv, batch)
        o_s = _stick_breaking(sq, sk, sv, batch)
        tile, out_rows = (ROW_TILE, real_rows) if last else (MID_TILE, rows)
        h = _merge(h, l, gains(mix_norm), w_in, gains(ret_gn), o_r, o_s, w_ret_proj, w_sb_proj, w_out,
                   tile=tile, out_rows=out_rows)
        h = _ffn(h, l, gains(ffn2_norm), ffn2_w_gu, ffn2_w_down, final_gain, final_norm=last, tile=tile)
    return h.reshape(batch, seq, d)
```

```python
import functools

import jax
import jax.numpy as jnp
import numpy as np
from jax import lax
from jax.experimental import pallas as pl
from jax.experimental.pallas import tpu as pltpu

D_MODEL = 1024
DEPTH = 2
N_META = 16
RET_HEADS = 8
RET_DK = 64
RET_DV = 128
SB_HEADS = 8
SB_DH = 64
D_FF = 2816
ROPE_BASE = 10000.0
EPS = 1e-6

LANES = 128
BLK = 256
PAD = BLK - N_META
FF_CHUNK = 256
ROW_TILE = 512
MID_TILE = 656
QUERY_BLOCKS = 2
RET_CHUNKS = 2
MASK_BIG = 1e30
LOG2E = 1.4426950408889634
UNDERFLOW_LOG2 = -160.0
VMEM_LIMIT = 56 * 1024 * 1024

_COLS = np.cumsum([0, RET_HEADS * RET_DK, RET_HEADS * RET_DK, RET_HEADS * RET_DV, RET_HEADS * RET_DV,
                   SB_HEADS * SB_DH, SB_HEADS * SB_DH, SB_HEADS * SB_DH, D_MODEL, D_MODEL])
C_RQ, C_RK, C_RV, C_RG, C_SQ, C_SK, C_SV, C_GA, C_GB, C_END = (int(c) for c in _COLS)

F32 = jnp.float32
BF16 = jnp.bfloat16


def _dot(a, b):
    return jnp.dot(a, b, preferred_element_type=F32)


def _dot_nt(a, b):
    return lax.dot_general(a, b, (((1,), (1,)), ((), ())), preferred_element_type=F32)


def _rmsnorm(x, g):
    r = lax.rsqrt(jnp.mean(x * x, axis=-1, keepdims=True) + EPS)
    return (x * r) * g


def _sigmoid(x):
    return 1.0 / (1.0 + jnp.exp(-x))


def _params(*sem):
    return pltpu.CompilerParams(dimension_semantics=sem, vmem_limit_bytes=VMEM_LIMIT)


def _resident_spec(block, index_map):
    return pl.BlockSpec(block, index_map, pipeline_mode=pl.Buffered(1))


def _layer_spec(layer, shape, col_block=0):
    return _resident_spec((None,) + shape, lambda *_: (layer, 0, col_block))


def _ffn_kernel(x_ref, g_ref, wgu_ref, wd_ref, gf_ref, *rest, final_norm):
    o_ref, gate_ref = rest[-2:]
    x = x_ref[...]
    u = _rmsnorm(x, g_ref[...]).astype(BF16)
    for c in range(0, D_FF, FF_CHUNK):
        a = _dot(u, wgu_ref[:, c:c + FF_CHUNK].astype(BF16))
        b = _dot(u, wgu_ref[:, D_FF + c:D_FF + c + FF_CHUNK].astype(BF16))
        gate_ref[:, c:c + FF_CHUNK] = (0.5 * a * _sigmoid(a) * b).astype(BF16)
    y = x + _dot(gate_ref[...], wd_ref[...].astype(BF16))
    if final_norm:
        y = _rmsnorm(y, gf_ref[...])
    o_ref[...] = y


def _ffn(h, layer, g, w_gu, w_down, gf, *, final_norm, tile, out_rows=None, out_offset=0, filled=()):
    out_rows = h.shape[0] if out_rows is None else out_rows
    return pl.pallas_call(
        functools.partial(_ffn_kernel, final_norm=final_norm),
        grid=(h.shape[0] // tile,),
        in_specs=[
            pl.BlockSpec((tile, D_MODEL), lambda i: (i, 0)),
            _layer_spec(layer, (1, D_MODEL)),
            _layer_spec(layer, (D_MODEL, 2 * D_FF)),
            _layer_spec(layer, (D_FF, D_MODEL)),
            _resident_spec((1, D_MODEL), lambda i: (0, 0)),
        ] + [pl.BlockSpec(memory_space=pl.ANY)] * len(filled),
        out_specs=pl.BlockSpec((tile, D_MODEL), lambda i: (i + out_offset, 0)),
        out_shape=jax.ShapeDtypeStruct((out_rows, D_MODEL), F32),
        input_output_aliases={5: 0} if filled else {},
        scratch_shapes=[pltpu.VMEM((tile, D_FF), BF16)],
        compiler_params=_params("parallel"),
        name="ffn_final" if final_norm else "ffn",
    )(h, g, w_gu, w_down, gf, *filled)


def _proj_kernel(x_ref, g_ref, wr_ref, ws_ref, cos_ref, sin_ref,
                 rq_ref, rk_ref, rv_ref, sq_ref, sk_ref, sv_ref):
    u = _rmsnorm(x_ref[...], g_ref[...]).astype(BF16)
    qk_w = RET_HEADS * RET_DK
    cos = jnp.concatenate([cos_ref[...]] * (qk_w // LANES), axis=1)
    sin = jnp.concatenate([sin_ref[...]] * (qk_w // LANES), axis=1)
    lane = lax.broadcasted_iota(jnp.int32, (x_ref.shape[0], qk_w), 1)
    first_half = (lane % RET_DK) < (RET_DK // 2)

    def rotary(t):
        partner = jnp.where(first_half, pltpu.roll(t, qk_w - RET_DK // 2, 1),
                            pltpu.roll(t, RET_DK // 2, 1))
        return t * cos + partner * sin

    def proj(w_ref, lo, hi):
        return _dot(u, w_ref[:, lo:hi].astype(BF16))

    rq_ref[...] = rotary(proj(wr_ref, C_RQ, C_RK)).astype(BF16)
    rk_ref[...] = (rotary(proj(wr_ref, C_RK, C_RV)) * RET_DK ** -0.5).astype(BF16)
    rv_ref[...] = proj(wr_ref, C_RV, C_RG).astype(BF16)
    sq_ref[...] = (proj(ws_ref, 0, C_SK - C_SQ) * (-LOG2E * SB_DH ** -0.5)).astype(BF16)
    sk_ref[...] = proj(ws_ref, C_SK - C_SQ, C_SV - C_SQ).astype(BF16)
    sv_ref[...] = proj(ws_ref, C_SV - C_SQ, C_GA - C_SQ).astype(BF16)


def _proj(h, layer, g, w_in, cos_t, sin_t, *, tile):
    rows = h.shape[0]

    def row_spec(width):
        return pl.BlockSpec((tile, width), lambda i: (i, 0))

    tab_spec = row_spec(LANES)
    widths = (C_RK - C_RQ, C_RV - C_RK, C_RG - C_RV, C_SK - C_SQ, C_SV - C_SK, C_GA - C_SV)
    sb_cols = C_GA - C_SQ
    assert C_SQ % sb_cols == 0
    return pl.pallas_call(
        _proj_kernel,
        grid=(rows // tile,),
        in_specs=[row_spec(D_MODEL), _layer_spec(layer, (1, D_MODEL)),
                  _layer_spec(layer, (D_MODEL, C_RG)),
                  _layer_spec(layer, (D_MODEL, sb_cols), col_block=C_SQ // sb_cols),
                  tab_spec, tab_spec],
        out_specs=[row_spec(w) for w in widths],
        out_shape=[jax.ShapeDtypeStruct((rows, w), BF16) for w in widths],
        compiler_params=_params("parallel"),
        name="mixer_proj",
    )(h, g, w_in, w_in, cos_t, sin_t)


def _meta_chunk(m_ref):
    m = m_ref[...]
    return jnp.concatenate([jnp.zeros((PAD, m.shape[1]), m.dtype), m], axis=0)


def _ret_kernel(lg_ref, *refs, meta_only):
    o_ref, state_ref, dec_ref, kdec_ref, qdec_ref, sdec_ref = refs[-6:]
    c = pl.program_id(1)
    pairs = RET_HEADS // 2

    @pl.when(c == 0)
    def _init():
        t = lax.broadcasted_iota(jnp.int32, (BLK, BLK), 0)
        s = lax.broadcasted_iota(jnp.int32, (BLK, BLK), 1)
        diff = (t - s).astype(F32)
        lane = lax.broadcasted_iota(jnp.int32, (BLK, LANES), 1)
        pos = lax.broadcasted_iota(jnp.int32, (BLK, LANES), 0).astype(F32)
        col = lax.broadcasted_iota(jnp.int32, (BLK, 2 * RET_DV), 1)
        posv = lax.broadcasted_iota(jnp.int32, (BLK, 2 * RET_DV), 0).astype(F32)
        srow = lax.broadcasted_iota(jnp.int32, (LANES, 2 * RET_DV), 0)
        scol = lax.broadcasted_iota(jnp.int32, (LANES, 2 * RET_DV), 1)
        own = (srow // RET_DK) == (scol // RET_DV)
        for p in range(pairs):
            lg = lg_ref[p]
            for i in range(2):
                dec_ref[2 * p + i] = jnp.where(
                    diff >= 0, jnp.exp(lg[i:i + 1, :] * jnp.maximum(diff, 0.0)), 0.0)
            lg_k = jnp.where(lane < RET_DK, lg[0:1, :LANES], lg[1:2, :LANES])
            kdec_ref[p] = jnp.exp(lg_k * (BLK - 1.0 - pos))
            lg_v = jnp.where(col < RET_DV, lg[0:1, :], lg[1:2, :])
            qdec_ref[p] = jnp.exp(lg_v * (posv + 1.0))
            lg_s = jnp.where(scol < RET_DV, lg[0:1, :], lg[1:2, :])
            sdec_ref[2 * p] = jnp.where(own, jnp.exp(lg_s * float(BLK)), 0.0)
            sdec_ref[2 * p + 1] = jnp.where(own, 1.0, 0.0)
        state_ref[...] = jnp.zeros_like(state_ref)

    def decayed_keys_t(k_pair, p):
        return (k_pair.astype(F32) * kdec_ref[p]).T.astype(BF16)

    def chunk(q_all, k_all, v_all, out_rows=slice(None)):
        lane = lax.broadcasted_iota(jnp.int32, (BLK, LANES), 1)
        qs, ks, crosses, scores, outs = {}, {}, {}, {}, {}

        def pair_start(p):
            qs[p] = q_all[:, p * LANES:(p + 1) * LANES]
            ks[p] = k_all[:, p * LANES:(p + 1) * LANES]
            crosses[p] = _dot(qs[p], state_ref[p].astype(BF16)) * qdec_ref[p]

        def head_scores(head):
            p, i = divmod(head, 2)
            in_head = (lane < RET_DK) if i == 0 else (lane >= RET_DK)
            qi = jnp.where(in_head, qs[p], jnp.zeros_like(qs[p]))
            scores[head] = (_dot_nt(qi, ks[p]) * dec_ref[head]).astype(BF16)

        def head_values(head):
            p, i = divmod(head, 2)
            o = _dot(scores.pop(head), v_all[:, head * RET_DV:(head + 1) * RET_DV])
            outs[head] = o + crosses[p][:, i * RET_DV:(i + 1) * RET_DV]

        def head_norm(head):
            o = outs.pop(head)
            mu = jnp.mean(o, axis=-1, keepdims=True)
            d = o - mu
            var = jnp.mean(d * d, axis=-1, keepdims=True)
            normed = (d * lax.rsqrt(var + EPS)).astype(BF16)
            o_ref[out_rows, head * RET_DV:(head + 1) * RET_DV] = normed[PAD:] if meta_only else normed

        def pair_state(p):
            v = v_all[:, 2 * p * RET_DV:2 * (p + 1) * RET_DV]
            state_ref[p] = (state_ref[p] * sdec_ref[2 * p]
                            + _dot(decayed_keys_t(ks[p], p), v) * sdec_ref[2 * p + 1])

        for t in range(RET_HEADS + 2):
            if t < RET_HEADS:
                if t % 2 == 0:
                    pair_start(t // 2)
                head_scores(t)
            if 0 <= t - 1 < RET_HEADS:
                head_values(t - 1)
            if 0 <= t - 2 < RET_HEADS:
                head_norm(t - 2)
                if (t - 2) % 2 == 1:
                    pair_state((t - 2) // 2)

    if meta_only:
        chunk(*(_meta_chunk(r) for r in refs[:3]))
    else:
        @pl.when(c == 0)
        def _replay_meta_chunk():
            k_m, v_m = _meta_chunk(refs[3]), _meta_chunk(refs[4])
            for p in range(pairs):
                kd_t = decayed_keys_t(k_m[:, p * LANES:(p + 1) * LANES], p)
                v = v_m[:, 2 * p * RET_DV:2 * (p + 1) * RET_DV]
                state_ref[p] = _dot(kd_t, v) * sdec_ref[2 * p + 1]
            o_ref[...] = jnp.zeros_like(o_ref)

        @pl.when(c > 0)
        def _real_chunks():
            for i in range(refs[0].shape[0] // BLK):
                rows = slice(i * BLK, (i + 1) * BLK)
                chunk(refs[0][rows, :], refs[1][rows, :], refs[2][rows, :], rows)


def _retention(lg_tab, rq, rk, rv, batch):
    rows = rq.shape[0]
    pairs = RET_HEADS // 2
    qk_w, v_w = RET_HEADS * RET_DK, RET_HEADS * RET_DV
    real_rows = rows - N_META
    per_seq = real_rows // BLK // batch
    meta_rows = lambda *_: (real_rows // N_META, 0)
    meta_specs = [pl.BlockSpec((N_META, w), meta_rows) for w in (qk_w, qk_w, v_w)]
    lg_spec = pl.BlockSpec((pairs, 2, BLK), lambda b, c: (0, 0, 0))
    scratch = [
        pltpu.VMEM((pairs, LANES, 2 * RET_DV), F32),
        pltpu.VMEM((RET_HEADS, BLK, BLK), F32),
        pltpu.VMEM((pairs, BLK, LANES), F32),
        pltpu.VMEM((pairs, BLK, 2 * RET_DV), F32),
        pltpu.VMEM((2 * pairs, LANES, 2 * RET_DV), F32),
    ]
    out_shape = jax.ShapeDtypeStruct((rows, v_w), BF16)
    out = pl.pallas_call(
        functools.partial(_ret_kernel, meta_only=True),
        grid=(1, 1),
        in_specs=[lg_spec] + meta_specs,
        out_specs=pl.BlockSpec((N_META, v_w), meta_rows),
        out_shape=out_shape,
        scratch_shapes=scratch,
        compiler_params=_params("arbitrary", "arbitrary"),
        name="retention_meta",
    )(lg_tab, rq, rk, rv)

    assert per_seq % RET_CHUNKS == 0
    steps = per_seq // RET_CHUNKS
    step_rows = RET_CHUNKS * BLK

    def real_rows_of(b, c):
        return b * steps + jnp.maximum(c - 1, 0), 0

    return pl.pallas_call(
        functools.partial(_ret_kernel, meta_only=False),
        grid=(batch, steps + 1),
        in_specs=[lg_spec] + [pl.BlockSpec((step_rows, w), real_rows_of) for w in (qk_w, qk_w, v_w)]
        + meta_specs[1:] + [pl.BlockSpec(memory_space=pl.ANY)],
        out_specs=pl.BlockSpec((step_rows, v_w), real_rows_of),
        out_shape=out_shape,
        input_output_aliases={6: 0},
        scratch_shapes=scratch,
        compiler_params=_params("parallel", "arbitrary"),
        name="retention",
    )(lg_tab, rq, rk, rv, rk, rv, out)


def _sb_kernel(q_ref, km_ref, vm_ref, *rest, meta_only):
    o_ref, k_ref, v_ref, acc_ref, carry_ref = rest[-5:]
    step_id = pl.program_id(1)

    @pl.when(step_id == 0)
    def _assemble_sequence():
        k_ref[0:BLK] = _meta_chunk(km_ref)
        v_ref[0:BLK] = _meta_chunk(vm_ref)
        if not meta_only:
            k_ref[BLK:] = rest[0][...]
            v_ref[BLK:] = rest[1][...]

    pairs = SB_HEADS // 2
    lane = lax.broadcasted_iota(jnp.int32, (BLK, LANES), 1)
    q_all = _meta_chunk(q_ref) if meta_only else q_ref[...]
    q_heads = []
    for qi in range(q_all.shape[0] // BLK):
        for p in range(pairs):
            q = q_all[qi * BLK:(qi + 1) * BLK, p * LANES:(p + 1) * LANES]
            zero = jnp.zeros_like(q)
            q_heads += [jnp.where(lane < SB_DH, q, zero), jnp.where(lane >= SB_DH, q, zero)]
    j = lax.broadcasted_iota(jnp.int32, (BLK, BLK), 0)
    c = lax.broadcasted_iota(jnp.int32, (BLK, BLK), 1)
    suffix = jnp.where(j >= c, 1.0, 0.0).astype(BF16)
    key_col = lax.broadcasted_iota(jnp.int32, (1, BLK), 1)

    def step(blocks, first):
        chains = [(qi * SB_HEADS + h, kb, bias, h) for qi, kb, bias in blocks for h in range(SB_HEADS)]
        slots = sorted({slot for slot, _, _, _ in chains})
        ys, logs, totals, sums = {}, {}, {}, {}
        carries = {slot: None if first else carry_ref[slot] for slot in slots}
        accs = {slot: None for slot in slots}

        def soft_log(y):
            return jnp.minimum(y, 0.0) - jnp.log2(1.0 + jnp.exp2(-jnp.abs(y)))

        def rows_of(kb):
            start = kb * BLK
            return pl.ds(start if isinstance(start, int) else pl.multiple_of(start, BLK), BLK)

        def front(n):
            slot, kb, bias, h = chains[n]
            cols = slice((h // 2) * LANES, (h // 2 + 1) * LANES)
            y = _dot_nt(q_heads[slot], k_ref[rows_of(kb), cols])
            if bias is not None:
                y = y + bias
            log_1mb = soft_log(y)
            ys[n], logs[n] = y, log_1mb.astype(BF16)
            totals[n] = jnp.sum(log_1mb, axis=1, keepdims=True)

        def back(n):
            slot, kb, _, h = chains[n]
            cols = slice((h // 2) * LANES, (h // 2 + 1) * LANES)
            incl = sums.pop(n) if carries[slot] is None else sums.pop(n) + carries[slot]
            a = jnp.exp2(incl - ys.pop(n)).astype(BF16)
            pv = _dot(a, v_ref[rows_of(kb), cols])
            carries[slot] = totals[n] if carries[slot] is None else carries[slot] + totals[n]
            accs[slot] = pv if accs[slot] is None else accs[slot] + pv

        for t in range(len(chains) + 2):
            if t < len(chains):
                front(t)
            if 0 <= t - 1 < len(chains):
                sums[t - 1] = _dot(logs.pop(t - 1), suffix)
            if 0 <= t - 2 < len(chains):
                back(t - 2)
        for slot in slots:
            carry_ref[slot] = carries[slot]
            if first:
                acc_ref[slot] = accs[slot]
            else:
                acc_ref[slot] += accs[slot]

    def valid_bias(kb):
        return jnp.where(kb * BLK + key_col < PAD, MASK_BIG, 0.0).astype(F32)

    r = lax.broadcasted_iota(jnp.int32, (BLK, BLK), 0)
    s = lax.broadcasted_iota(jnp.int32, (BLK, BLK), 1)
    causal = jnp.where(s < r, 0.0, MASK_BIG).astype(F32)

    if meta_only:
        step([(0, 0, causal + valid_bias(0))], True)
    else:
        first_q = QUERY_BLOCKS * step_id + 1
        blocks = []
        for qi in range(QUERY_BLOCKS):
            prev_bias = valid_bias(first_q - 1) if qi == 0 else None
            blocks += [(qi, first_q + qi, causal), (qi, first_q + qi - 1, prev_bias)]
        step(blocks, True)

        for qi in range(QUERY_BLOCKS):
            def decayed(qi=qi):
                top = carry_ref[qi * SB_HEADS]
                for h in range(1, SB_HEADS):
                    top = jnp.maximum(top, carry_ref[qi * SB_HEADS + h])
                return jnp.max(top) < UNDERFLOW_LOG2

            def live(state):
                kb, done = state
                return jnp.logical_and(kb >= 0, jnp.logical_not(done))

            def body(state, qi=qi, decayed=decayed):
                kb, _ = state
                step([(qi, kb, valid_bias(kb))], False)
                return kb - 1, decayed()

            lax.while_loop(live, body, (first_q + qi - 2, decayed()))
    for qi in range(q_all.shape[0] // BLK):
        for p in range(pairs):
            slot = qi * SB_HEADS + 2 * p
            out = jnp.where(lane < SB_DH, acc_ref[slot], acc_ref[slot + 1]).astype(BF16)
            rows = slice(None) if meta_only else slice(qi * BLK, (qi + 1) * BLK)
            o_ref[rows, p * LANES:(p + 1) * LANES] = out[PAD:] if meta_only else out


def _stick_breaking(sq, sk, sv, batch):
    rows, width = sq.shape
    assert PAD <= BLK and width == SB_HEADS * SB_DH
    real_rows = rows - N_META
    per_seq = real_rows // BLK // batch
    seq = per_seq * BLK
    meta_spec = pl.BlockSpec((N_META, width), lambda b, i: (real_rows // N_META, 0))
    out_shape = jax.ShapeDtypeStruct((rows, width), BF16)

    def scratch(key_rows, query_blocks):
        slots = query_blocks * SB_HEADS
        return [pltpu.VMEM((key_rows, width), BF16), pltpu.VMEM((key_rows, width), BF16),
                pltpu.VMEM((slots, BLK, LANES), F32), pltpu.VMEM((slots, BLK, 1), F32)]

    out = pl.pallas_call(
        functools.partial(_sb_kernel, meta_only=True),
        grid=(1, 1),
        in_specs=[meta_spec] * 3,
        out_specs=meta_spec,
        out_shape=out_shape,
        scratch_shapes=scratch(BLK, 1),
        compiler_params=_params("arbitrary", "arbitrary"),
        name="stick_breaking_meta",
    )(sq, sk, sv)
    assert per_seq % QUERY_BLOCKS == 0
    steps = per_seq // QUERY_BLOCKS
    q_spec = pl.BlockSpec((QUERY_BLOCKS * BLK, width), lambda b, i: (b * steps + i, 0))
    real_spec = pl.BlockSpec((seq, width), lambda b, i: (b, 0))
    return pl.pallas_call(
        functools.partial(_sb_kernel, meta_only=False),
        grid=(batch, steps),
        in_specs=[q_spec, meta_spec, meta_spec, real_spec, real_spec, pl.BlockSpec(memory_space=pl.ANY)],
        out_specs=q_spec,
        out_shape=out_shape,
        input_output_aliases={5: 0},
        scratch_shapes=scratch(seq + BLK, QUERY_BLOCKS),
        compiler_params=_params("parallel", "arbitrary"),
        name="stick_breaking",
    )(sq, sk, sv, sk, sv, out)


def _merge_kernel(x_ref, g_ref, wrg_ref, wga0_ref, wga1_ref, wgb0_ref, wgb1_ref, gn_ref,
                  or_ref, os_ref, wr_ref, ws_ref, wo_ref, o_ref):
    x = x_ref[...]
    u = _rmsnorm(x, g_ref[...]).astype(BF16)

    def gate(*w_refs):
        return jnp.concatenate([_dot(u, w[...].astype(BF16)) for w in w_refs], axis=1)

    rg = gate(wrg_ref)
    y_sb = _dot(os_ref[...], ws_ref[...].astype(BF16))
    gb = _sigmoid(gate(wgb0_ref, wgb1_ref)) * y_sb
    ga = _sigmoid(gate(wga0_ref, wga1_ref))
    gated = (rg * _sigmoid(rg)) * (or_ref[...].astype(F32) * gn_ref[...])
    y_ret = _dot(gated.astype(BF16), wr_ref[...].astype(BF16))
    y = ga * y_ret + gb
    o_ref[...] = x + _dot(y.astype(BF16), wo_ref[...].astype(BF16))


def _merge(h, layer, g, w_in, gn, o_r, o_s, w_ret, w_sb, w_out, *, tile, out_rows):
    def in_spec(width):
        return pl.BlockSpec((tile, width), lambda i: (i, 0))

    half = D_MODEL // 2
    assert C_RG % D_MODEL == 0 and C_GA % half == 0
    ret_w, sb_w = RET_HEADS * RET_DV, SB_HEADS * SB_DH
    return pl.pallas_call(
        _merge_kernel,
        grid=(out_rows // tile,),
        in_specs=[
            in_spec(D_MODEL), _layer_spec(layer, (1, D_MODEL)),
            _layer_spec(layer, (D_MODEL, D_MODEL), col_block=C_RG // D_MODEL),
            _layer_spec(layer, (D_MODEL, half), col_block=C_GA // half),
            _layer_spec(layer, (D_MODEL, half), col_block=C_GA // half + 1),
            _layer_spec(layer, (D_MODEL, half), col_block=C_GB // half),
            _layer_spec(layer, (D_MODEL, half), col_block=C_GB // half + 1),
            _layer_spec(layer, (1, ret_w)), in_spec(ret_w), in_spec(sb_w),
            _layer_spec(layer, (ret_w, D_MODEL)), _layer_spec(layer, (sb_w, D_MODEL)),
            _layer_spec(layer, (D_MODEL, D_MODEL)),
        ],
        out_specs=pl.BlockSpec((tile, D_MODEL), lambda i: (i, 0)),
        out_shape=jax.ShapeDtypeStruct((out_rows, D_MODEL), F32),
        compiler_params=_params("parallel"),
        name="mixer_merge",
    )(h, g, w_in, w_in, w_in, w_in, w_in, gn, o_r, o_s, w_ret, w_sb, w_out)


def kernel(x, meta, ffn1_norm, ffn1_w_gu, ffn1_w_down, mix_norm, w_in, ret_gn, w_ret_proj,
           w_sb_proj, w_out, ffn2_norm, ffn2_w_gu, ffn2_w_down, final_norm):
    batch, seq, d = x.shape
    real_rows = batch * seq
    rows = real_rows + N_META
    assert d == D_MODEL and seq % BLK == 0 and real_rows % ROW_TILE == 0 and rows % MID_TILE == 0
    assert meta.shape == (N_META, d) and w_in.shape == (DEPTH, d, C_END)
    assert ffn1_w_gu.shape == (DEPTH, d, 2 * D_FF) and ffn1_w_down.shape == (DEPTH, D_FF, d)

    f32 = np.float32
    pos = np.concatenate([np.tile(np.arange(seq) + N_META, batch), np.arange(N_META)]).astype(f32)
    freqs = f32(ROPE_BASE) ** (-np.arange(0, RET_DK, 2, dtype=f32) / f32(RET_DK))
    ang = pos[:, None] * freqs[None, :]
    cos_h, sin_h = np.cos(ang).astype(f32), np.sin(ang).astype(f32)
    cos_t = jnp.asarray(np.concatenate([cos_h, cos_h] * 2, axis=1))
    sin_t = jnp.asarray(np.concatenate([-sin_h, sin_h] * 2, axis=1))
    log_gamma = np.log(f32(1.0) - f32(2.0) ** (-5.0 - np.arange(RET_HEADS, dtype=f32))).astype(f32)
    lg_tab = jnp.asarray(np.broadcast_to(log_gamma.reshape(RET_HEADS // 2, 2, 1), (RET_HEADS // 2, 2, BLK)))

    gains = lambda t: t.reshape(t.shape[0], 1, t.shape[1])
    final_gain = final_norm.reshape(1, d)
    for l in range(DEPTH):
        last = l == DEPTH - 1
        ffn1 = functools.partial(_ffn, layer=l, g=gains(ffn1_norm), w_gu=ffn1_w_gu, w_down=ffn1_w_down,
                                 gf=final_gain, final_norm=False)
        if l == 0:
            h = ffn1(x.reshape(real_rows, d), tile=ROW_TILE, out_rows=rows)
            h = ffn1(meta.astype(x.dtype), tile=N_META, out_rows=rows,
                     out_offset=real_rows // N_META, filled=(h,))
        else:
            h = ffn1(h, tile=MID_TILE)
        rq, rk, rv, sq, sk, sv = _proj(h, l, gains(mix_norm), w_in, cos_t, sin_t, tile=MID_TILE)
        o_r = _retention(lg_tab, rq, rk, rv, batch)
        o_s = _stick_breaking(sq, sk, sv, batch)
        tile, out_rows = (ROW_TILE, real_rows) if last else (MID_TILE, rows)
        h = _merge(h, l, gains(mix_norm), w_in, gains(ret_gn), o_r, o_s, w_ret_proj, w_sb_proj, w_out,
                   tile=tile, out_rows=out_rows)
        h = _ffn(h, l, gains(ffn2_norm), ffn2_w_gu, ffn2_w_down, final_gain, final_norm=last, tile=tile)
    return h.reshape(batch, seq, d)
```

```python
import functools

import jax
import jax.numpy as jnp
import numpy as np
from jax import lax
from jax.experimental import pallas as pl
from jax.experimental.pallas import tpu as pltpu

D_MODEL = 1024
DEPTH = 2
N_META = 16
RET_HEADS = 8
RET_DK = 64
RET_DV = 128
SB_HEADS = 8
SB_DH = 64
D_FF = 2816
ROPE_BASE = 10000.0
EPS = 1e-6

LANES = 128
BLK = 256
PAD = BLK - N_META
FF_CHUNK = 256
ROW_TILE = 512
MID_TILE = 656
QUERY_BLOCKS = 2
RET_CHUNKS = 2
MASK_BIG = 1e30
LOG2E = 1.4426950408889634
UNDERFLOW_LOG2 = -160.0
VMEM_LIMIT = 56 * 1024 * 1024

_COLS = np.cumsum([0, RET_HEADS * RET_DK, RET_HEADS * RET_DK, RET_HEADS * RET_DV, RET_HEADS * RET_DV,
                   SB_HEADS * SB_DH, SB_HEADS * SB_DH, SB_HEADS * SB_DH, D_MODEL, D_MODEL])
C_RQ, C_RK, C_RV, C_RG, C_SQ, C_SK, C_SV, C_GA, C_GB, C_END = (int(c) for c in _COLS)

F32 = jnp.float32
BF16 = jnp.bfloat16


def _dot(a, b):
    return jnp.dot(a, b, preferred_element_type=F32)


def _dot_nt(a, b):
    return lax.dot_general(a, b, (((1,), (1,)), ((), ())), preferred_element_type=F32)


def _rmsnorm(x, g):
    r = lax.rsqrt(jnp.mean(x * x, axis=-1, keepdims=True) + EPS)
    return (x * r) * g


def _sigmoid(x):
    return 1.0 / (1.0 + jnp.exp(-x))


def _params(*sem):
    return pltpu.CompilerParams(dimension_semantics=sem, vmem_limit_bytes=VMEM_LIMIT)


def _resident_spec(block, index_map):
    return pl.BlockSpec(block, index_map, pipeline_mode=pl.Buffered(1))


def _layer_spec(layer, shape, col_block=0):
    return _resident_spec((None,) + shape, lambda *_: (layer, 0, col_block))


def _ffn_kernel(x_ref, g_ref, wgu_ref, wd_ref, gf_ref, *rest, final_norm):
    o_ref, gate_ref = rest[-2:]
    x = x_ref[...]
    u = _rmsnorm(x, g_ref[...]).astype(BF16)
    for c in range(0, D_FF, FF_CHUNK):
        a = _dot(u, wgu_ref[:, c:c + FF_CHUNK].astype(BF16))
        b = _dot(u, wgu_ref[:, D_FF + c:D_FF + c + FF_CHUNK].astype(BF16))
        gate_ref[:, c:c + FF_CHUNK] = (0.5 * a * _sigmoid(a) * b).astype(BF16)
    y = x + _dot(gate_ref[...], wd_ref[...].astype(BF16))
    if final_norm:
        y = _rmsnorm(y, gf_ref[...])
    o_ref[...] = y


def _ffn(h, layer, g, w_gu, w_down, gf, *, final_norm, tile, out_rows=None, out_offset=0, filled=()):
    out_rows = h.shape[0] if out_rows is None else out_rows
    return pl.pallas_call(
        functools.partial(_ffn_kernel, final_norm=final_norm),
        grid=(h.shape[0] // tile,),
        in_specs=[
            pl.BlockSpec((tile, D_MODEL), lambda i: (i, 0)),
            _layer_spec(layer, (1, D_MODEL)),
            _layer_spec(layer, (D_MODEL, 2 * D_FF)),
            _layer_spec(layer, (D_FF, D_MODEL)),
            _resident_spec((1, D_MODEL), lambda i: (0, 0)),
        ] + [pl.BlockSpec(memory_space=pl.ANY)] * len(filled),
        out_specs=pl.BlockSpec((tile, D_MODEL), lambda i: (i + out_offset, 0)),
        out_shape=jax.ShapeDtypeStruct((out_rows, D_MODEL), F32),
        input_output_aliases={5: 0} if filled else {},
        scratch_shapes=[pltpu.VMEM((tile, D_FF), BF16)],
        compiler_params=_params("parallel"),
        name="ffn_final" if final_norm else "ffn",
    )(h, g, w_gu, w_down, gf, *filled)


def _proj_kernel(x_ref, g_ref, wr_ref, ws_ref, cos_ref, sin_ref,
                 rq_ref, rk_ref, rv_ref, sq_ref, sk_ref, sv_ref):
    u = _rmsnorm(x_ref[...], g_ref[...]).astype(BF16)
    qk_w = RET_HEADS * RET_DK
    cos = jnp.concatenate([cos_ref[...]] * (qk_w // LANES), axis=1)
    sin = jnp.concatenate([sin_ref[...]] * (qk_w // LANES), axis=1)
    lane = lax.broadcasted_iota(jnp.int32, (x_ref.shape[0], qk_w), 1)
    first_half = (lane % RET_DK) < (RET_DK // 2)

    def rotary(t):
        partner = jnp.where(first_half, pltpu.roll(t, qk_w - RET_DK // 2, 1),
                            pltpu.roll(t, RET_DK // 2, 1))
        return t * cos + partner * sin

    def proj(w_ref, lo, hi):
        return _dot(u, w_ref[:, lo:hi].astype(BF16))

    rq_ref[...] = rotary(proj(wr_ref, C_RQ, C_RK)).astype(BF16)
    rk_ref[...] = (rotary(proj(wr_ref, C_RK, C_RV)) * RET_DK ** -0.5).astype(BF16)
    rv_ref[...] = proj(wr_ref, C_RV, C_RG).astype(BF16)
    sq_ref[...] = (proj(ws_ref, 0, C_SK - C_SQ) * (-LOG2E * SB_DH ** -0.5)).astype(BF16)
    sk_ref[...] = proj(ws_ref, C_SK - C_SQ, C_SV - C_SQ).astype(BF16)
    sv_ref[...] = proj(ws_ref, C_SV - C_SQ, C_GA - C_SQ).astype(BF16)


def _proj(h, layer, g, w_in, cos_t, sin_t, *, tile):
    rows = h.shape[0]

    def row_spec(width):
        return pl.BlockSpec((tile, width), lambda i: (i, 0))

    tab_spec = row_spec(LANES)
    widths = (C_RK - C_RQ, C_RV - C_RK, C_RG - C_RV, C_SK - C_SQ, C_SV - C_SK, C_GA - C_SV)
    sb_cols = C_GA - C_SQ
    assert C_SQ % sb_cols == 0
    return pl.pallas_call(
        _proj_kernel,
        grid=(rows // tile,),
        in_specs=[row_spec(D_MODEL), _layer_spec(layer, (1, D_MODEL)),
                  _layer_spec(layer, (D_MODEL, C_RG)),
                  _layer_spec(layer, (D_MODEL, sb_cols), col_block=C_SQ // sb_cols),
                  tab_spec, tab_spec],
        out_specs=[row_spec(w) for w in widths],
        out_shape=[jax.ShapeDtypeStruct((rows, w), BF16) for w in widths],
        compiler_params=_params("parallel"),
        name="mixer_proj",
    )(h, g, w_in, w_in, cos_t, sin_t)


def _meta_chunk(m_ref):
    m = m_ref[...]
    return jnp.concatenate([jnp.zeros((PAD, m.shape[1]), m.dtype), m], axis=0)


def _ret_kernel(lg_ref, *refs, meta_only):
    o_ref, state_ref, dec_ref, kdec_ref, qdec_ref, sdec_ref = refs[-6:]
    c = pl.program_id(1)
    pairs = RET_HEADS // 2

    @pl.when(c == 0)
    def _init():
        t = lax.broadcasted_iota(jnp.int32, (BLK, BLK), 0)
        s = lax.broadcasted_iota(jnp.int32, (BLK, BLK), 1)
        diff = (t - s).astype(F32)
        lane = lax.broadcasted_iota(jnp.int32, (BLK, LANES), 1)
        pos = lax.broadcasted_iota(jnp.int32, (BLK, LANES), 0).astype(F32)
        col = lax.broadcasted_iota(jnp.int32, (BLK, 2 * RET_DV), 1)
        posv = lax.broadcasted_iota(jnp.int32, (BLK, 2 * RET_DV), 0).astype(F32)
        srow = lax.broadcasted_iota(jnp.int32, (LANES, 2 * RET_DV), 0)
        scol = lax.broadcasted_iota(jnp.int32, (LANES, 2 * RET_DV), 1)
        own = (srow // RET_DK) == (scol // RET_DV)
        for p in range(pairs):
            lg = lg_ref[p]
            for i in range(2):
                dec_ref[2 * p + i] = jnp.where(
                    diff >= 0, jnp.exp(lg[i:i + 1, :] * jnp.maximum(diff, 0.0)), 0.0)
            lg_k = jnp.where(lane < RET_DK, lg[0:1, :LANES], lg[1:2, :LANES])
            kdec_ref[p] = jnp.exp(lg_k * (BLK - 1.0 - pos))
            lg_v = jnp.where(col < RET_DV, lg[0:1, :], lg[1:2, :])
            qdec_ref[p] = jnp.exp(lg_v * (posv + 1.0))
            lg_s = jnp.where(scol < RET_DV, lg[0:1, :], lg[1:2, :])
            sdec_ref[2 * p] = jnp.where(own, jnp.exp(lg_s * float(BLK)), 0.0)
            sdec_ref[2 * p + 1] = jnp.where(own, 1.0, 0.0)
        state_ref[...] = jnp.zeros_like(state_ref)

    def decayed_keys_t(k_pair, p):
        return (k_pair.astype(F32) * kdec_ref[p]).T.astype(BF16)

    def chunk(q_all, k_all, v_all, out_rows=slice(None)):
        lane = lax.broadcasted_iota(jnp.int32, (BLK, LANES), 1)
        qs, ks, crosses, scores, outs = {}, {}, {}, {}, {}

        def pair_start(p):
            qs[p] = q_all[:, p * LANES:(p + 1) * LANES]
            ks[p] = k_all[:, p * LANES:(p + 1) * LANES]
            crosses[p] = _dot(qs[p], state_ref[p].astype(BF16)) * qdec_ref[p]

        def head_scores(head):
            p, i = divmod(head, 2)
            in_head = (lane < RET_DK) if i == 0 else (lane >= RET_DK)
            qi = jnp.where(in_head, qs[p], jnp.zeros_like(qs[p]))
            scores[head] = (_dot_nt(qi, ks[p]) * dec_ref[head]).astype(BF16)

        def head_values(head):
            p, i = divmod(head, 2)
            o = _dot(scores.pop(head), v_all[:, head * RET_DV:(head + 1) * RET_DV])
            outs[head] = o + crosses[p][:, i * RET_DV:(i + 1) * RET_DV]

        def head_norm(head):
            o = outs.pop(head)
            mu = jnp.mean(o, axis=-1, keepdims=True)
            d = o - mu
            var = jnp.mean(d * d, axis=-1, keepdims=True)
            normed = (d * lax.rsqrt(var + EPS)).astype(BF16)
            o_ref[out_rows, head * RET_DV:(head + 1) * RET_DV] = normed[PAD:] if meta_only else normed

        def pair_state(p):
            v = v_all[:, 2 * p * RET_DV:2 * (p + 1) * RET_DV]
            state_ref[p] = (state_ref[p] * sdec_ref[2 * p]
                            + _dot(decayed_keys_t(ks[p], p), v) * sdec_ref[2 * p + 1])

        for t in range(RET_HEADS + 2):
            if t < RET_HEADS:
                if t % 2 == 0:
                    pair_start(t // 2)
                head_scores(t)
            if 0 <= t - 1 < RET_HEADS:
                head_values(t - 1)
            if 0 <= t - 2 < RET_HEADS:
                head_norm(t - 2)
                if (t - 2) % 2 == 1:
                    pair_state((t - 2) // 2)

    if meta_only:
        chunk(*(_meta_chunk(r) for r in refs[:3]))
    else:
        @pl.when(c == 0)
        def _replay_meta_chunk():
            k_m, v_m = _meta_chunk(refs[3]), _meta_chunk(refs[4])
            for p in range(pairs):
                kd_t = decayed_keys_t(k_m[:, p * LANES:(p + 1) * LANES], p)
                v = v_m[:, 2 * p * RET_DV:2 * (p + 1) * RET_DV]
                state_ref[p] = _dot(kd_t, v) * sdec_ref[2 * p + 1]
            o_ref[...] = jnp.zeros_like(o_ref)

        @pl.when(c > 0)
        def _real_chunks():
            for i in range(refs[0].shape[0] // BLK):
                rows = slice(i * BLK, (i + 1) * BLK)
                chunk(refs[0][rows, :], refs[1][rows, :], refs[2][rows, :], rows)


def _retention(lg_tab, rq, rk, rv, batch):
    rows = rq.shape[0]
    pairs = RET_HEADS // 2
    qk_w, v_w = RET_HEADS * RET_DK, RET_HEADS * RET_DV
    real_rows = rows - N_META
    per_seq = real_rows // BLK // batch
    meta_rows = lambda *_: (real_rows // N_META, 0)
    meta_specs = [pl.BlockSpec((N_META, w), meta_rows) for w in (qk_w, qk_w, v_w)]
    lg_spec = pl.BlockSpec((pairs, 2, BLK), lambda b, c: (0, 0, 0))
    scratch = [
        pltpu.VMEM((pairs, LANES, 2 * RET_DV), F32),
        pltpu.VMEM((RET_HEADS, BLK, BLK), F32),
        pltpu.VMEM((pairs, BLK, LANES), F32),
        pltpu.VMEM((pairs, BLK, 2 * RET_DV), F32),
        pltpu.VMEM((2 * pairs, LANES, 2 * RET_DV), F32),
    ]
    out_shape = jax.ShapeDtypeStruct((rows, v_w), BF16)
    out = pl.pallas_call(
        functools.partial(_ret_kernel, meta_only=True),
        grid=(1, 1),
        in_specs=[lg_spec] + meta_specs,
        out_specs=pl.BlockSpec((N_META, v_w), meta_rows),
        out_shape=out_shape,
        scratch_shapes=scratch,
        compiler_params=_params("arbitrary", "arbitrary"),
        name="retention_meta",
    )(lg_tab, rq, rk, rv)

    assert per_seq % RET_CHUNKS == 0
    steps = per_seq // RET_CHUNKS
    step_rows = RET_CHUNKS * BLK

    def real_rows_of(b, c):
        return b * steps + jnp.maximum(c - 1, 0), 0

    return pl.pallas_call(
        functools.partial(_ret_kernel, meta_only=False),
        grid=(batch, steps + 1),
        in_specs=[lg_spec] + [pl.BlockSpec((step_rows, w), real_rows_of) for w in (qk_w, qk_w, v_w)]
        + meta_specs[1:] + [pl.BlockSpec(memory_space=pl.ANY)],
        out_specs=pl.BlockSpec((step_rows, v_w), real_rows_of),
        out_shape=out_shape,
        input_output_aliases={6: 0},
        scratch_shapes=scratch,
        compiler_params=_params("parallel", "arbitrary"),
        name="retention",
    )(lg_tab, rq, rk, rv, rk, rv, out)


def _sb_kernel(q_ref, km_ref, vm_ref, *rest, meta_only):
    o_ref, k_ref, v_ref, acc_ref, carry_ref = rest[-5:]
    step_id = pl.program_id(1)

    @pl.when(step_id == 0)
    def _assemble_sequence():
        k_ref[0:BLK] = _meta_chunk(km_ref)
        v_ref[0:BLK] = _meta_chunk(vm_ref)
        if not meta_only:
            k_ref[BLK:] = rest[0][...]
            v_ref[BLK:] = rest[1][...]

    pairs = SB_HEADS // 2
    lane = lax.broadcasted_iota(jnp.int32, (BLK, LANES), 1)
    q_all = _meta_chunk(q_ref) if meta_only else q_ref[...]
    q_heads = []
    for qi in range(q_all.shape[0] // BLK):
        for p in range(pairs):
            q = q_all[qi * BLK:(qi + 1) * BLK, p * LANES:(p + 1) * LANES]
            zero = jnp.zeros_like(q)
            q_heads += [jnp.where(lane < SB_DH, q, zero), jnp.where(lane >= SB_DH, q, zero)]
    j = lax.broadcasted_iota(jnp.int32, (BLK, BLK), 0)
    c = lax.broadcasted_iota(jnp.int32, (BLK, BLK), 1)
    suffix = jnp.where(j >= c, 1.0, 0.0).astype(BF16)
    key_col = lax.broadcasted_iota(jnp.int32, (1, BLK), 1)

    def step(blocks, first):
        chains = [(qi * SB_HEADS + h, kb, bias, h) for qi, kb, bias in blocks for h in range(SB_HEADS)]
        slots = sorted({slot for slot, _, _, _ in chains})
        ys, logs, totals, sums = {}, {}, {}, {}
        carries = {slot: None if first else carry_ref[slot] for slot in slots}
        accs = {slot: None for slot in slots}

        def soft_log(y):
            return jnp.minimum(y, 0.0) - jnp.log2(1.0 + jnp.exp2(-jnp.abs(y)))

        def rows_of(kb):
            start = kb * BLK
            return pl.ds(start if isinstance(start, int) else pl.multiple_of(start, BLK), BLK)

        def front(n):
            slot, kb, bias, h = chains[n]
            cols = slice((h // 2) * LANES, (h // 2 + 1) * LANES)
            y = _dot_nt(q_heads[slot], k_ref[rows_of(kb), cols])
            if bias is not None:
                y = y + bias
            log_1mb = soft_log(y)
            ys[n], logs[n] = y, log_1mb.astype(BF16)
            totals[n] = jnp.sum(log_1mb, axis=1, keepdims=True)

        def back(n):
            slot, kb, _, h = chains[n]
            cols = slice((h // 2) * LANES, (h // 2 + 1) * LANES)
            incl = sums.pop(n) if carries[slot] is None else sums.pop(n) + carries[slot]
            a = jnp.exp2(incl - ys.pop(n)).astype(BF16)
            pv = _dot(a, v_ref[rows_of(kb), cols])
            carries[slot] = totals[n] if carries[slot] is None else carries[slot] + totals[n]
            accs[slot] = pv if accs[slot] is None else accs[slot] + pv

        for t in range(len(chains) + 2):
            if t < len(chains):
                front(t)
            if 0 <= t - 1 < len(chains):
                sums[t - 1] = _dot(logs.pop(t - 1), suffix)
            if 0 <= t - 2 < len(chains):
                back(t - 2)
        for slot in slots:
            carry_ref[slot] = carries[slot]
            if first:
                acc_ref[slot] = accs[slot]
            else:
                acc_ref[slot] += accs[slot]

    def valid_bias(kb):
        return jnp.where(kb * BLK + key_col < PAD, MASK_BIG, 0.0).astype(F32)

    r = lax.broadcasted_iota(jnp.int32, (BLK, BLK), 0)
    s = lax.broadcasted_iota(jnp.int32, (BLK, BLK), 1)
    causal = jnp.where(s < r, 0.0, MASK_BIG).astype(F32)

    if meta_only:
        step([(0, 0, causal + valid_bias(0))], True)
    else:
        first_q = QUERY_BLOCKS * step_id + 1
        blocks = []
        for qi in range(QUERY_BLOCKS):
            prev_bias = valid_bias(first_q - 1) if qi == 0 else None
            blocks += [(qi, first_q + qi, causal), (qi, first_q + qi - 1, prev_bias)]
        step(blocks, True)

        for qi in range(QUERY_BLOCKS):
            def decayed(qi=qi):
                top = carry_ref[qi * SB_HEADS]
                for h in range(1, SB_HEADS):
                    top = jnp.maximum(top, carry_ref[qi * SB_HEADS + h])
                return jnp.max(top) < UNDERFLOW_LOG2

            def live(state):
                kb, done = state
                return jnp.logical_and(kb >= 0, jnp.logical_not(done))

            def body(state, qi=qi, decayed=decayed):
                kb, _ = state
                step([(qi, kb, valid_bias(kb))], False)
                return kb - 1, decayed()

            lax.while_loop(live, body, (first_q + qi - 2, decayed()))
    for qi in range(q_all.shape[0] // BLK):
        for p in range(pairs):
            slot = qi * SB_HEADS + 2 * p
            out = jnp.where(lane < SB_DH, acc_ref[slot], acc_ref[slot + 1]).astype(BF16)
            rows = slice(None) if meta_only else slice(qi * BLK, (qi + 1) * BLK)
            o_ref[rows, p * LANES:(p + 1) * LANES] = out[PAD:] if meta_only else out


def _stick_breaking(sq, sk, sv, batch):
    rows, width = sq.shape
    assert PAD <= BLK and width == SB_HEADS * SB_DH
    real_rows = rows - N_META
    per_seq = real_rows // BLK // batch
    seq = per_seq * BLK
    meta_spec = pl.BlockSpec((N_META, width), lambda b, i: (real_rows // N_META, 0))
    out_shape = jax.ShapeDtypeStruct((rows, width), BF16)

    def scratch(key_rows, query_blocks):
        slots = query_blocks * SB_HEADS
        return [pltpu.VMEM((key_rows, width), BF16), pltpu.VMEM((key_rows, width), BF16),
                pltpu.VMEM((slots, BLK, LANES), F32), pltpu.VMEM((slots, BLK, 1), F32)]

    out = pl.pallas_call(
        functools.partial(_sb_kernel, meta_only=True),
        grid=(1, 1),
        in_specs=[meta_spec] * 3,
        out_specs=meta_spec,
        out_shape=out_shape,
        scratch_shapes=scratch(BLK, 1),
        compiler_params=_params("arbitrary", "arbitrary"),
        name="stick_breaking_meta",
    )(sq, sk, sv)
    assert per_seq % QUERY_BLOCKS == 0
    steps = per_seq // QUERY_BLOCKS
    q_spec = pl.BlockSpec((QUERY_BLOCKS * BLK, width), lambda b, i: (b * steps + i, 0))
    real_spec = pl.BlockSpec((seq, width), lambda b, i: (b, 0))
    return pl.pallas_call(
        functools.partial(_sb_kernel, meta_only=False),
        grid=(batch, steps),
        in_specs=[q_spec, meta_spec, meta_spec, real_spec, real_spec, pl.BlockSpec(memory_space=pl.ANY)],
        out_specs=q_spec,
        out_shape=out_shape,
        input_output_aliases={5: 0},
        scratch_shapes=scratch(seq + BLK, QUERY_BLOCKS),
        compiler_params=_params("parallel", "arbitrary"),
        name="stick_breaking",
    )(sq, sk, sv, sk, sv, out)


def _merge_kernel(x_ref, g_ref, wrg_ref, wga0_ref, wga1_ref, wgb0_ref, wgb1_ref, gn_ref,
                  or_ref, os_ref, wr_ref, ws_ref, wo_ref, o_ref):
    x = x_ref[...]
    u = _rmsnorm(x, g_ref[...]).astype(BF16)

    def gate(*w_refs):
        return jnp.concatenate([_dot(u, w[...].astype(BF16)) for w in w_refs], axis=1)

    rg = gate(wrg_ref)
    y_sb = _dot(os_ref[...], ws_ref[...].astype(BF16))
    gb = _sigmoid(gate(wgb0_ref, wgb1_ref)) * y_sb
    ga = _sigmoid(gate(wga0_ref, wga1_ref))
    gated = (rg * _sigmoid(rg)) * (or_ref[...].astype(F32) * gn_ref[...])
    y_ret = _dot(gated.astype(BF16), wr_ref[...].astype(BF16))
    y = ga * y_ret + gb
    o_ref[...] = x + _dot(y.astype(BF16), wo_ref[...].astype(BF16))


def _merge(h, layer, g, w_in, gn, o_r, o_s, w_ret, w_sb, w_out, *, tile, out_rows):
    def in_spec(width):
        return pl.BlockSpec((tile, width), lambda i: (i, 0))

    half = D_MODEL // 2
    assert C_RG % D_MODEL == 0 and C_GA % half == 0
    ret_w, sb_w = RET_HEADS * RET_DV, SB_HEADS * SB_DH
    return pl.pallas_call(
        _merge_kernel,
        grid=(out_rows // tile,),
        in_specs=[
            in_spec(D_MODEL), _layer_spec(layer, (1, D_MODEL)),
            _layer_spec(layer, (D_MODEL, D_MODEL), col_block=C_RG // D_MODEL),
            _layer_spec(layer, (D_MODEL, half), col_block=C_GA // half),
            _layer_spec(layer, (D_MODEL, half), col_block=C_GA // half + 1),
            _layer_spec(layer, (D_MODEL, half), col_block=C_GB // half),
            _layer_spec(layer, (D_MODEL, half), col_block=C_GB // half + 1),
            _layer_spec(layer, (1, ret_w)), in_spec(ret_w), in_spec(sb_w),
            _layer_spec(layer, (ret_w, D_MODEL)), _layer_spec(layer, (sb_w, D_MODEL)),
            _layer_spec(layer, (D_MODEL, D_MODEL)),
        ],
        out_specs=pl.BlockSpec((tile, D_MODEL), lambda i: (i, 0)),
        out_shape=jax.ShapeDtypeStruct((out_rows, D_MODEL), F32),
        compiler_params=_params("parallel"),
        name="mixer_merge",
    )(h, g, w_in, w_in, w_in, w_in, w_in, gn, o_r, o_s, w_ret, w_sb, w_out)


def kernel(x, meta, ffn1_norm, ffn1_w_gu, ffn1_w_down, mix_norm, w_in, ret_gn, w_ret_proj,
           w_sb_proj, w_out, ffn2_norm, ffn2_w_gu, ffn2_w_down, final_norm):
    batch, seq, d = x.shape
    real_rows = batch * seq
    rows = real_rows + N_META
    assert d == D_MODEL and seq % BLK == 0 and real_rows % ROW_TILE == 0 and rows % MID_TILE == 0
    assert meta.shape == (N_META, d) and w_in.shape == (DEPTH, d, C_END)
    assert ffn1_w_gu.shape == (DEPTH, d, 2 * D_FF) and ffn1_w_down.shape == (DEPTH, D_FF, d)

    f32 = np.float32
    pos = np.concatenate([np.tile(np.arange(seq) + N_META, batch), np.arange(N_META)]).astype(f32)
    freqs = f32(ROPE_BASE) ** (-np.arange(0, RET_DK, 2, dtype=f32) / f32(RET_DK))
    ang = pos[:, None] * freqs[None, :]
    cos_h, sin_h = np.cos(ang).astype(f32), np.sin(ang).astype(f32)
    cos_t = jnp.asarray(np.concatenate([cos_h, cos_h] * 2, axis=1))
    sin_t = jnp.asarray(np.concatenate([-sin_h, sin_h] * 2, axis=1))
    log_gamma = np.log(f32(1.0) - f32(2.0) ** (-5.0 - np.arange(RET_HEADS, dtype=f32))).astype(f32)
    lg_tab = jnp.asarray(np.broadcast_to(log_gamma.reshape(RET_HEADS // 2, 2, 1), (RET_HEADS // 2, 2, BLK)))

    gains = lambda t: t.reshape(t.shape[0], 1, t.shape[1])
    final_gain = final_norm.reshape(1, d)
    for l in range(DEPTH):
        last = l == DEPTH - 1
        ffn1 = functools.partial(_ffn, layer=l, g=gains(ffn1_norm), w_gu=ffn1_w_gu, w_down=ffn1_w_down,
                                 gf=final_gain, final_norm=False)
        if l == 0:
            h = ffn1(x.reshape(real_rows, d), tile=ROW_TILE, out_rows=rows)
            h = ffn1(meta.astype(x.dtype), tile=N_META, out_rows=rows,
                     out_offset=real_rows // N_META, filled=(h,))
        else:
            h = ffn1(h, tile=MID_TILE)
        rq, rk, rv, sq, sk, sv = _proj(h, l, gains(mix_norm), w_in, cos_t, sin_t, tile=MID_TILE)
        o_r = _retention(lg_tab, rq, rk, rv, batch)
        o_s = _stick_breaking(sq, sk, sv, batch)
        h = _merge(h, l, gains(mix_norm), w_in, gains(ret_gn), o_r, o_s, w_ret_proj, w_sb_proj, w_out,
                   tile=MID_TILE, out_rows=rows)
        h = _ffn(h, l, gains(ffn2_norm), ffn2_w_gu, ffn2_w_down, final_gain, final_norm=last,
                 tile=MID_TILE, out_rows=real_rows if last else rows)
    return h.reshape(batch, seq, d)
```

```python
import functools

import jax
import jax.numpy as jnp
import numpy as np
from jax import lax
from jax.experimental import pallas as pl
from jax.experimental.pallas import tpu as pltpu

D_MODEL = 1024
DEPTH = 2
N_META = 16
RET_HEADS = 8
RET_DK = 64
RET_DV = 128
SB_HEADS = 8
SB_DH = 64
D_FF = 2816
ROPE_BASE = 10000.0
EPS = 1e-6

LANES = 128
BLK = 256
PAD = BLK - N_META
FF_CHUNK = 256
ROW_TILE = 512
MID_TILE = 656
QUERY_BLOCKS = 2
RET_CHUNKS = 2
MASK_BIG = 1e30
LOG2E = 1.4426950408889634
UNDERFLOW_LOG2 = -160.0
VMEM_LIMIT = 56 * 1024 * 1024

_COLS = np.cumsum([0, RET_HEADS * RET_DK, RET_HEADS * RET_DK, RET_HEADS * RET_DV, RET_HEADS * RET_DV,
                   SB_HEADS * SB_DH, SB_HEADS * SB_DH, SB_HEADS * SB_DH, D_MODEL, D_MODEL])
C_RQ, C_RK, C_RV, C_RG, C_SQ, C_SK, C_SV, C_GA, C_GB, C_END = (int(c) for c in _COLS)

F32 = jnp.float32
BF16 = jnp.bfloat16


def _dot(a, b):
    return jnp.dot(a, b, preferred_element_type=F32)


def _dot_nt(a, b):
    return lax.dot_general(a, b, (((1,), (1,)), ((), ())), preferred_element_type=F32)


def _rmsnorm(x, g):
    r = lax.rsqrt(jnp.mean(x * x, axis=-1, keepdims=True) + EPS)
    return (x * r) * g


def _sigmoid(x):
    return 1.0 / (1.0 + jnp.exp(-x))


def _params(*sem):
    return pltpu.CompilerParams(dimension_semantics=sem, vmem_limit_bytes=VMEM_LIMIT)


def _resident_spec(block, index_map):
    return pl.BlockSpec(block, index_map, pipeline_mode=pl.Buffered(1))


def _layer_spec(layer, shape, col_block=0):
    return _resident_spec((None,) + shape, lambda *_: (layer, 0, col_block))


def _ffn_kernel(x_ref, g_ref, wgu_ref, wd_ref, gf_ref, *rest, final_norm):
    o_ref, gate_ref = rest[-2:]
    x = x_ref[...]
    u = _rmsnorm(x, g_ref[...]).astype(BF16)
    for c in range(0, D_FF, FF_CHUNK):
        a = _dot(u, wgu_ref[:, c:c + FF_CHUNK].astype(BF16))
        b = _dot(u, wgu_ref[:, D_FF + c:D_FF + c + FF_CHUNK].astype(BF16))
        gate_ref[:, c:c + FF_CHUNK] = (0.5 * a * _sigmoid(a) * b).astype(BF16)
    y = x + _dot(gate_ref[...], wd_ref[...].astype(BF16))
    if final_norm:
        y = _rmsnorm(y, gf_ref[...])
    o_ref[...] = y


def _ffn(h, layer, g, w_gu, w_down, gf, *, final_norm, tile, out_rows=None, out_offset=0, filled=()):
    out_rows = h.shape[0] if out_rows is None else out_rows
    return pl.pallas_call(
        functools.partial(_ffn_kernel, final_norm=final_norm),
        grid=(h.shape[0] // tile,),
        in_specs=[
            pl.BlockSpec((tile, D_MODEL), lambda i: (i, 0)),
            _layer_spec(layer, (1, D_MODEL)),
            _layer_spec(layer, (D_MODEL, 2 * D_FF)),
            _layer_spec(layer, (D_FF, D_MODEL)),
            _resident_spec((1, D_MODEL), lambda i: (0, 0)),
        ] + [pl.BlockSpec(memory_space=pl.ANY)] * len(filled),
        out_specs=pl.BlockSpec((tile, D_MODEL), lambda i: (i + out_offset, 0)),
        out_shape=jax.ShapeDtypeStruct((out_rows, D_MODEL), F32),
        input_output_aliases={5: 0} if filled else {},
        scratch_shapes=[pltpu.VMEM((tile, D_FF), BF16)],
        compiler_params=_params("parallel"),
        name="ffn_final" if final_norm else "ffn",
    )(h, g, w_gu, w_down, gf, *filled)


def _proj_kernel(x_ref, g_ref, wr_ref, ws_ref, cos_ref, sin_ref,
                 rq_ref, rk_ref, rv_ref, sq_ref, sk_ref, sv_ref):
    u = _rmsnorm(x_ref[...], g_ref[...]).astype(BF16)
    qk_w = RET_HEADS * RET_DK
    cos = jnp.concatenate([cos_ref[...]] * (qk_w // LANES), axis=1)
    sin = jnp.concatenate([sin_ref[...]] * (qk_w // LANES), axis=1)
    lane = lax.broadcasted_iota(jnp.int32, (x_ref.shape[0], qk_w), 1)
    first_half = (lane % RET_DK) < (RET_DK // 2)

    def rotary(t):
        partner = jnp.where(first_half, pltpu.roll(t, qk_w - RET_DK // 2, 1),
                            pltpu.roll(t, RET_DK // 2, 1))
        return t * cos + partner * sin

    def proj(w_ref, lo, hi):
        return _dot(u, w_ref[:, lo:hi].astype(BF16))

    rq_ref[...] = rotary(proj(wr_ref, C_RQ, C_RK)).astype(BF16)
    rk_ref[...] = (rotary(proj(wr_ref, C_RK, C_RV)) * RET_DK ** -0.5).astype(BF16)
    rv_ref[...] = proj(wr_ref, C_RV, C_RG).astype(BF16)
    sq_ref[...] = (proj(ws_ref, 0, C_SK - C_SQ) * (-LOG2E * SB_DH ** -0.5)).astype(BF16)
    sk_ref[...] = proj(ws_ref, C_SK - C_SQ, C_SV - C_SQ).astype(BF16)
    sv_ref[...] = proj(ws_ref, C_SV - C_SQ, C_GA - C_SQ).astype(BF16)


def _proj(h, layer, g, w_in, cos_t, sin_t, *, tile):
    rows = h.shape[0]

    def row_spec(width):
        return pl.BlockSpec((tile, width), lambda i: (i, 0))

    tab_spec = row_spec(LANES)
    widths = (C_RK - C_RQ, C_RV - C_RK, C_RG - C_RV, C_SK - C_SQ, C_SV - C_SK, C_GA - C_SV)
    sb_cols = C_GA - C_SQ
    assert C_SQ % sb_cols == 0
    return pl.pallas_call(
        _proj_kernel,
        grid=(rows // tile,),
        in_specs=[row_spec(D_MODEL), _layer_spec(layer, (1, D_MODEL)),
                  _layer_spec(layer, (D_MODEL, C_RG)),
                  _layer_spec(layer, (D_MODEL, sb_cols), col_block=C_SQ // sb_cols),
                  tab_spec, tab_spec],
        out_specs=[row_spec(w) for w in widths],
        out_shape=[jax.ShapeDtypeStruct((rows, w), BF16) for w in widths],
        compiler_params=_params("parallel"),
        name="mixer_proj",
    )(h, g, w_in, w_in, cos_t, sin_t)


def _meta_chunk(m_ref):
    m = m_ref[...]
    return jnp.concatenate([jnp.zeros((PAD, m.shape[1]), m.dtype), m], axis=0)


def _ret_kernel(lg_ref, *refs, meta_only):
    o_ref, state_ref, dec_ref, kdec_ref, qdec_ref, sdec_ref = refs[-6:]
    c = pl.program_id(1)
    pairs = RET_HEADS // 2

    @pl.when(c == 0)
    def _init():
        t = lax.broadcasted_iota(jnp.int32, (BLK, BLK), 0)
        s = lax.broadcasted_iota(jnp.int32, (BLK, BLK), 1)
        diff = (t - s).astype(F32)
        lane = lax.broadcasted_iota(jnp.int32, (BLK, LANES), 1)
        pos = lax.broadcasted_iota(jnp.int32, (BLK, LANES), 0).astype(F32)
        col = lax.broadcasted_iota(jnp.int32, (BLK, 2 * RET_DV), 1)
        posv = lax.broadcasted_iota(jnp.int32, (BLK, 2 * RET_DV), 0).astype(F32)
        srow = lax.broadcasted_iota(jnp.int32, (LANES, 2 * RET_DV), 0)
        scol = lax.broadcasted_iota(jnp.int32, (LANES, 2 * RET_DV), 1)
        own = (srow // RET_DK) == (scol // RET_DV)
        for p in range(pairs):
            lg = lg_ref[p]
            for i in range(2):
                dec_ref[2 * p + i] = jnp.where(
                    diff >= 0, jnp.exp(lg[i:i + 1, :] * jnp.maximum(diff, 0.0)), 0.0)
            lg_k = jnp.where(lane < RET_DK, lg[0:1, :LANES], lg[1:2, :LANES])
            kdec_ref[p] = jnp.exp(lg_k * (BLK - 1.0 - pos))
            lg_v = jnp.where(col < RET_DV, lg[0:1, :], lg[1:2, :])
            qdec_ref[p] = jnp.exp(lg_v * (posv + 1.0))
            lg_s = jnp.where(scol < RET_DV, lg[0:1, :], lg[1:2, :])
            sdec_ref[2 * p] = jnp.where(own, jnp.exp(lg_s * float(BLK)), 0.0)
            sdec_ref[2 * p + 1] = jnp.where(own, 1.0, 0.0)
        state_ref[...] = jnp.zeros_like(state_ref)

    def decayed_keys_t(k_pair, p):
        return (k_pair.astype(F32) * kdec_ref[p]).T.astype(BF16)

    def chunk(q_all, k_all, v_all, out_rows=slice(None)):
        lane = lax.broadcasted_iota(jnp.int32, (BLK, LANES), 1)
        qs, ks, crosses, scores, outs = {}, {}, {}, {}, {}

        def pair_start(p):
            qs[p] = q_all[:, p * LANES:(p + 1) * LANES]
            ks[p] = k_all[:, p * LANES:(p + 1) * LANES]
            crosses[p] = _dot(qs[p], state_ref[p].astype(BF16)) * qdec_ref[p]

        def head_scores(head):
            p, i = divmod(head, 2)
            in_head = (lane < RET_DK) if i == 0 else (lane >= RET_DK)
            qi = jnp.where(in_head, qs[p], jnp.zeros_like(qs[p]))
            scores[head] = (_dot_nt(qi, ks[p]) * dec_ref[head]).astype(BF16)

        def head_values(head):
            p, i = divmod(head, 2)
            o = _dot(scores.pop(head), v_all[:, head * RET_DV:(head + 1) * RET_DV])
            outs[head] = o + crosses[p][:, i * RET_DV:(i + 1) * RET_DV]

        def head_norm(head):
            o = outs.pop(head)
            mu = jnp.mean(o, axis=-1, keepdims=True)
            d = o - mu
            var = jnp.mean(d * d, axis=-1, keepdims=True)
            normed = (d * lax.rsqrt(var + EPS)).astype(BF16)
            o_ref[out_rows, head * RET_DV:(head + 1) * RET_DV] = normed[PAD:] if meta_only else normed

        def pair_state(p):
            v = v_all[:, 2 * p * RET_DV:2 * (p + 1) * RET_DV]
            state_ref[p] = (state_ref[p] * sdec_ref[2 * p]
                            + _dot(decayed_keys_t(ks[p], p), v) * sdec_ref[2 * p + 1])

        for t in range(RET_HEADS + 2):
            if t < RET_HEADS:
                if t % 2 == 0:
                    pair_start(t // 2)
                head_scores(t)
            if 0 <= t - 1 < RET_HEADS:
                head_values(t - 1)
            if 0 <= t - 2 < RET_HEADS:
                head_norm(t - 2)
                if (t - 2) % 2 == 1:
                    pair_state((t - 2) // 2)

    if meta_only:
        chunk(*(_meta_chunk(r) for r in refs[:3]))
    else:
        @pl.when(c == 0)
        def _replay_meta_chunk():
            k_m, v_m = _meta_chunk(refs[3]), _meta_chunk(refs[4])
            for p in range(pairs):
                kd_t = decayed_keys_t(k_m[:, p * LANES:(p + 1) * LANES], p)
                v = v_m[:, 2 * p * RET_DV:2 * (p + 1) * RET_DV]
                state_ref[p] = _dot(kd_t, v) * sdec_ref[2 * p + 1]
            o_ref[...] = jnp.zeros_like(o_ref)

        @pl.when(c > 0)
        def _real_chunks():
            for i in range(refs[0].shape[0] // BLK):
                rows = slice(i * BLK, (i + 1) * BLK)
                chunk(refs[0][rows, :], refs[1][rows, :], refs[2][rows, :], rows)


def _retention(lg_tab, rq, rk, rv, batch):
    rows = rq.shape[0]
    pairs = RET_HEADS // 2
    qk_w, v_w = RET_HEADS * RET_DK, RET_HEADS * RET_DV
    real_rows = rows - N_META
    per_seq = real_rows // BLK // batch
    meta_rows = lambda *_: (real_rows // N_META, 0)
    meta_specs = [pl.BlockSpec((N_META, w), meta_rows) for w in (qk_w, qk_w, v_w)]
    lg_spec = pl.BlockSpec((pairs, 2, BLK), lambda b, c: (0, 0, 0))
    scratch = [
        pltpu.VMEM((pairs, LANES, 2 * RET_DV), F32),
        pltpu.VMEM((RET_HEADS, BLK, BLK), F32),
        pltpu.VMEM((pairs, BLK, LANES), F32),
        pltpu.VMEM((pairs, BLK, 2 * RET_DV), F32),
        pltpu.VMEM((2 * pairs, LANES, 2 * RET_DV), F32),
    ]
    out_shape = jax.ShapeDtypeStruct((rows, v_w), BF16)
    out = pl.pallas_call(
        functools.partial(_ret_kernel, meta_only=True),
        grid=(1, 1),
        in_specs=[lg_spec] + meta_specs,
        out_specs=pl.BlockSpec((N_META, v_w), meta_rows),
        out_shape=out_shape,
        scratch_shapes=scratch,
        compiler_params=_params("arbitrary", "arbitrary"),
        name="retention_meta",
    )(lg_tab, rq, rk, rv)

    assert per_seq % RET_CHUNKS == 0
    steps = per_seq // RET_CHUNKS
    step_rows = RET_CHUNKS * BLK

    def real_rows_of(b, c):
        return b * steps + jnp.maximum(c - 1, 0), 0

    return pl.pallas_call(
        functools.partial(_ret_kernel, meta_only=False),
        grid=(batch, steps + 1),
        in_specs=[lg_spec] + [pl.BlockSpec((step_rows, w), real_rows_of) for w in (qk_w, qk_w, v_w)]
        + meta_specs[1:] + [pl.BlockSpec(memory_space=pl.ANY)],
        out_specs=pl.BlockSpec((step_rows, v_w), real_rows_of),
        out_shape=out_shape,
        input_output_aliases={6: 0},
        scratch_shapes=scratch,
        compiler_params=_params("parallel", "arbitrary"),
        name="retention",
    )(lg_tab, rq, rk, rv, rk, rv, out)


def _sb_kernel(q_ref, km_ref, vm_ref, *rest, meta_only):
    o_ref, k_ref, v_ref, acc_ref, carry_ref = rest[-5:]
    step_id = pl.program_id(1)

    @pl.when(step_id == 0)
    def _assemble_sequence():
        k_ref[0:BLK] = _meta_chunk(km_ref)
        v_ref[0:BLK] = _meta_chunk(vm_ref)
        if not meta_only:
            k_ref[BLK:] = rest[0][...]
            v_ref[BLK:] = rest[1][...]

    pairs = SB_HEADS // 2
    lane = lax.broadcasted_iota(jnp.int32, (BLK, LANES), 1)
    q_all = _meta_chunk(q_ref) if meta_only else q_ref[...]
    q_heads = []
    for qi in range(q_all.shape[0] // BLK):
        for p in range(pairs):
            q = q_all[qi * BLK:(qi + 1) * BLK, p * LANES:(p + 1) * LANES]
            zero = jnp.zeros_like(q)
            q_heads += [jnp.where(lane < SB_DH, q, zero), jnp.where(lane >= SB_DH, q, zero)]
    j = lax.broadcasted_iota(jnp.int32, (2 * BLK, BLK), 0) % BLK
    c = lax.broadcasted_iota(jnp.int32, (2 * BLK, BLK), 1)
    suffix = jnp.where(j >= c, 1.0, 0.0).astype(BF16)
    key_col = lax.broadcasted_iota(jnp.int32, (1, BLK), 1)

    def step(blocks, first):
        chains = [(qi * SB_HEADS + h, kb, bias, h) for qi, kb, bias in blocks for h in range(SB_HEADS)]
        slots = sorted({slot for slot, _, _, _ in chains})
        ys, logs, sums = {}, {}, {}
        carries = {slot: None if first else carry_ref[slot] for slot in slots}
        accs = {slot: None for slot in slots}

        def soft_log(y):
            return jnp.minimum(y, 0.0) - jnp.log2(1.0 + jnp.exp2(-jnp.abs(y)))

        def rows_of(kb):
            start = kb * BLK
            return pl.ds(start if isinstance(start, int) else pl.multiple_of(start, BLK), BLK)

        def front(n):
            slot, kb, bias, h = chains[n]
            cols = slice((h // 2) * LANES, (h // 2 + 1) * LANES)
            y = _dot_nt(q_heads[slot], k_ref[rows_of(kb), cols])
            if bias is not None:
                y = y + bias
            log_1mb = soft_log(y)
            hi = log_1mb.astype(BF16)
            lo = (log_1mb - hi.astype(F32)).astype(BF16)
            ys[n], logs[n] = y, jnp.concatenate([hi, lo], axis=1)

        def back(n):
            slot, kb, _, h = chains[n]
            cols = slice((h // 2) * LANES, (h // 2 + 1) * LANES)
            incl = sums.pop(n) if carries[slot] is None else sums.pop(n) + carries[slot]
            a = jnp.exp2(incl - ys.pop(n)).astype(BF16)
            pv = _dot(a, v_ref[rows_of(kb), cols])
            carries[slot] = incl[:, 0:1]
            accs[slot] = pv if accs[slot] is None else accs[slot] + pv

        for t in range(len(chains) + 2):
            if t < len(chains):
                front(t)
            if 0 <= t - 1 < len(chains):
                sums[t - 1] = _dot(logs.pop(t - 1), suffix)
            if 0 <= t - 2 < len(chains):
                back(t - 2)
        for slot in slots:
            carry_ref[slot] = carries[slot]
            if first:
                acc_ref[slot] = accs[slot]
            else:
                acc_ref[slot] += accs[slot]

    def valid_bias(kb):
        return jnp.where(kb * BLK + key_col < PAD, MASK_BIG, 0.0).astype(F32)

    r = lax.broadcasted_iota(jnp.int32, (BLK, BLK), 0)
    s = lax.broadcasted_iota(jnp.int32, (BLK, BLK), 1)
    causal = jnp.where(s < r, 0.0, MASK_BIG).astype(F32)

    if meta_only:
        step([(0, 0, causal + valid_bias(0))], True)
    else:
        first_q = QUERY_BLOCKS * step_id + 1
        blocks = []
        for qi in range(QUERY_BLOCKS):
            prev_bias = valid_bias(first_q - 1) if qi == 0 else None
            blocks += [(qi, first_q + qi, causal), (qi, first_q + qi - 1, prev_bias)]
        step(blocks, True)

        for qi in range(QUERY_BLOCKS):
            def decayed(qi=qi):
                top = carry_ref[qi * SB_HEADS]
                for h in range(1, SB_HEADS):
                    top = jnp.maximum(top, carry_ref[qi * SB_HEADS + h])
                return jnp.max(top) < UNDERFLOW_LOG2

            def live(state):
                kb, done = state
                return jnp.logical_and(kb >= 0, jnp.logical_not(done))

            def body(state, qi=qi, decayed=decayed):
                kb, _ = state
                step([(qi, kb, valid_bias(kb))], False)
                return kb - 1, decayed()

            lax.while_loop(live, body, (first_q + qi - 2, decayed()))
    for qi in range(q_all.shape[0] // BLK):
        for p in range(pairs):
            slot = qi * SB_HEADS + 2 * p
            out = jnp.where(lane < SB_DH, acc_ref[slot], acc_ref[slot + 1]).astype(BF16)
            rows = slice(None) if meta_only else slice(qi * BLK, (qi + 1) * BLK)
            o_ref[rows, p * LANES:(p + 1) * LANES] = out[PAD:] if meta_only else out


def _stick_breaking(sq, sk, sv, batch):
    rows, width = sq.shape
    assert PAD <= BLK and width == SB_HEADS * SB_DH
    real_rows = rows - N_META
    per_seq = real_rows // BLK // batch
    seq = per_seq * BLK
    meta_spec = pl.BlockSpec((N_META, width), lambda b, i: (real_rows // N_META, 0))
    out_shape = jax.ShapeDtypeStruct((rows, width), BF16)

    def scratch(key_rows, query_blocks):
        slots = query_blocks * SB_HEADS
        return [pltpu.VMEM((key_rows, width), BF16), pltpu.VMEM((key_rows, width), BF16),
                pltpu.VMEM((slots, BLK, LANES), F32), pltpu.VMEM((slots, BLK, 1), F32)]

    out = pl.pallas_call(
        functools.partial(_sb_kernel, meta_only=True),
        grid=(1, 1),
        in_specs=[meta_spec] * 3,
        out_specs=meta_spec,
        out_shape=out_shape,
        scratch_shapes=scratch(BLK, 1),
        compiler_params=_params("arbitrary", "arbitrary"),
        name="stick_breaking_meta",
    )(sq, sk, sv)
    assert per_seq % QUERY_BLOCKS == 0
    steps = per_seq // QUERY_BLOCKS
    q_spec = pl.BlockSpec((QUERY_BLOCKS * BLK, width), lambda b, i: (b * steps + i, 0))
    real_spec = pl.BlockSpec((seq, width), lambda b, i: (b, 0))
    return pl.pallas_call(
        functools.partial(_sb_kernel, meta_only=False),
        grid=(batch, steps),
        in_specs=[q_spec, meta_spec, meta_spec, real_spec, real_spec, pl.BlockSpec(memory_space=pl.ANY)],
        out_specs=q_spec,
        out_shape=out_shape,
        input_output_aliases={5: 0},
        scratch_shapes=scratch(seq + BLK, QUERY_BLOCKS),
        compiler_params=_params("parallel", "arbitrary"),
        name="stick_breaking",
    )(sq, sk, sv, sk, sv, out)


def _merge_kernel(x_ref, g_ref, wrg_ref, wga0_ref, wga1_ref, wgb0_ref, wgb1_ref, gn_ref,
                  or_ref, os_ref, wr_ref, ws_ref, wo_ref, o_ref):
    x = x_ref[...]
    u = _rmsnorm(x, g_ref[...]).astype(BF16)

    def gate(*w_refs):
        return jnp.concatenate([_dot(u, w[...].astype(BF16)) for w in w_refs], axis=1)

    rg = gate(wrg_ref)
    y_sb = _dot(os_ref[...], ws_ref[...].astype(BF16))
    gb = _sigmoid(gate(wgb0_ref, wgb1_ref)) * y_sb
    ga = _sigmoid(gate(wga0_ref, wga1_ref))
    gated = (rg * _sigmoid(rg)) * (or_ref[...].astype(F32) * gn_ref[...])
    y_ret = _dot(gated.astype(BF16), wr_ref[...].astype(BF16))
    y = ga * y_ret + gb
    o_ref[...] = x + _dot(y.astype(BF16), wo_ref[...].astype(BF16))


def _merge(h, layer, g, w_in, gn, o_r, o_s, w_ret, w_sb, w_out, *, tile, out_rows):
    def in_spec(width):
        return pl.BlockSpec((tile, width), lambda i: (i, 0))

    half = D_MODEL // 2
    assert C_RG % D_MODEL == 0 and C_GA % half == 0
    ret_w, sb_w = RET_HEADS * RET_DV, SB_HEADS * SB_DH
    return pl.pallas_call(
        _merge_kernel,
        grid=(out_rows // tile,),
        in_specs=[
            in_spec(D_MODEL), _layer_spec(layer, (1, D_MODEL)),
            _layer_spec(layer, (D_MODEL, D_MODEL), col_block=C_RG // D_MODEL),
            _layer_spec(layer, (D_MODEL, half), col_block=C_GA // half),
            _layer_spec(layer, (D_MODEL, half), col_block=C_GA // half + 1),
            _layer_spec(layer, (D_MODEL, half), col_block=C_GB // half),
            _layer_spec(layer, (D_MODEL, half), col_block=C_GB // half + 1),
            _layer_spec(layer, (1, ret_w)), in_spec(ret_w), in_spec(sb_w),
            _layer_spec(layer, (ret_w, D_MODEL)), _layer_spec(layer, (sb_w, D_MODEL)),
            _layer_spec(layer, (D_MODEL, D_MODEL)),
        ],
        out_specs=pl.BlockSpec((tile, D_MODEL), lambda i: (i, 0)),
        out_shape=jax.ShapeDtypeStruct((out_rows, D_MODEL), F32),
        compiler_params=_params("parallel"),
        name="mixer_merge",
    )(h, g, w_in, w_in, w_in, w_in, w_in, gn, o_r, o_s, w_ret, w_sb, w_out)


def kernel(x, meta, ffn1_norm, ffn1_w_gu, ffn1_w_down, mix_norm, w_in, ret_gn, w_ret_proj,
           w_sb_proj, w_out, ffn2_norm, ffn2_w_gu, ffn2_w_down, final_norm):
    batch, seq, d = x.shape
    real_rows = batch * seq
    rows = real_rows + N_META
    assert d == D_MODEL and seq % BLK == 0 and real_rows % ROW_TILE == 0 and rows % MID_TILE == 0
    assert meta.shape == (N_META, d) and w_in.shape == (DEPTH, d, C_END)
    assert ffn1_w_gu.shape == (DEPTH, d, 2 * D_FF) and ffn1_w_down.shape == (DEPTH, D_FF, d)

    f32 = np.float32
    pos = np.concatenate([np.tile(np.arange(seq) + N_META, batch), np.arange(N_META)]).astype(f32)
    freqs = f32(ROPE_BASE) ** (-np.arange(0, RET_DK, 2, dtype=f32) / f32(RET_DK))
    ang = pos[:, None] * freqs[None, :]
    cos_h, sin_h = np.cos(ang).astype(f32), np.sin(ang).astype(f32)
    cos_t = jnp.asarray(np.concatenate([cos_h, cos_h] * 2, axis=1))
    sin_t = jnp.asarray(np.concatenate([-sin_h, sin_h] * 2, axis=1))
    log_gamma = np.log(f32(1.0) - f32(2.0) ** (-5.0 - np.arange(RET_HEADS, dtype=f32))).astype(f32)
    lg_tab = jnp.asarray(np.broadcast_to(log_gamma.reshape(RET_HEADS // 2, 2, 1), (RET_HEADS // 2, 2, BLK)))

    gains = lambda t: t.reshape(t.shape[0], 1, t.shape[1])
    final_gain = final_norm.reshape(1, d)
    for l in range(DEPTH):
        last = l == DEPTH - 1
        ffn1 = functools.partial(_ffn, layer=l, g=gains(ffn1_norm), w_gu=ffn1_w_gu, w_down=ffn1_w_down,
                                 gf=final_gain, final_norm=False)
        if l == 0:
            h = ffn1(x.reshape(real_rows, d), tile=ROW_TILE, out_rows=rows)
            h = ffn1(meta.astype(x.dtype), tile=N_META, out_rows=rows,
                     out_offset=real_rows // N_META, filled=(h,))
        else:
            h = ffn1(h, tile=MID_TILE)
        rq, rk, rv, sq, sk, sv = _proj(h, l, gains(mix_norm), w_in, cos_t, sin_t, tile=MID_TILE)
        o_r = _retention(lg_tab, rq, rk, rv, batch)
        o_s = _stick_breaking(sq, sk, sv, batch)
        h = _merge(h, l, gains(mix_norm), w_in, gains(ret_gn), o_r, o_s, w_ret_proj, w_sb_proj, w_out,
                   tile=MID_TILE, out_rows=rows)
        h = _ffn(h, l, gains(ffn2_norm), ffn2_w_gu, ffn2_w_down, final_gain, final_norm=last,
                 tile=MID_TILE, out_rows=real_rows if last else rows)
    return h.reshape(batch, seq, d)
```

```python
import functools

import jax
import jax.numpy as jnp
import numpy as np
from jax import lax
from jax.experimental import pallas as pl
from jax.experimental.pallas import tpu as pltpu

D_MODEL = 1024
DEPTH = 2
N_META = 16
RET_HEADS = 8
RET_DK = 64
RET_DV = 128
SB_HEADS = 8
SB_DH = 64
D_FF = 2816
ROPE_BASE = 10000.0
EPS = 1e-6

LANES = 128
BLK = 256
PAD = BLK - N_META
FF_CHUNK = 256
ROW_TILE = 512
MID_TILE = 656
QUERY_BLOCKS = 2
RET_CHUNKS = 2
MASK_BIG = 1e30
LOG2E = 1.4426950408889634
UNDERFLOW_LOG2 = -160.0
VMEM_LIMIT = 56 * 1024 * 1024

_COLS = np.cumsum([0, RET_HEADS * RET_DK, RET_HEADS * RET_DK, RET_HEADS * RET_DV, RET_HEADS * RET_DV,
                   SB_HEADS * SB_DH, SB_HEADS * SB_DH, SB_HEADS * SB_DH, D_MODEL, D_MODEL])
C_RQ, C_RK, C_RV, C_RG, C_SQ, C_SK, C_SV, C_GA, C_GB, C_END = (int(c) for c in _COLS)

F32 = jnp.float32
BF16 = jnp.bfloat16


def _dot(a, b):
    return jnp.dot(a, b, preferred_element_type=F32)


def _dot_nt(a, b):
    return lax.dot_general(a, b, (((1,), (1,)), ((), ())), preferred_element_type=F32)


def _rmsnorm(x, g):
    r = lax.rsqrt(jnp.mean(x * x, axis=-1, keepdims=True) + EPS)
    return (x * r) * g


def _sigmoid(x):
    return 1.0 / (1.0 + jnp.exp(-x))


def _params(*sem):
    return pltpu.CompilerParams(dimension_semantics=sem, vmem_limit_bytes=VMEM_LIMIT)


def _resident_spec(block, index_map):
    return pl.BlockSpec(block, index_map, pipeline_mode=pl.Buffered(1))


def _layer_spec(layer, shape, col_block=0):
    return _resident_spec((None,) + shape, lambda *_: (layer, 0, col_block))


def _ffn_kernel(x_ref, g_ref, wgu_ref, wd_ref, gf_ref, *rest, final_norm):
    o_ref, gate_ref = rest[-2:]
    x = x_ref[...]
    u = _rmsnorm(x, g_ref[...]).astype(BF16)
    for c in range(0, D_FF, FF_CHUNK):
        a = _dot(u, wgu_ref[:, c:c + FF_CHUNK].astype(BF16))
        b = _dot(u, wgu_ref[:, D_FF + c:D_FF + c + FF_CHUNK].astype(BF16))
        gate_ref[:, c:c + FF_CHUNK] = (0.5 * a * _sigmoid(a) * b).astype(BF16)
    y = x + _dot(gate_ref[...], wd_ref[...].astype(BF16))
    if final_norm:
        y = _rmsnorm(y, gf_ref[...])
    o_ref[...] = y


def _ffn(h, layer, g, w_gu, w_down, gf, *, final_norm, tile, out_rows=None, out_offset=0, filled=()):
    out_rows = h.shape[0] if out_rows is None else out_rows
    return pl.pallas_call(
        functools.partial(_ffn_kernel, final_norm=final_norm),
        grid=(h.shape[0] // tile,),
        in_specs=[
            pl.BlockSpec((tile, D_MODEL), lambda i: (i, 0)),
            _layer_spec(layer, (1, D_MODEL)),
            _layer_spec(layer, (D_MODEL, 2 * D_FF)),
            _layer_spec(layer, (D_FF, D_MODEL)),
            _resident_spec((1, D_MODEL), lambda i: (0, 0)),
        ] + [pl.BlockSpec(memory_space=pl.ANY)] * len(filled),
        out_specs=pl.BlockSpec((tile, D_MODEL), lambda i: (i + out_offset, 0)),
        out_shape=jax.ShapeDtypeStruct((out_rows, D_MODEL), F32),
        input_output_aliases={5: 0} if filled else {},
        scratch_shapes=[pltpu.VMEM((tile, D_FF), BF16)],
        compiler_params=_params("parallel"),
        name="ffn_final" if final_norm else "ffn",
    )(h, g, w_gu, w_down, gf, *filled)


def _proj_kernel(x_ref, g_ref, wr_ref, ws_ref, cos_ref, sin_ref,
                 rq_ref, rk_ref, rv_ref, sq_ref, sk_ref, sv_ref):
    u = _rmsnorm(x_ref[...], g_ref[...]).astype(BF16)
    qk_w = RET_HEADS * RET_DK
    cos = jnp.concatenate([cos_ref[...]] * (qk_w // LANES), axis=1)
    sin = jnp.concatenate([sin_ref[...]] * (qk_w // LANES), axis=1)
    lane = lax.broadcasted_iota(jnp.int32, (x_ref.shape[0], qk_w), 1)
    first_half = (lane % RET_DK) < (RET_DK // 2)

    def rotary(t):
        partner = jnp.where(first_half, pltpu.roll(t, qk_w - RET_DK // 2, 1),
                            pltpu.roll(t, RET_DK // 2, 1))
        return t * cos + partner * sin

    def proj(w_ref, lo, hi):
        return _dot(u, w_ref[:, lo:hi].astype(BF16))

    rq_ref[...] = rotary(proj(wr_ref, C_RQ, C_RK)).astype(BF16)
    rk_ref[...] = (rotary(proj(wr_ref, C_RK, C_RV)) * RET_DK ** -0.5).astype(BF16)
    rv_ref[...] = proj(wr_ref, C_RV, C_RG).astype(BF16)
    sq_ref[...] = (proj(ws_ref, 0, C_SK - C_SQ) * (-LOG2E * SB_DH ** -0.5)).astype(BF16)
    sk_ref[...] = proj(ws_ref, C_SK - C_SQ, C_SV - C_SQ).astype(BF16)
    sv_ref[...] = proj(ws_ref, C_SV - C_SQ, C_GA - C_SQ).astype(BF16)


def _proj(h, layer, g, w_in, cos_t, sin_t, *, tile):
    rows = h.shape[0]

    def row_spec(width):
        return pl.BlockSpec((tile, width), lambda i: (i, 0))

    tab_spec = row_spec(LANES)
    widths = (C_RK - C_RQ, C_RV - C_RK, C_RG - C_RV, C_SK - C_SQ, C_SV - C_SK, C_GA - C_SV)
    sb_cols = C_GA - C_SQ
    assert C_SQ % sb_cols == 0
    return pl.pallas_call(
        _proj_kernel,
        grid=(rows // tile,),
        in_specs=[row_spec(D_MODEL), _layer_spec(layer, (1, D_MODEL)),
                  _layer_spec(layer, (D_MODEL, C_RG)),
                  _layer_spec(layer, (D_MODEL, sb_cols), col_block=C_SQ // sb_cols),
                  tab_spec, tab_spec],
        out_specs=[row_spec(w) for w in widths],
        out_shape=[jax.ShapeDtypeStruct((rows, w), BF16) for w in widths],
        compiler_params=_params("parallel"),
        name="mixer_proj",
    )(h, g, w_in, w_in, cos_t, sin_t)


def _meta_chunk(m_ref):
    m = m_ref[...]
    return jnp.concatenate([jnp.zeros((PAD, m.shape[1]), m.dtype), m], axis=0)


def _ret_kernel(lg_ref, *refs, meta_only):
    o_ref, state_ref, dec_ref, kdec_ref, qdec_ref, sdec_ref = refs[-6:]
    c = pl.program_id(1)
    pairs = RET_HEADS // 2

    @pl.when(c == 0)
    def _init():
        t = lax.broadcasted_iota(jnp.int32, (BLK, BLK), 0)
        s = lax.broadcasted_iota(jnp.int32, (BLK, BLK), 1)
        diff = (t - s).astype(F32)
        lane = lax.broadcasted_iota(jnp.int32, (BLK, LANES), 1)
        pos = lax.broadcasted_iota(jnp.int32, (BLK, LANES), 0).astype(F32)
        col = lax.broadcasted_iota(jnp.int32, (BLK, 2 * RET_DV), 1)
        posv = lax.broadcasted_iota(jnp.int32, (BLK, 2 * RET_DV), 0).astype(F32)
        srow = lax.broadcasted_iota(jnp.int32, (LANES, 2 * RET_DV), 0)
        scol = lax.broadcasted_iota(jnp.int32, (LANES, 2 * RET_DV), 1)
        own = (srow // RET_DK) == (scol // RET_DV)
        for p in range(pairs):
            lg = lg_ref[p]
            for i in range(2):
                dec_ref[2 * p + i] = jnp.where(
                    diff >= 0, jnp.exp(lg[i:i + 1, :] * jnp.maximum(diff, 0.0)), 0.0)
            lg_k = jnp.where(lane < RET_DK, lg[0:1, :LANES], lg[1:2, :LANES])
            kdec_ref[p] = jnp.exp(lg_k * (BLK - 1.0 - pos))
            lg_v = jnp.where(col < RET_DV, lg[0:1, :], lg[1:2, :])
            qdec_ref[p] = jnp.exp(lg_v * (posv + 1.0))
            lg_s = jnp.where(scol < RET_DV, lg[0:1, :], lg[1:2, :])
            sdec_ref[2 * p] = jnp.where(own, jnp.exp(lg_s * float(BLK)), 0.0)
            sdec_ref[2 * p + 1] = jnp.where(own, 1.0, 0.0)
        state_ref[...] = jnp.zeros_like(state_ref)

    def decayed_keys_t(k_pair, p):
        return (k_pair.astype(F32) * kdec_ref[p]).T.astype(BF16)

    def chunk(q_all, k_all, v_all, out_rows=slice(None)):
        lane = lax.broadcasted_iota(jnp.int32, (BLK, LANES), 1)
        qs, ks, crosses, scores, outs = {}, {}, {}, {}, {}

        def pair_start(p):
            qs[p] = q_all[:, p * LANES:(p + 1) * LANES]
            ks[p] = k_all[:, p * LANES:(p + 1) * LANES]
            crosses[p] = _dot(qs[p], state_ref[p].astype(BF16)) * qdec_ref[p]

        def head_scores(head):
            p, i = divmod(head, 2)
            in_head = (lane < RET_DK) if i == 0 else (lane >= RET_DK)
            qi = jnp.where(in_head, qs[p], jnp.zeros_like(qs[p]))
            scores[head] = (_dot_nt(qi, ks[p]) * dec_ref[head]).astype(BF16)

        def head_values(head):
            p, i = divmod(head, 2)
            o = _dot(scores.pop(head), v_all[:, head * RET_DV:(head + 1) * RET_DV])
            outs[head] = o + crosses[p][:, i * RET_DV:(i + 1) * RET_DV]

        def head_norm(head):
            o = outs.pop(head)
            mu = jnp.mean(o, axis=-1, keepdims=True)
            d = o - mu
            var = jnp.mean(d * d, axis=-1, keepdims=True)
            normed = (d * lax.rsqrt(var + EPS)).astype(BF16)
            o_ref[out_rows, head * RET_DV:(head + 1) * RET_DV] = normed[PAD:] if meta_only else normed

        def pair_state(p):
            v = v_all[:, 2 * p * RET_DV:2 * (p + 1) * RET_DV]
            state_ref[p] = (state_ref[p] * sdec_ref[2 * p]
                            + _dot(decayed_keys_t(ks[p], p), v) * sdec_ref[2 * p + 1])

        for t in range(RET_HEADS + 2):
            if t < RET_HEADS:
                if t % 2 == 0:
                    pair_start(t // 2)
                head_scores(t)
            if 0 <= t - 1 < RET_HEADS:
                head_values(t - 1)
            if 0 <= t - 2 < RET_HEADS:
                head_norm(t - 2)
                if (t - 2) % 2 == 1:
                    pair_state((t - 2) // 2)

    if meta_only:
        chunk(*(_meta_chunk(r) for r in refs[:3]))
    else:
        @pl.when(c == 0)
        def _replay_meta_chunk():
            k_m, v_m = _meta_chunk(refs[3]), _meta_chunk(refs[4])
            for p in range(pairs):
                kd_t = decayed_keys_t(k_m[:, p * LANES:(p + 1) * LANES], p)
                v = v_m[:, 2 * p * RET_DV:2 * (p + 1) * RET_DV]
                state_ref[p] = _dot(kd_t, v) * sdec_ref[2 * p + 1]
            o_ref[...] = jnp.zeros_like(o_ref)

        @pl.when(c > 0)
        def _real_chunks():
            for i in range(refs[0].shape[0] // BLK):
                rows = slice(i * BLK, (i + 1) * BLK)
                chunk(refs[0][rows, :], refs[1][rows, :], refs[2][rows, :], rows)


def _retention(lg_tab, rq, rk, rv, batch):
    rows = rq.shape[0]
    pairs = RET_HEADS // 2
    qk_w, v_w = RET_HEADS * RET_DK, RET_HEADS * RET_DV
    real_rows = rows - N_META
    per_seq = real_rows // BLK // batch
    meta_rows = lambda *_: (real_rows // N_META, 0)
    meta_specs = [pl.BlockSpec((N_META, w), meta_rows) for w in (qk_w, qk_w, v_w)]
    lg_spec = pl.BlockSpec((pairs, 2, BLK), lambda b, c: (0, 0, 0))
    scratch = [
        pltpu.VMEM((pairs, LANES, 2 * RET_DV), F32),
        pltpu.VMEM((RET_HEADS, BLK, BLK), F32),
        pltpu.VMEM((pairs, BLK, LANES), F32),
        pltpu.VMEM((pairs, BLK, 2 * RET_DV), F32),
        pltpu.VMEM((2 * pairs, LANES, 2 * RET_DV), F32),
    ]
    out_shape = jax.ShapeDtypeStruct((rows, v_w), BF16)
    out = pl.pallas_call(
        functools.partial(_ret_kernel, meta_only=True),
        grid=(1, 1),
        in_specs=[lg_spec] + meta_specs,
        out_specs=pl.BlockSpec((N_META, v_w), meta_rows),
        out_shape=out_shape,
        scratch_shapes=scratch,
        compiler_params=_params("arbitrary", "arbitrary"),
        name="retention_meta",
    )(lg_tab, rq, rk, rv)

    assert per_seq % RET_CHUNKS == 0
    steps = per_seq // RET_CHUNKS
    step_rows = RET_CHUNKS * BLK

    def real_rows_of(b, c):
        return b * steps + jnp.maximum(c - 1, 0), 0

    return pl.pallas_call(
        functools.partial(_ret_kernel, meta_only=False),
        grid=(batch, steps + 1),
        in_specs=[lg_spec] + [pl.BlockSpec((step_rows, w), real_rows_of) for w in (qk_w, qk_w, v_w)]
        + meta_specs[1:] + [pl.BlockSpec(memory_space=pl.ANY)],
        out_specs=pl.BlockSpec((step_rows, v_w), real_rows_of),
        out_shape=out_shape,
        input_output_aliases={6: 0},
        scratch_shapes=scratch,
        compiler_params=_params("parallel", "arbitrary"),
        name="retention",
    )(lg_tab, rq, rk, rv, rk, rv, out)


def _sb_kernel(q_ref, km_ref, vm_ref, *rest, meta_only):
    o_ref, k_ref, v_ref, acc_ref, carry_ref = rest[-5:]
    step_id = pl.program_id(1)

    @pl.when(step_id == 0)
    def _assemble_sequence():
        k_ref[0:BLK] = _meta_chunk(km_ref)
        v_ref[0:BLK] = _meta_chunk(vm_ref)
        if not meta_only:
            k_ref[BLK:] = rest[0][...]
            v_ref[BLK:] = rest[1][...]

    pairs = SB_HEADS // 2
    lane = lax.broadcasted_iota(jnp.int32, (BLK, LANES), 1)
    q_all = _meta_chunk(q_ref) if meta_only else q_ref[...]
    q_heads = []
    for qi in range(q_all.shape[0] // BLK):
        for p in range(pairs):
            q = q_all[qi * BLK:(qi + 1) * BLK, p * LANES:(p + 1) * LANES]
            zero = jnp.zeros_like(q)
            q_heads += [jnp.where(lane < SB_DH, q, zero), jnp.where(lane >= SB_DH, q, zero)]
    j = lax.broadcasted_iota(jnp.int32, (BLK, BLK), 0)
    c = lax.broadcasted_iota(jnp.int32, (BLK, BLK), 1)
    suffix = jnp.where(j > c, 1.0, 0.0).astype(BF16)
    key_col = lax.broadcasted_iota(jnp.int32, (1, BLK), 1)

    def step(blocks, first):
        chains = [(qi * SB_HEADS + h, kb, bias, h) for qi, kb, bias in blocks for h in range(SB_HEADS)]
        slots = sorted({slot for slot, _, _, _ in chains})
        owns, logs, firsts, sums = {}, {}, {}, {}
        carries = {slot: None if first else carry_ref[slot] for slot in slots}
        accs = {slot: None for slot in slots}

        def soft_log(y):
            return jnp.minimum(y, 0.0) - jnp.log2(1.0 + jnp.exp2(-jnp.abs(y)))

        def rows_of(kb):
            start = kb * BLK
            return pl.ds(start if isinstance(start, int) else pl.multiple_of(start, BLK), BLK)

        def front(n):
            slot, kb, bias, h = chains[n]
            cols = slice((h // 2) * LANES, (h // 2 + 1) * LANES)
            y = _dot_nt(q_heads[slot], k_ref[rows_of(kb), cols])
            if bias is not None:
                y = y + bias
            log_1mb = soft_log(y)
            owns[n], logs[n] = log_1mb - y, log_1mb.astype(BF16)
            firsts[n] = log_1mb[:, 0:1]

        def back(n):
            slot, kb, _, h = chains[n]
            cols = slice((h // 2) * LANES, (h // 2 + 1) * LANES)
            later = sums.pop(n) if carries[slot] is None else sums.pop(n) + carries[slot]
            a = jnp.exp2(owns.pop(n) + later).astype(BF16)
            pv = _dot(a, v_ref[rows_of(kb), cols])
            carries[slot] = later[:, 0:1] + firsts.pop(n)
            accs[slot] = pv if accs[slot] is None else accs[slot] + pv

        for t in range(len(chains) + 2):
            if t < len(chains):
                front(t)
            if 0 <= t - 1 < len(chains):
                sums[t - 1] = _dot(logs.pop(t - 1), suffix)
            if 0 <= t - 2 < len(chains):
                back(t - 2)
        for slot in slots:
            carry_ref[slot] = carries[slot]
            if first:
                acc_ref[slot] = accs[slot]
            else:
                acc_ref[slot] += accs[slot]

    def valid_bias(kb):
        return jnp.where(kb * BLK + key_col < PAD, MASK_BIG, 0.0).astype(F32)

    r = lax.broadcasted_iota(jnp.int32, (BLK, BLK), 0)
    s = lax.broadcasted_iota(jnp.int32, (BLK, BLK), 1)
    causal = jnp.where(s < r, 0.0, MASK_BIG).astype(F32)

    if meta_only:
        step([(0, 0, causal + valid_bias(0))], True)
    else:
        first_q = QUERY_BLOCKS * step_id + 1
        blocks = []
        for qi in range(QUERY_BLOCKS):
            prev_bias = valid_bias(first_q - 1) if qi == 0 else None
            blocks += [(qi, first_q + qi, causal), (qi, first_q + qi - 1, prev_bias)]
        step(blocks, True)

        for qi in range(QUERY_BLOCKS):
            def decayed(qi=qi):
                top = carry_ref[qi * SB_HEADS]
                for h in range(1, SB_HEADS):
                    top = jnp.maximum(top, carry_ref[qi * SB_HEADS + h])
                return jnp.max(top) < UNDERFLOW_LOG2

            def live(state):
                kb, done = state
                return jnp.logical_and(kb >= 0, jnp.logical_not(done))

            def body(state, qi=qi, decayed=decayed):
                kb, _ = state
                step([(qi, kb, valid_bias(kb))], False)
                return kb - 1, decayed()

            lax.while_loop(live, body, (first_q + qi - 2, decayed()))
    for qi in range(q_all.shape[0] // BLK):
        for p in range(pairs):
            slot = qi * SB_HEADS + 2 * p
            out = jnp.where(lane < SB_DH, acc_ref[slot], acc_ref[slot + 1]).astype(BF16)
            rows = slice(None) if meta_only else slice(qi * BLK, (qi + 1) * BLK)
            o_ref[rows, p * LANES:(p + 1) * LANES] = out[PAD:] if meta_only else out


def _stick_breaking(sq, sk, sv, batch):
    rows, width = sq.shape
    assert PAD <= BLK and width == SB_HEADS * SB_DH
    real_rows = rows - N_META
    per_seq = real_rows // BLK // batch
    seq = per_seq * BLK
    meta_spec = pl.BlockSpec((N_META, width), lambda b, i: (real_rows // N_META, 0))
    out_shape = jax.ShapeDtypeStruct((rows, width), BF16)

    def scratch(key_rows, query_blocks):
        slots = query_blocks * SB_HEADS
        return [pltpu.VMEM((key_rows, width), BF16), pltpu.VMEM((key_rows, width), BF16),
                pltpu.VMEM((slots, BLK, LANES), F32), pltpu.VMEM((slots, BLK, 1), F32)]

    out = pl.pallas_call(
        functools.partial(_sb_kernel, meta_only=True),
        grid=(1, 1),
        in_specs=[meta_spec] * 3,
        out_specs=meta_spec,
        out_shape=out_shape,
        scratch_shapes=scratch(BLK, 1),
        compiler_params=_params("arbitrary", "arbitrary"),
        name="stick_breaking_meta",
    )(sq, sk, sv)
    assert per_seq % QUERY_BLOCKS == 0
    steps = per_seq // QUERY_BLOCKS
    q_spec = pl.BlockSpec((QUERY_BLOCKS * BLK, width), lambda b, i: (b * steps + i, 0))
    real_spec = pl.BlockSpec((seq, width), lambda b, i: (b, 0))
    return pl.pallas_call(
        functools.partial(_sb_kernel, meta_only=False),
        grid=(batch, steps),
        in_specs=[q_spec, meta_spec, meta_spec, real_spec, real_spec, pl.BlockSpec(memory_space=pl.ANY)],
        out_specs=q_spec,
        out_shape=out_shape,
        input_output_aliases={5: 0},
        scratch_shapes=scratch(seq + BLK, QUERY_BLOCKS),
        compiler_params=_params("parallel", "arbitrary"),
        name="stick_breaking",
    )(sq, sk, sv, sk, sv, out)


def _merge_kernel(x_ref, g_ref, wrg_ref, wga0_ref, wga1_ref, wgb0_ref, wgb1_ref, gn_ref,
                  or_ref, os_ref, wr_ref, ws_ref, wo_ref, o_ref):
    x = x_ref[...]
    u = _rmsnorm(x, g_ref[...]).astype(BF16)

    def gate(*w_refs):
        return jnp.concatenate([_dot(u, w[...].astype(BF16)) for w in w_refs], axis=1)

    rg = gate(wrg_ref)
    y_sb = _dot(os_ref[...], ws_ref[...].astype(BF16))
    gb = _sigmoid(gate(wgb0_ref, wgb1_ref)) * y_sb
    ga = _sigmoid(gate(wga0_ref, wga1_ref))
    gated = (rg * _sigmoid(rg)) * (or_ref[...].astype(F32) * gn_ref[...])
    y_ret = _dot(gated.astype(BF16), wr_ref[...].astype(BF16))
    y = ga * y_ret + gb
    o_ref[...] = x + _dot(y.astype(BF16), wo_ref[...].astype(BF16))


def _merge(h, layer, g, w_in, gn, o_r, o_s, w_ret, w_sb, w_out, *, tile, out_rows):
    def in_spec(width):
        return pl.BlockSpec((tile, width), lambda i: (i, 0))

    half = D_MODEL // 2
    assert C_RG % D_MODEL == 0 and C_GA % half == 0
    ret_w, sb_w = RET_HEADS * RET_DV, SB_HEADS * SB_DH
    return pl.pallas_call(
        _merge_kernel,
        grid=(out_rows // tile,),
        in_specs=[
            in_spec(D_MODEL), _layer_spec(layer, (1, D_MODEL)),
            _layer_spec(layer, (D_MODEL, D_MODEL), col_block=C_RG // D_MODEL),
            _layer_spec(layer, (D_MODEL, half), col_block=C_GA // half),
            _layer_spec(layer, (D_MODEL, half), col_block=C_GA // half + 1),
            _layer_spec(layer, (D_MODEL, half), col_block=C_GB // half),
            _layer_spec(layer, (D_MODEL, half), col_block=C_GB // half + 1),
            _layer_spec(layer, (1, ret_w)), in_spec(ret_w), in_spec(sb_w),
            _layer_spec(layer, (ret_w, D_MODEL)), _layer_spec(layer, (sb_w, D_MODEL)),
            _layer_spec(layer, (D_MODEL, D_MODEL)),
        ],
        out_specs=pl.BlockSpec((tile, D_MODEL), lambda i: (i, 0)),
        out_shape=jax.ShapeDtypeStruct((out_rows, D_MODEL), F32),
        compiler_params=_params("parallel"),
        name="mixer_merge",
    )(h, g, w_in, w_in, w_in, w_in, w_in, gn, o_r, o_s, w_ret, w_sb, w_out)


def kernel(x, meta, ffn1_norm, ffn1_w_gu, ffn1_w_down, mix_norm, w_in, ret_gn, w_ret_proj,
           w_sb_proj, w_out, ffn2_norm, ffn2_w_gu, ffn2_w_down, final_norm):
    batch, seq, d = x.shape
    real_rows = batch * seq
    rows = real_rows + N_META
    assert d == D_MODEL and seq % BLK == 0 and real_rows % ROW_TILE == 0 and rows % MID_TILE == 0
    assert meta.shape == (N_META, d) and w_in.shape == (DEPTH, d, C_END)
    assert ffn1_w_gu.shape == (DEPTH, d, 2 * D_FF) and ffn1_w_down.shape == (DEPTH, D_FF, d)

    f32 = np.float32
    pos = np.concatenate([np.tile(np.arange(seq) + N_META, batch), np.arange(N_META)]).astype(f32)
    freqs = f32(ROPE_BASE) ** (-np.arange(0, RET_DK, 2, dtype=f32) / f32(RET_DK))
    ang = pos[:, None] * freqs[None, :]
    cos_h, sin_h = np.cos(ang).astype(f32), np.sin(ang).astype(f32)
    cos_t = jnp.asarray(np.concatenate([cos_h, cos_h] * 2, axis=1))
    sin_t = jnp.asarray(np.concatenate([-sin_h, sin_h] * 2, axis=1))
    log_gamma = np.log(f32(1.0) - f32(2.0) ** (-5.0 - np.arange(RET_HEADS, dtype=f32))).astype(f32)
    lg_tab = jnp.asarray(np.broadcast_to(log_gamma.reshape(RET_HEADS // 2, 2, 1), (RET_HEADS // 2, 2, BLK)))

    gains = lambda t: t.reshape(t.shape[0], 1, t.shape[1])
    final_gain = final_norm.reshape(1, d)
    for l in range(DEPTH):
        last = l == DEPTH - 1
        ffn1 = functools.partial(_ffn, layer=l, g=gains(ffn1_norm), w_gu=ffn1_w_gu, w_down=ffn1_w_down,
                                 gf=final_gain, final_norm=False)
        if l == 0:
            h = ffn1(x.reshape(real_rows, d), tile=ROW_TILE, out_rows=rows)
            h = ffn1(meta.astype(x.dtype), tile=N_META, out_rows=rows,
                     out_offset=real_rows // N_META, filled=(h,))
        else:
            h = ffn1(h, tile=MID_TILE)
        rq, rk, rv, sq, sk, sv = _proj(h, l, gains(mix_norm), w_in, cos_t, sin_t, tile=MID_TILE)
        o_r = _retention(lg_tab, rq, rk, rv, batch)
        o_s = _stick_breaking(sq, sk, sv, batch)
        h = _merge(h, l, gains(mix_norm), w_in, gains(ret_gn), o_r, o_s, w_ret_proj, w_sb_proj, w_out,
                   tile=MID_TILE, out_rows=rows)
        h = _ffn(h, l, gains(ffn2_norm), ffn2_w_gu, ffn2_w_down, final_gain, final_norm=last,
                 tile=MID_TILE, out_rows=real_rows if last else rows)
    return h.reshape(batch, seq, d)
```
